```python
import jax
import jax.numpy as jnp
from jax import lax
import numpy as np

D_MODEL = 2048
BATCH = 2
SEQ = 8192
DEPTH = 2

ATT_HEADS = 16
ATT_KV_HEADS = 4
ATT_HEAD_DIM = 128
IDX_HEADS = 16
IDX_HEAD_DIM = 128
TOPK_MAX = 256
Q_BLOCK = 128
GLA_HEADS = 4
GLA_DK = D_MODEL // (2 * GLA_HEADS)
GLA_DV = D_MODEL // GLA_HEADS
GLA_GATE_RANK = 16
GLA_GATE_TAU = 16.0
GLA_CHUNK = 64
D_FF = 7 * D_MODEL // 2
N_EXPERTS = 8
TOP_K_EXPERTS = 2
N_MOD = 6
EPS = 1e-6

SPLIT_SIZES = (
    ATT_HEADS * ATT_HEAD_DIM,
    ATT_KV_HEADS * ATT_HEAD_DIM,
    ATT_KV_HEADS * ATT_HEAD_DIM,
    IDX_HEADS * IDX_HEAD_DIM,
    IDX_HEAD_DIM,
    IDX_HEADS,
    GLA_HEADS * GLA_DK,
    GLA_HEADS * GLA_DK,
    GLA_HEADS * GLA_DV,
    GLA_GATE_RANK,
    GLA_HEADS * GLA_DV,
    D_MODEL,
    D_MODEL,
)
D_IN = sum(SPLIT_SIZES)

kernel_name = 'hybrid_dsa_gla_moe_adaln'


def rms_norm(x, gain):
    xf = x.astype(jnp.float32)
    y = xf * lax.rsqrt(jnp.mean(xf * xf, axis=-1, keepdims=True) + EPS)
    return (y * gain.astype(jnp.float32)).astype(x.dtype)


def modulate(x, gain, shift, scale):
    return rms_norm(x, gain) * (1 + scale[:, None, :]) + shift[:, None, :]


def split_columns(a):
    pts, acc = [], 0
    for n in SPLIT_SIZES[:-1]:
        acc += n
        pts.append(acc)
    return jnp.split(a, pts, axis=-1)


def dsa_attention(q, k, v, q_idx, k_idx, w_idx):
    B, S = q.shape[0], q.shape[1]
    topk = min(TOPK_MAX, S // 4)
    nb = S // Q_BLOCK
    group = ATT_HEADS // ATT_KV_HEADS
    att_scale = ATT_HEAD_DIM ** -0.5
    idx_scale = (IDX_HEAD_DIM ** -0.5) * (IDX_HEADS ** -0.5)
    kpos = jnp.arange(S, dtype=jnp.int32)

    def to_blocks(a):
        return jnp.moveaxis(a.reshape((B, nb, Q_BLOCK) + a.shape[2:]), 1, 0)

    def block(args):
        start, qb, qib, wb = args
        qpos = start + jnp.arange(Q_BLOCK, dtype=jnp.int32)
        causal = kpos[None, :] <= qpos[:, None]
        logits = jnp.einsum('bqhd,bsd->bqhs', qib, k_idx, preferred_element_type=jnp.float32)
        score = jnp.einsum('bqh,bqhs->bqs', wb.astype(jnp.float32), jax.nn.relu(logits)) * idx_scale
        score = jnp.where(causal[None], score, -jnp.inf)
        _, sel = lax.top_k(score, topk)
        valid = sel <= qpos[None, :, None]
        kg = jax.vmap(lambda kk, ii: kk[ii])(k, sel)
        vg = jax.vmap(lambda vv, ii: vv[ii])(v, sel)
        qg = qb.reshape(B, Q_BLOCK, ATT_KV_HEADS, group, ATT_HEAD_DIM)
        s = jnp.einsum('bqgrd,bqkgd->bqgrk', qg, kg, preferred_element_type=jnp.float32) * att_scale
        s = jnp.where(valid[:, :, None, None, :], s, -jnp.inf)
        p = jax.nn.softmax(s, axis=-1)
        o = jnp.einsum('bqgrk,bqkgd->bqgrd', p.astype(vg.dtype), vg)
        return o.reshape(B, Q_BLOCK, ATT_HEADS * ATT_HEAD_DIM)

    starts = jnp.arange(nb, dtype=jnp.int32) * Q_BLOCK
    out = lax.map(block, (starts, to_blocks(q), to_blocks(q_idx), to_blocks(w_idx)))
    return jnp.moveaxis(out, 0, 1).reshape(B, S, ATT_HEADS * ATT_HEAD_DIM)


def gla_chunked(q, k, v, log_a):
    B, S, H, dk = q.shape
    dv = v.shape[-1]
    C = GLA_CHUNK
    nc = S // C
    f32 = jnp.float32
    q = q.astype(f32) * (dk ** -0.5)
    k = k.astype(f32)
    v = v.astype(f32)
    g = log_a.astype(f32)

    def chunks(a):
        return a.reshape(B, nc, C, H, a.shape[-1])

    qc, kc, vc, gc = chunks(q), chunks(k), chunks(v), chunks(g)
    b = jnp.cumsum(gc, axis=2)
    b_last = b[:, :, -1:]
    q_dec = qc * jnp.exp(b)
    k_inv = kc * jnp.exp(-b)
    k_tail = kc * jnp.exp(b_last - b)
    A = jnp.einsum('bnchk,bndhk->bnhcd', q_dec, k_inv)
    mask = jnp.tril(jnp.ones((C, C), dtype=bool))
    A = jnp.where(mask, A, 0.0)
    o_intra = jnp.einsum('bnhcd,bndhv->bnchv', A, vc)
    decay = jnp.exp(b_last[:, :, 0])

    def step(state, inp):
        q_n, kt_n, v_n, d_n = inp
        o = jnp.einsum('bchk,bhkv->bchv', q_n, state)
        state = d_n[..., None] * state + jnp.einsum('bchk,bchv->bhkv', kt_n, v_n)
        return state, o

    xs = (jnp.moveaxis(q_dec, 1, 0), jnp.moveaxis(k_tail, 1, 0),
          jnp.moveaxis(vc, 1, 0), jnp.moveaxis(decay, 1, 0))
    _, o_inter = lax.scan(step, jnp.zeros((B, H, dk, dv), f32), xs)
    o = o_intra + jnp.moveaxis(o_inter, 0, 1)
    return o.reshape(B, S, H, dv)


def hybrid_mixer(h, w_in, q_norm, k_norm, idx_k_norm, w_gla_gate, b_gla_gate, gla_out_norm,
                 w_out_attn, w_out_gla, w_out):
    B, S, _ = h.shape
    proj = jnp.einsum('bsd,de->bse', h, w_in)
    (aq, ak, av, iq, ik, iw, gq, gk, gv, glr, gr, m_att, m_gla) = split_columns(proj)
    aq = rms_norm(aq.reshape(B, S, ATT_HEADS, ATT_HEAD_DIM), q_norm)
    ak = rms_norm(ak.reshape(B, S, ATT_KV_HEADS, ATT_HEAD_DIM), k_norm)
    av = av.reshape(B, S, ATT_KV_HEADS, ATT_HEAD_DIM)
    iq = iq.reshape(B, S, IDX_HEADS, IDX_HEAD_DIM)
    ik = rms_norm(ik, idx_k_norm)
    y_att = dsa_attention(aq, ak, av, iq, ik, iw)
    gate_logits = jnp.einsum('bsr,rk->bsk', glr, w_gla_gate) + b_gla_gate
    log_a = jax.nn.log_sigmoid(gate_logits.astype(jnp.float32)) / GLA_GATE_TAU
    o = gla_chunked(gq.reshape(B, S, GLA_HEADS, GLA_DK), gk.reshape(B, S, GLA_HEADS, GLA_DK),
                    gv.reshape(B, S, GLA_HEADS, GLA_DV), log_a.reshape(B, S, GLA_HEADS, GLA_DK))
    o = rms_norm(o.astype(h.dtype), gla_out_norm) * jax.nn.silu(gr.reshape(B, S, GLA_HEADS, GLA_DV))
    y_gla = o.reshape(B, S, GLA_HEADS * GLA_DV)
    merged = (jax.nn.sigmoid(m_att) * jnp.einsum('bse,ed->bsd', y_att, w_out_attn)
              + jax.nn.sigmoid(m_gla) * jnp.einsum('bse,ed->bsd', y_gla, w_out_gla))
    return jnp.einsum('bsd,de->bse', merged, w_out)


def swiglu(t, w1, w3, w2):
    return jnp.matmul(jax.nn.silu(jnp.matmul(t, w1)) * jnp.matmul(t, w3), w2)


def moe_swiglu(h, w_router, w1, w3, w2):
    B, S, D = h.shape
    t = h.reshape(B * S, D)
    logits = jnp.matmul(t, w_router).astype(jnp.float32)
    top_val, top_idx = lax.top_k(logits, TOP_K_EXPERTS)
    top_w = jax.nn.softmax(top_val, axis=-1)
    combine = jnp.sum(jax.nn.one_hot(top_idx, N_EXPERTS, dtype=jnp.float32) * top_w[..., None], axis=1)
    combine = combine.astype(t.dtype)
    y = jnp.zeros_like(t)
    for e in range(N_EXPERTS):
        y = y + combine[:, e:e + 1] * swiglu(t, w1[e], w3[e], w2[e])
    return y.reshape(B, S, D)


def setup_inputs(seed: int = 0) -> dict:
    key = jax.random.key(seed)
    ks = jax.random.split(key, 24)
    f32 = jnp.float32

    def nrm(k, shape, s):
        return jax.random.normal(k, shape, f32) * s

    n_dense = (DEPTH + 1) // 2
    n_moe = DEPTH // 2
    return {
        'x': nrm(ks[0], (BATCH, SEQ, D_MODEL), 1.0),
        'c': nrm(ks[1], (BATCH, D_MODEL), 1.0),
        'ada_w': nrm(ks[2], (DEPTH, D_MODEL, N_MOD * D_MODEL), D_MODEL ** -0.5),
        'ada_b': nrm(ks[3], (DEPTH, N_MOD * D_MODEL), 0.02),
        'norm_mix': 1.0 + nrm(ks[4], (DEPTH, D_MODEL), 0.02),
        'norm_ffn': 1.0 + nrm(ks[5], (DEPTH, D_MODEL), 0.02),
        'w_in': nrm(ks[6], (DEPTH, D_MODEL, D_IN), D_MODEL ** -0.5),
        'q_norm': 1.0 + nrm(ks[7], (DEPTH, ATT_HEAD_DIM), 0.02),
        'k_norm': 1.0 + nrm(ks[8], (DEPTH, ATT_HEAD_DIM), 0.02),
        'idx_k_norm': 1.0 + nrm(ks[9], (DEPTH, IDX_HEAD_DIM), 0.02),
        'w_gla_gate': nrm(ks[10], (DEPTH, GLA_GATE_RANK, GLA_HEADS * GLA_DK), GLA_GATE_RANK ** -0.5),
        'b_gla_gate': nrm(ks[11], (DEPTH, GLA_HEADS * GLA_DK), 0.1),
        'gla_out_norm': 1.0 + nrm(ks[12], (DEPTH, GLA_DV), 0.02),
        'w_out_attn': nrm(ks[13], (DEPTH, ATT_HEADS * ATT_HEAD_DIM, D_MODEL), (ATT_HEADS * ATT_HEAD_DIM) ** -0.5),
        'w_out_gla': nrm(ks[14], (DEPTH, GLA_HEADS * GLA_DV, D_MODEL), (GLA_HEADS * GLA_DV) ** -0.5),
        'w_out': nrm(ks[15], (DEPTH, D_MODEL, D_MODEL), D_MODEL ** -0.5),
        'ffn_w1': nrm(ks[16], (n_dense, D_MODEL, D_FF), D_MODEL ** -0.5),
        'ffn_w3': nrm(ks[17], (n_dense, D_MODEL, D_FF), D_MODEL ** -0.5),
        'ffn_w2': nrm(ks[18], (n_dense, D_FF, D_MODEL), D_FF ** -0.5),
        'moe_router': nrm(ks[19], (n_moe, D_MODEL, N_EXPERTS), D_MODEL ** -0.5),
        'moe_w1': nrm(ks[20], (n_moe, N_EXPERTS, D_MODEL, D_FF), D_MODEL ** -0.5),
        'moe_w3': nrm(ks[21], (n_moe, N_EXPERTS, D_MODEL, D_FF), D_MODEL ** -0.5),
        'moe_w2': nrm(ks[22], (n_moe, N_EXPERTS, D_FF, D_MODEL), D_FF ** -0.5),
    }


def reference(x, c, ada_w, ada_b, norm_mix, norm_ffn, w_in, q_norm, k_norm, idx_k_norm,
              w_gla_gate, b_gla_gate, gla_out_norm, w_out_attn, w_out_gla, w_out,
              ffn_w1, ffn_w3, ffn_w2, moe_router, moe_w1, moe_w3, moe_w2):
    B, _, D = x.shape
    cond = jax.nn.silu(c)
    for layer in range(DEPTH):
        mod = (jnp.matmul(cond, ada_w[layer]) + ada_b[layer]).reshape(B, N_MOD, D)
        shift_m, scale_m, gate_m = mod[:, 0], mod[:, 1], mod[:, 2]
        shift_f, scale_f, gate_f = mod[:, 3], mod[:, 4], mod[:, 5]
        h = modulate(x, norm_mix[layer], shift_m, scale_m)
        y = hybrid_mixer(h, w_in[layer], q_norm[layer], k_norm[layer], idx_k_norm[layer],
                         w_gla_gate[layer], b_gla_gate[layer], gla_out_norm[layer],
                         w_out_attn[layer], w_out_gla[layer], w_out[layer])
        x = x + gate_m[:, None, :] * y
        h = modulate(x, norm_ffn[layer], shift_f, scale_f)
        if layer % 2 == 0:
            i = layer // 2
            f = swiglu(h, ffn_w1[i], ffn_w3[i], ffn_w2[i])
        else:
            i = layer // 2
            f = moe_swiglu(h, moe_router[i], moe_w1[i], moe_w3[i], moe_w2[i])
        x = x + gate_f[:, None, :] * f
    return x
```

```python
import functools

import jax
import jax.numpy as jnp
from jax import lax
from jax.experimental import pallas as pl
from jax.experimental.pallas import tpu as pltpu

F32 = jnp.float32
BF16 = jnp.bfloat16

ATT_HEADS = 16
ATT_KV_HEADS = 4
HEAD_DIM = 128
IDX_HEADS = 16
TOPK_MAX = 256
GLA_HEADS = 4
GLA_GATE_RANK = 16
GLA_GATE_TAU = 16.0
GLA_CHUNK = 64
N_EXPERTS = 8
EPS = 1e-6
LANES = 128
NEG_BIG = -1e30
INT_MIN = -(2 ** 31)

PROJ_TN = 512
VMEM_LIMIT = 56 * 1024 * 1024


def _nt_dot(a, b):
    return lax.dot_general(a, b, (((1,), (1,)), ((), ())), preferred_element_type=F32)


def _rms(a):
    return a * lax.rsqrt(jnp.mean(a * a, axis=-1, keepdims=True) + EPS)


def _sigmoid(a):
    return 1.0 / (1.0 + jnp.exp(-a))


def _modulate(x, g, scale, shift):
    return _rms(x) * g * (1.0 + scale) + shift


def _ada_kernel(c_ref, w_ref, b_ref, o_ref):
    c = c_ref[...]
    cond = c * _sigmoid(c)
    o_ref[...] = jnp.dot(cond, w_ref[...], preferred_element_type=F32,
                         precision=lax.Precision.HIGHEST) + b_ref[...]


def _ada(c8, ada_w, ada_b):
    depth, d, n = ada_w.shape
    tn = 1024
    return pl.pallas_call(
        _ada_kernel,
        grid=(depth, n // tn),
        in_specs=[
            pl.BlockSpec((8, d), lambda l, j: (0, 0)),
            pl.BlockSpec((None, d, tn), lambda l, j: (l, 0, j)),
            pl.BlockSpec((None, 1, tn), lambda l, j: (l, 0, j)),
        ],
        out_specs=pl.BlockSpec((None, 8, tn), lambda l, j: (l, 0, j)),
        out_shape=jax.ShapeDtypeStruct((depth, 8, n), F32),
        name="ada_mod",
    )(c8, ada_w, ada_b.reshape(depth, 1, n))


def _modproj_kernel(x_ref, g_ref, sc_ref, sh_ref, w_ref, cg_ref, w2_ref, cg2_ref,
                    o_ref, ik_ref, sm_ref, hn_ref, *, norm_tiles, sig_start):
    j = pl.program_id(1)

    @pl.when(j == 0)
    def _():
        h = _modulate(x_ref[...], g_ref[...], sc_ref[...], sh_ref[...]).astype(BF16)
        hn_ref[...] = h
        small = jnp.dot(h, w2_ref[...], preferred_element_type=F32)
        ik_ref[...] = (_rms(small[:, :LANES]) * cg2_ref[:, :LANES]).astype(BF16)
        sm_ref[...] = small[:, LANES:] * cg2_ref[:, LANES:]

    acc = jnp.dot(hn_ref[...], w_ref[...], preferred_element_type=F32)
    is_norm = functools.reduce(jnp.logical_or, [j == t for t in norm_tiles])
    is_sig = j >= sig_start

    @pl.when(is_norm)
    def _():
        for c in range(PROJ_TN // LANES):
            sl = slice(c * LANES, (c + 1) * LANES)
            o_ref[:, sl] = (_rms(acc[:, sl]) * cg_ref[:, sl]).astype(BF16)

    @pl.when(is_sig)
    def _():
        o_ref[...] = _sigmoid(acc).astype(BF16)

    @pl.when(jnp.logical_not(jnp.logical_or(is_norm, is_sig)))
    def _():
        o_ref[...] = acc.astype(BF16)


def _modproj(x2, g, scale, shift, w, cg, w2, cg2, seq, norm_tiles, sig_start):
    n, d = x2.shape
    ncols = w.shape[1]
    tm = min(1024, seq)
    per_b = seq // tm
    kern = functools.partial(_modproj_kernel, norm_tiles=norm_tiles, sig_start=sig_start)
    return pl.pallas_call(
        kern,
        grid=(n // tm, ncols // PROJ_TN),
        in_specs=[
            pl.BlockSpec((tm, d), lambda i, j: (i, 0)),
            pl.BlockSpec((1, d), lambda i, j: (0, 0)),
            pl.BlockSpec((None, 1, d), lambda i, j: (i // per_b, 0, 0)),
            pl.BlockSpec((None, 1, d), lambda i, j: (i // per_b, 0, 0)),
            pl.BlockSpec((d, PROJ_TN), lambda i, j: (0, j)),
            pl.BlockSpec((1, PROJ_TN), lambda i, j: (0, j)),
            pl.BlockSpec((d, 2 * LANES), lambda i, j: (0, 0)),
            pl.BlockSpec((1, 2 * LANES), lambda i, j: (0, 0)),
        ],
        out_specs=[
            pl.BlockSpec((tm, PROJ_TN), lambda i, j: (i, j)),
            pl.BlockSpec((tm, LANES), lambda i, j: (i, 0)),
            pl.BlockSpec((tm, LANES), lambda i, j: (i, 0)),
        ],
        out_shape=[
            jax.ShapeDtypeStruct((n, ncols), BF16),
            jax.ShapeDtypeStruct((n, LANES), BF16),
            jax.ShapeDtypeStruct((n, LANES), F32),
        ],
        scratch_shapes=[pltpu.VMEM((tm, d), BF16)],
        compiler_params=pltpu.CompilerParams(
            dimension_semantics=("arbitrary", "arbitrary"), vmem_limit_bytes=VMEM_LIMIT),
        name="modproj",
    )(x2, g, scale, shift, w, cg, w2, cg2)


ATT_TQ = 128
ATT_TK = 256


def _sortable(a):
    a = jnp.where(a == 0.0, 0.0, a)
    bits = pltpu.bitcast(a, jnp.int32)
    return bits ^ ((bits >> 31) & 0x7FFFFFFF)


def _attn_kernel(aq_ref, iq_ref, ak_ref, av_ref, ik_ref, iw_ref, o_ref,
                 sc_ref, qs_ref, m_ref, l_ref, acc_ref, *, topk):
    tq, tk = ATT_TQ, ATT_TK
    qi = pl.program_id(1)
    n_kt = (qi * tq + tq + tk - 1) // tk
    row = qi * tq + lax.broadcasted_iota(jnp.int32, (tq, 1), 0)
    lane_col = lax.broadcasted_iota(jnp.int32, (1, tk), 1)
    w = iw_ref[:, 0:IDX_HEADS]

    def score_tile(kt, carry):
        kblk = ik_ref[pl.ds(pl.multiple_of(kt * tk, tk), tk), :]
        acc = jnp.zeros((tq, tk), F32)
        for h in range(IDX_HEADS):
            z = _nt_dot(iq_ref[:, h * HEAD_DIM:(h + 1) * HEAD_DIM], kblk)
            acc = acc + w[:, h:h + 1] * jnp.maximum(z, 0.0)
        causal = (kt * tk + lane_col) <= row
        acc = jnp.where(causal, acc, -jnp.inf)
        sc_ref[kt] = _sortable(acc)
        return carry

    lax.fori_loop(0, n_kt, score_tile, 0)

    def count_ge(t):
        def body(kt, c):
            hit = jnp.where(sc_ref[kt] >= t, 1.0, 0.0)
            for cc in range(tk // LANES):
                c = c + hit[:, cc * LANES:(cc + 1) * LANES]
            return c
        c = lax.fori_loop(0, n_kt, body, jnp.zeros((tq, LANES), F32))
        return jnp.sum(c, axis=-1, keepdims=True)

    kf = float(topk)
    t0 = jnp.where(count_ge(jnp.zeros((tq, 1), jnp.int32)) >= kf, 0, INT_MIN).astype(jnp.int32)

    def bit_step(i, t):
        cand = t | jnp.left_shift(jnp.int32(1), 30 - i)
        return jnp.where(count_ge(cand) >= kf, cand, t)

    thr = lax.fori_loop(0, 31, bit_step, t0)

    for h in range(ATT_HEADS):
        qs_ref[h * tq:(h + 1) * tq, :] = aq_ref[:, h * HEAD_DIM:(h + 1) * HEAD_DIM]

    group = ATT_HEADS // ATT_KV_HEADS
    for g in range(ATT_KV_HEADS):
        q4 = qs_ref[g * group * tq:(g + 1) * group * tq, :]
        m_ref[...] = jnp.full(m_ref.shape, NEG_BIG, F32)
        l_ref[...] = jnp.zeros(l_ref.shape, F32)
        acc_ref[...] = jnp.zeros(acc_ref.shape, F32)

        def kv_tile(kt, carry, g=g, q4=q4):
            r0 = pl.multiple_of(kt * tk, tk)
            kblk = ak_ref[pl.ds(r0, tk), g * HEAD_DIM:(g + 1) * HEAD_DIM]
            vblk = av_ref[pl.ds(r0, tk), g * HEAD_DIM:(g + 1) * HEAD_DIM]
            keep = jnp.logical_and(sc_ref[kt] >= thr, (kt * tk + lane_col) <= row)
            s = _nt_dot(q4, kblk).reshape(group, tq, tk)
            s = jnp.where(keep[None], s, NEG_BIG).reshape(group * tq, tk)
            m_old = m_ref[...]
            m_new = jnp.maximum(m_old, jnp.max(s, axis=-1, keepdims=True))
            alpha = jnp.exp(m_old - m_new)
            p = jnp.exp(s - m_new)
            l_ref[...] = alpha * l_ref[...] + jnp.sum(p, axis=-1, keepdims=True)
            acc_ref[...] = alpha * acc_ref[...] + jnp.dot(p.astype(BF16), vblk,
                                                         preferred_element_type=F32)
            m_ref[...] = m_new
            return carry

        lax.fori_loop(0, n_kt, kv_tile, 0)
        og = acc_ref[...] / l_ref[...]
        for r in range(group):
            h = g * group + r
            o_ref[:, h * HEAD_DIM:(h + 1) * HEAD_DIM] = og[r * tq:(r + 1) * tq, :].astype(BF16)


def _attention(proj, ik, sm, batch, seq, cols):
    n = proj.shape[0]
    tq, tk = ATT_TQ, ATT_TK
    nq = seq // tq
    width = ATT_HEADS * HEAD_DIM
    kvw = ATT_KV_HEADS * HEAD_DIM
    topk = min(TOPK_MAX, seq // 4)
    group = ATT_HEADS // ATT_KV_HEADS
    kern = functools.partial(_attn_kernel, topk=topk)
    once = pl.Buffered(1)
    return pl.pallas_call(
        kern,
        grid=(batch, nq),
        in_specs=[
            pl.BlockSpec((tq, width), lambda b, q: (b * nq + q, cols["aq"] // width)),
            pl.BlockSpec((tq, width), lambda b, q: (b * nq + q, cols["iq"] // width)),
            pl.BlockSpec((seq, kvw), lambda b, q: (b, cols["ak"] // kvw), pipeline_mode=once),
            pl.BlockSpec((seq, kvw), lambda b, q: (b, cols["av"] // kvw), pipeline_mode=once),
            pl.BlockSpec((seq, LANES), lambda b, q: (b, 0), pipeline_mode=once),
            pl.BlockSpec((tq, LANES), lambda b, q: (b * nq + q, 0)),
        ],
        out_specs=pl.BlockSpec((tq, width), lambda b, q: (b * nq + q, 0)),
        out_shape=jax.ShapeDtypeStruct((n, width), BF16),
        scratch_shapes=[
            pltpu.VMEM((seq // tk, tq, tk), jnp.int32),
            pltpu.VMEM((ATT_HEADS * tq, HEAD_DIM), BF16),
            pltpu.VMEM((group * tq, 1), F32),
            pltpu.VMEM((group * tq, 1), F32),
            pltpu.VMEM((group * tq, HEAD_DIM), F32),
        ],
        compiler_params=pltpu.CompilerParams(
            dimension_semantics=("arbitrary", "arbitrary"), vmem_limit_bytes=VMEM_LIMIT),
        name="dsa_attention",
    )(proj, proj, proj, proj, ik, sm)


GLA_T = 512


def _gla_kernel(q_ref, k_ref, v_ref, r_ref, lr_ref, wg_ref, bg_ref, gn_ref, o_ref, st_ref, *, dk):
    c_len = GLA_CHUNK
    hi = lax.Precision.HIGHEST

    @pl.when(pl.program_id(2) == 0)
    def _():
        st_ref[...] = jnp.zeros(st_ref.shape, F32)

    ri = lax.broadcasted_iota(jnp.int32, (c_len, c_len), 0)
    ci = lax.broadcasted_iota(jnp.int32, (c_len, c_len), 1)
    lower = ri >= ci
    tril = jnp.where(lower, 1.0, 0.0).astype(F32)
    dv = v_ref.shape[1]
    qscale = dk ** -0.5

    for c in range(GLA_T // c_len):
        rows = slice(c * c_len, (c + 1) * c_len)
        gl = jnp.dot(lr_ref[rows, :], wg_ref[...], preferred_element_type=F32, precision=hi) + bg_ref[...]
        log_a = (jnp.minimum(gl, 0.0) - jnp.log(1.0 + jnp.exp(-jnp.abs(gl)))) / GLA_GATE_TAU
        b = jnp.dot(tril, log_a, preferred_element_type=F32, precision=hi)
        b_last = b[c_len - 1:c_len, :]
        q = q_ref[rows, :].astype(F32) * qscale
        k = k_ref[rows, :].astype(F32)
        v = v_ref[rows, :]
        q_dec = (q * jnp.exp(b)).astype(BF16)
        k_inv = (k * jnp.exp(-b)).astype(BF16)
        k_tail = k * jnp.exp(b_last - b)
        a = jnp.where(lower, _nt_dot(q_dec, k_inv), 0.0)
        state = st_ref[...]
        o = (jnp.dot(a.astype(BF16), v, preferred_element_type=F32)
             + jnp.dot(q_dec, state.astype(BF16), preferred_element_type=F32))
        dec = jnp.exp(b.T[:, c_len - 1:c_len])
        upd = jnp.dot(k_tail.T.astype(BF16), v, preferred_element_type=F32)
        for cc in range(dv // LANES):
            sl = slice(cc * LANES, (cc + 1) * LANES)
            st_ref[:, sl] = dec * state[:, sl] + upd[:, sl]
        r = r_ref[rows, :].astype(F32)
        o_ref[rows, :] = (_rms(o) * gn_ref[...] * (r * _sigmoid(r))).astype(BF16)


def _gla(proj, sm, wg, bg, gn, batch, seq, cols):
    n = proj.shape[0]
    dk = wg.shape[1] // GLA_HEADS
    dv = gn.shape[1]
    t = min(GLA_T, seq)
    assert t == GLA_T
    ns = seq // t
    kern = functools.partial(_gla_kernel, dk=dk)
    return pl.pallas_call(
        kern,
        grid=(batch, GLA_HEADS, ns),
        in_specs=[
            pl.BlockSpec((t, dk), lambda b, h, s: (b * ns + s, cols["gq"] // dk + h)),
            pl.BlockSpec((t, dk), lambda b, h, s: (b * ns + s, cols["gk"] // dk + h)),
            pl.BlockSpec((t, dv), lambda b, h, s: (b * ns + s, cols["gv"] // dv + h)),
            pl.BlockSpec((t, dv), lambda b, h, s: (b * ns + s, cols["gr"] // dv + h)),
            pl.BlockSpec((t, LANES), lambda b, h, s: (b * ns + s, 0)),
            pl.BlockSpec((LANES, dk), lambda b, h, s: (0, h)),
            pl.BlockSpec((1, dk), lambda b, h, s: (0, h)),
            pl.BlockSpec((1, dv), lambda b, h, s: (0, 0)),
        ],
        out_specs=pl.BlockSpec((t, dv), lambda b, h, s: (b * ns + s, h)),
        out_shape=jax.ShapeDtypeStruct((n, GLA_HEADS * dv), BF16),
        scratch_shapes=[pltpu.VMEM((dk, dv), F32)],
        compiler_params=pltpu.CompilerParams(
            dimension_semantics=("arbitrary", "arbitrary", "arbitrary"),
            vmem_limit_bytes=VMEM_LIMIT),
        name="gla",
    )(proj, proj, proj, proj, sm, wg, bg, gn)


def _merge_kernel(ya_ref, yg_ref, wa_ref, wg_ref, ma_ref, mg_ref, o_ref):
    a = jnp.dot(ya_ref[...], wa_ref[...], preferred_element_type=F32)
    g = jnp.dot(yg_ref[...], wg_ref[...], preferred_element_type=F32)
    o_ref[...] = (ma_ref[...].astype(F32) * a + mg_ref[...].astype(F32) * g).astype(BF16)


def _merge(y_att, y_gla, w_att, w_gla, proj, cols):
    n, d_att = y_att.shape
    d_gla = y_gla.shape[1]
    d = w_att.shape[1]
    tm, tn = 1024, PROJ_TN
    return pl.pallas_call(
        _merge_kernel,
        grid=(n // tm, d // tn),
        in_specs=[
            pl.BlockSpec((tm, d_att), lambda i, j: (i, 0)),
            pl.BlockSpec((tm, d_gla), lambda i, j: (i, 0)),
            pl.BlockSpec((d_att, tn), lambda i, j: (0, j)),
            pl.BlockSpec((d_gla, tn), lambda i, j: (0, j)),
            pl.BlockSpec((tm, tn), lambda i, j: (i, cols["m_att"] // tn + j)),
            pl.BlockSpec((tm, tn), lambda i, j: (i, cols["m_gla"] // tn + j)),
        ],
        out_specs=pl.BlockSpec((tm, tn), lambda i, j: (i, j)),
        out_shape=jax.ShapeDtypeStruct((n, d), BF16),
        compiler_params=pltpu.CompilerParams(
            dimension_semantics=("arbitrary", "arbitrary"), vmem_limit_bytes=VMEM_LIMIT),
        name="merge",
    )(y_att, y_gla, w_att, w_gla, proj, proj)


def _resproj_kernel(m_ref, w_ref, x_ref, gate_ref, o_ref):
    y = jnp.dot(m_ref[...], w_ref[...], preferred_element_type=F32)
    o_ref[...] = x_ref[...] + gate_ref[...] * y


def _resproj(merged, w, x2, gate, seq):
    n, d = x2.shape
    tm, tn = 1024, PROJ_TN
    per_b = seq // tm
    return pl.pallas_call(
        _resproj_kernel,
        grid=(n // tm, d // tn),
        in_specs=[
            pl.BlockSpec((tm, merged.shape[1]), lambda i, j: (i, 0)),
            pl.BlockSpec((merged.shape[1], tn), lambda i, j: (0, j)),
            pl.BlockSpec((tm, tn), lambda i, j: (i, j)),
            pl.BlockSpec((None, 1, tn), lambda i, j: (i // per_b, 0, j)),
        ],
        out_specs=pl.BlockSpec((tm, tn), lambda i, j: (i, j)),
        out_shape=jax.ShapeDtypeStruct((n, d), F32),
        compiler_params=pltpu.CompilerParams(
            dimension_semantics=("arbitrary", "arbitrary"), vmem_limit_bytes=VMEM_LIMIT),
        name="resproj",
    )(merged, w, x2, gate)


FFN_TM = 512
FFN_TF = 512


def _swiglu_step(hn, w1, w3, w2):
    a = jnp.dot(hn, w1, preferred_element_type=F32)
    b = jnp.dot(hn, w3, preferred_element_type=F32)
    return a * _sigmoid(a) * b, w2


def _ffn_kernel(x_ref, g_ref, sc_ref, sh_ref, w1_ref, w3_ref, w2_ref, gate_ref, o_ref,
                hn_ref, acc_ref):
    j = pl.program_id(1)

    @pl.when(j == 0)
    def _():
        hn_ref[...] = _modulate(x_ref[...], g_ref[...], sc_ref[...], sh_ref[...]).astype(BF16)
        acc_ref[...] = jnp.zeros(acc_ref.shape, F32)

    act, w2 = _swiglu_step(hn_ref[...], w1_ref[...], w3_ref[...], w2_ref[...])
    acc_ref[...] += jnp.dot(act.astype(BF16), w2, preferred_element_type=F32)

    @pl.when(j == pl.num_programs(1) - 1)
    def _():
        o_ref[...] = x_ref[...] + gate_ref[...] * acc_ref[...]


def _ffn(x2, g, scale, shift, w1, w3, w2, gate, seq):
    n, d = x2.shape
    dff = w1.shape[1]
    tm, tf = min(FFN_TM, seq), FFN_TF
    per_b = seq // tm
    vec = pl.BlockSpec((None, 1, d), lambda i, j: (i // per_b, 0, 0))
    return pl.pallas_call(
        _ffn_kernel,
        grid=(n // tm, dff // tf),
        in_specs=[
            pl.BlockSpec((tm, d), lambda i, j: (i, 0)),
            pl.BlockSpec((1, d), lambda i, j: (0, 0)),
            vec, vec,
            pl.BlockSpec((d, tf), lambda i, j: (0, j)),
            pl.BlockSpec((d, tf), lambda i, j: (0, j)),
            pl.BlockSpec((tf, d), lambda i, j: (j, 0)),
            vec,
        ],
        out_specs=pl.BlockSpec((tm, d), lambda i, j: (i, 0)),
        out_shape=jax.ShapeDtypeStruct((n, d), F32),
        scratch_shapes=[pltpu.VMEM((tm, d), BF16), pltpu.VMEM((tm, d), F32)],
        compiler_params=pltpu.CompilerParams(
            dimension_semantics=("arbitrary", "arbitrary"), vmem_limit_bytes=VMEM_LIMIT),
        name="ffn_swiglu",
    )(x2, g, scale, shift, w1, w3, w2, gate)


def _moe_kernel(x_ref, g_ref, sc_ref, sh_ref, wr_ref, w1_ref, w3_ref, w2_ref, gate_ref, o_ref,
                hn_ref, acc_ref, comb_ref, ce_ref):
    e = pl.program_id(1)
    j = pl.program_id(2)
    first = jnp.logical_and(e == 0, j == 0)
    last = jnp.logical_and(e == pl.num_programs(1) - 1, j == pl.num_programs(2) - 1)
    lane = lax.broadcasted_iota(jnp.int32, (1, LANES), 1)

    @pl.when(first)
    def _():
        h = _modulate(x_ref[...], g_ref[...], sc_ref[...], sh_ref[...])
        hn_ref[...] = h.astype(BF16)
        acc_ref[...] = jnp.zeros(acc_ref.shape, F32)
        logits = jnp.dot(h, wr_ref[...], preferred_element_type=F32,
                         precision=lax.Precision.HIGHEST)
        logits = jnp.where(lane < N_EXPERTS, logits, -jnp.inf)
        lane_f = lane.astype(F32)
        m1 = jnp.max(logits, axis=-1, keepdims=True)
        i1 = jnp.min(jnp.where(logits == m1, lane_f, float(LANES)), axis=-1, keepdims=True)
        rest = jnp.where(lane_f == i1, -jnp.inf, logits)
        m2 = jnp.max(rest, axis=-1, keepdims=True)
        i2 = jnp.min(jnp.where(rest == m2, lane_f, float(LANES)), axis=-1, keepdims=True)
        e2 = jnp.exp(m2 - m1)
        p1 = 1.0 / (1.0 + e2)
        p2 = e2 / (1.0 + e2)
        comb_ref[...] = jnp.where(lane_f == i1, p1, 0.0) + jnp.where(lane_f == i2, p2, 0.0)

    @pl.when(j == 0)
    def _():
        col = jnp.sum(jnp.where(lane == e, comb_ref[...], 0.0), axis=-1, keepdims=True)
        ce_ref[...] = jnp.broadcast_to(col, ce_ref.shape)

    act, w2 = _swiglu_step(hn_ref[...], w1_ref[...], w3_ref[...], w2_ref[...])
    act = act * ce_ref[:, 0:1]
    acc_ref[...] += jnp.dot(act.astype(BF16), w2, preferred_element_type=F32)

    @pl.when(last)
    def _():
        o_ref[...] = x_ref[...] + gate_ref[...] * acc_ref[...]


def _moe(x2, g, scale, shift, w_router, w1, w3, w2, gate, seq):
    n, d = x2.shape
    n_exp, _, dff = w1.shape
    tm, tf = min(FFN_TM, seq), FFN_TF
    per_b = seq // tm
    vec = pl.BlockSpec((None, 1, d), lambda i, e, j: (i // per_b, 0, 0))
    return pl.pallas_call(
        _moe_kernel,
        grid=(n // tm, n_exp, dff // tf),
        in_specs=[
            pl.BlockSpec((tm, d), lambda i, e, j: (i, 0)),
            pl.BlockSpec((1, d), lambda i, e, j: (0, 0)),
            vec, vec,
            pl.BlockSpec((d, LANES), lambda i, e, j: (0, 0)),
            pl.BlockSpec((None, d, tf), lambda i, e, j: (e, 0, j)),
            pl.BlockSpec((None, d, tf), lambda i, e, j: (e, 0, j)),
            pl.BlockSpec((None, tf, d), lambda i, e, j: (e, j, 0)),
            vec,
        ],
        out_specs=pl.BlockSpec((tm, d), lambda i, e, j: (i, 0)),
        out_shape=jax.ShapeDtypeStruct((n, d), F32),
        scratch_shapes=[pltpu.VMEM((tm, d), BF16), pltpu.VMEM((tm, d), F32),
                        pltpu.VMEM((tm, LANES), F32), pltpu.VMEM((tm, LANES), F32)],
        compiler_params=pltpu.CompilerParams(
            dimension_semantics=("arbitrary", "arbitrary", "arbitrary"),
            vmem_limit_bytes=VMEM_LIMIT),
        name="moe_swiglu",
    )(x2, g, scale, shift, w_router, w1, w3, w2, gate)


def _prep_in_proj(w_in, q_norm, k_norm, idx_k_norm, d):
    sizes = dict(aq=ATT_HEADS * HEAD_DIM, ak=ATT_KV_HEADS * HEAD_DIM, av=ATT_KV_HEADS * HEAD_DIM,
                 iq=IDX_HEADS * HEAD_DIM, ik=HEAD_DIM, iw=IDX_HEADS,
                 gq=d // 2, gk=d // 2, gv=d, glr=GLA_GATE_RANK, gr=d, m_att=d, m_gla=d)
    src, acc = {}, 0
    for name in ("aq", "ak", "av", "iq", "ik", "iw", "gq", "gk", "gv", "glr", "gr", "m_att", "m_gla"):
        src[name] = (acc, sizes[name])
        acc += sizes[name]
    order = ("aq", "iq", "ak", "av", "gq", "gk", "gv", "gr", "m_att", "m_gla")
    cols, parts, off = {}, [], 0
    for name in order:
        s, width = src[name]
        cols[name] = off
        parts.append(w_in[:, s:s + width])
        off += width
    w_main = jnp.concatenate(parts, axis=1).astype(BF16)

    def seg(name):
        s, width = src[name]
        return w_in[:, s:s + width]

    pad = jnp.zeros((d, LANES - IDX_HEADS - GLA_GATE_RANK), w_in.dtype)
    w_small = jnp.concatenate([seg("ik"), seg("iw"), seg("glr"), pad], axis=1).astype(BF16)

    att_scale = HEAD_DIM ** -0.5
    idx_scale = (HEAD_DIM ** -0.5) * (IDX_HEADS ** -0.5)
    cg = jnp.ones((off,), F32)
    cg = cg.at[cols["aq"]:cols["aq"] + sizes["aq"]].set(jnp.tile(q_norm * att_scale, ATT_HEADS))
    cg = cg.at[cols["ak"]:cols["ak"] + sizes["ak"]].set(jnp.tile(k_norm, ATT_KV_HEADS))
    cg2 = jnp.concatenate([idx_k_norm, jnp.full((IDX_HEADS,), idx_scale, F32),
                           jnp.ones((LANES - IDX_HEADS,), F32)])
    norm_tiles = tuple(range(cols["aq"] // PROJ_TN, (cols["aq"] + sizes["aq"]) // PROJ_TN)) + \
        tuple(range(cols["ak"] // PROJ_TN, (cols["ak"] + sizes["ak"]) // PROJ_TN))
    sig_start = cols["m_att"] // PROJ_TN
    return w_main, w_small, cg.reshape(1, -1), cg2.reshape(1, -1), cols, norm_tiles, sig_start


def kernel(x, c, ada_w, ada_b, norm_mix, norm_ffn, w_in, q_norm, k_norm, idx_k_norm, w_gla_gate,
           b_gla_gate, gla_out_norm, w_out_attn, w_out_gla, w_out, ffn_w1, ffn_w3, ffn_w2,
           moe_router, moe_w1, moe_w3, moe_w2):
    batch, seq, d = x.shape
    depth = ada_w.shape[0]
    n = batch * seq
    x2 = x.reshape(n, d)

    c8 = jnp.zeros((8, d), F32).at[:batch].set(c)
    mod_all = _ada(c8, ada_w, ada_b)

    for layer in range(depth):
        mod = mod_all[layer, :batch].reshape(batch, 6, 1, d)
        shift_m, scale_m, gate_m = mod[:, 0], mod[:, 1], mod[:, 2]
        shift_f, scale_f, gate_f = mod[:, 3], mod[:, 4], mod[:, 5]

        w_main, w_small, cg, cg2, cols, norm_tiles, sig_start = _prep_in_proj(
            w_in[layer], q_norm[layer], k_norm[layer], idx_k_norm[layer], d)
        proj, ik, sm = _modproj(x2, norm_mix[layer].reshape(1, d), scale_m, shift_m,
                                w_main, cg, w_small, cg2, seq, norm_tiles, sig_start)
        y_att = _attention(proj, ik, sm, batch, seq, cols)
        wg = jnp.zeros((LANES, w_gla_gate.shape[2]), F32).at[
            GLA_GATE_RANK:2 * GLA_GATE_RANK].set(w_gla_gate[layer])
        y_gla = _gla(proj, sm, wg, b_gla_gate[layer].reshape(1, -1),
                     gla_out_norm[layer].reshape(1, -1), batch, seq, cols)
        merged = _merge(y_att, y_gla, w_out_attn[layer].astype(BF16),
                        w_out_gla[layer].astype(BF16), proj, cols)
        x2 = _resproj(merged, w_out[layer].astype(BF16), x2, gate_m, seq)

        g_f = norm_ffn[layer].reshape(1, d)
        i = layer // 2
        if layer % 2 == 0:
            x2 = _ffn(x2, g_f, scale_f, shift_f, ffn_w1[i].astype(BF16), ffn_w3[i].astype(BF16),
                      ffn_w2[i].astype(BF16), gate_f, seq)
        else:
            w_r = jnp.zeros((d, LANES), F32).at[:, :N_EXPERTS].set(moe_router[i])
            x2 = _moe(x2, g_f, scale_f, shift_f, w_r, moe_w1[i].astype(BF16),
                      moe_w3[i].astype(BF16), moe_w2[i].astype(BF16), gate_f, seq)
    return x2.reshape(batch, seq, d)
```

```python
import functools

import jax
import jax.numpy as jnp
from jax import lax
from jax.experimental import pallas as pl
from jax.experimental.pallas import tpu as pltpu

F32 = jnp.float32
BF16 = jnp.bfloat16

ATT_HEADS = 16
ATT_KV_HEADS = 4
HEAD_DIM = 128
IDX_HEADS = 16
TOPK_MAX = 256
GLA_HEADS = 4
GLA_GATE_RANK = 16
GLA_GATE_TAU = 16.0
GLA_CHUNK = 64
N_EXPERTS = 8
EPS = 1e-6
LANES = 128
NEG_BIG = -1e30
INT_MIN = -(2 ** 31)
LOG2_E = 1.4426950408889634

PROJ_TN = 512
VMEM_LIMIT = 56 * 1024 * 1024


def _nt_dot(a, b):
    return lax.dot_general(a, b, (((1,), (1,)), ((), ())), preferred_element_type=F32)


def _rms(a):
    return a * lax.rsqrt(jnp.mean(a * a, axis=-1, keepdims=True) + EPS)


def _sigmoid(a):
    return 1.0 / (1.0 + jnp.exp(-a))


def _modulate(x, g, scale, shift):
    return _rms(x) * g * (1.0 + scale) + shift


def _ada_kernel(c_ref, w_ref, b_ref, o_ref):
    c = c_ref[...]
    cond = c * _sigmoid(c)
    o_ref[...] = jnp.dot(cond, w_ref[...], preferred_element_type=F32,
                         precision=lax.Precision.HIGHEST) + b_ref[...]


def _ada(c8, ada_w, ada_b):
    depth, d, n = ada_w.shape
    tn = 1024
    return pl.pallas_call(
        _ada_kernel,
        grid=(depth, n // tn),
        in_specs=[
            pl.BlockSpec((8, d), lambda l, j: (0, 0)),
            pl.BlockSpec((None, d, tn), lambda l, j: (l, 0, j)),
            pl.BlockSpec((None, 1, tn), lambda l, j: (l, 0, j)),
        ],
        out_specs=pl.BlockSpec((None, 8, tn), lambda l, j: (l, 0, j)),
        out_shape=jax.ShapeDtypeStruct((depth, 8, n), F32),
        name="ada_mod",
    )(c8, ada_w, ada_b.reshape(depth, 1, n))


def _modproj_kernel(x_ref, g_ref, sc_ref, sh_ref, w_ref, cg_ref, w2_ref, cg2_ref,
                    o_ref, ik_ref, sm_ref, hn_ref, *, norm_tiles, sig_start):
    j = pl.program_id(1)

    @pl.when(j == 0)
    def _():
        h = _modulate(x_ref[...], g_ref[...], sc_ref[...], sh_ref[...]).astype(BF16)
        hn_ref[...] = h
        small = jnp.dot(h, w2_ref[...], preferred_element_type=F32)
        ik_ref[...] = (_rms(small[:, :LANES]) * cg2_ref[:, :LANES]).astype(BF16)
        sm_ref[...] = small[:, LANES:] * cg2_ref[:, LANES:]

    acc = jnp.dot(hn_ref[...], w_ref[...], preferred_element_type=F32)
    is_norm = functools.reduce(jnp.logical_or, [j == t for t in norm_tiles])
    is_sig = j >= sig_start

    @pl.when(is_norm)
    def _():
        for c in range(PROJ_TN // LANES):
            sl = slice(c * LANES, (c + 1) * LANES)
            o_ref[:, sl] = (_rms(acc[:, sl]) * cg_ref[:, sl]).astype(BF16)

    @pl.when(is_sig)
    def _():
        o_ref[...] = _sigmoid(acc).astype(BF16)

    @pl.when(jnp.logical_not(jnp.logical_or(is_norm, is_sig)))
    def _():
        o_ref[...] = acc.astype(BF16)


def _modproj(x2, g, scale, shift, w, cg, w2, cg2, seq, norm_tiles, sig_start):
    n, d = x2.shape
    ncols = w.shape[1]
    tm = min(1024, seq)
    per_b = seq // tm
    kern = functools.partial(_modproj_kernel, norm_tiles=norm_tiles, sig_start=sig_start)
    return pl.pallas_call(
        kern,
        grid=(n // tm, ncols // PROJ_TN),
        in_specs=[
            pl.BlockSpec((tm, d), lambda i, j: (i, 0)),
            pl.BlockSpec((1, d), lambda i, j: (0, 0)),
            pl.BlockSpec((None, 1, d), lambda i, j: (i // per_b, 0, 0)),
            pl.BlockSpec((None, 1, d), lambda i, j: (i // per_b, 0, 0)),
            pl.BlockSpec((d, PROJ_TN), lambda i, j: (0, j)),
            pl.BlockSpec((1, PROJ_TN), lambda i, j: (0, j)),
            pl.BlockSpec((d, 2 * LANES), lambda i, j: (0, 0)),
            pl.BlockSpec((1, 2 * LANES), lambda i, j: (0, 0)),
        ],
        out_specs=[
            pl.BlockSpec((tm, PROJ_TN), lambda i, j: (i, j)),
            pl.BlockSpec((tm, LANES), lambda i, j: (i, 0)),
            pl.BlockSpec((tm, LANES), lambda i, j: (i, 0)),
        ],
        out_shape=[
            jax.ShapeDtypeStruct((n, ncols), BF16),
            jax.ShapeDtypeStruct((n, LANES), BF16),
            jax.ShapeDtypeStruct((n, LANES), F32),
        ],
        scratch_shapes=[pltpu.VMEM((tm, d), BF16)],
        compiler_params=pltpu.CompilerParams(
            dimension_semantics=("arbitrary", "arbitrary"), vmem_limit_bytes=VMEM_LIMIT),
        name="modproj",
    )(x2, g, scale, shift, w, cg, w2, cg2)


ATT_TQ = 128
ATT_TK = 256


def _sortable(a):
    a = jnp.where(a == 0.0, 0.0, a)
    bits = pltpu.bitcast(a, jnp.int32)
    return bits ^ ((bits >> 31) & 0x7FFFFFFF)


def _attn_kernel(aq_ref, iq_ref, ak_ref, av_ref, ik_ref, iw_ref, o_ref,
                 sc_ref, qs_ref, m_ref, acc_ref, *, topk, seq):
    tq, tk = ATT_TQ, ATT_TK
    group = ATT_HEADS // ATT_KV_HEADS
    qi = pl.program_id(1)
    n_kt = (qi * tq + tq + tk - 1) // tk
    row = qi * tq + lax.broadcasted_iota(jnp.int32, (tq, 1), 0)
    lane_col = lax.broadcasted_iota(jnp.int32, (1, tk), 1)
    w = iw_ref[:, 0:IDX_HEADS]

    def score_tile(kt, carry):
        kblk = ik_ref[pl.ds(pl.multiple_of(kt * tk, tk), tk), :]
        acc = jnp.zeros((tq, tk), F32)
        for h in range(IDX_HEADS):
            z = _nt_dot(iq_ref[:, h * HEAD_DIM:(h + 1) * HEAD_DIM], kblk)
            acc = acc + w[:, h:h + 1] * jnp.maximum(z, 0.0)
        causal = (kt * tk + lane_col) <= row
        sc_ref[kt] = jnp.where(causal, _sortable(acc), INT_MIN)
        return carry

    lax.fori_loop(0, n_kt, score_tile, 0)

    def count_where(pred):
        def body(kt, c):
            hit = jnp.where(pred(sc_ref[kt], kt), 1.0, 0.0)
            for cc in range(tk // LANES):
                c = c + hit[:, cc * LANES:(cc + 1) * LANES]
            return c
        c = lax.fori_loop(0, n_kt, body, jnp.zeros((tq, LANES), F32))
        return jnp.sum(c, axis=-1, keepdims=True)

    def count_ge(t):
        return count_where(lambda key, kt: key >= t)

    kf = float(topk)
    t0 = jnp.where(count_ge(jnp.zeros((tq, 1), jnp.int32)) >= kf, 0, INT_MIN).astype(jnp.int32)

    def bit_step(i, t):
        cand = t | jnp.left_shift(jnp.int32(1), 30 - i)
        return jnp.where(count_ge(cand) >= kf, cand, t)

    thr = lax.fori_loop(0, 31, bit_step, t0)

    excess = jnp.logical_and(count_ge(thr) > kf, thr > INT_MIN)

    @pl.when(jnp.max(jnp.where(excess, 1.0, 0.0)) > 0.0)
    def _():
        need = kf - count_where(lambda key, kt: key > thr)

        def ties_below(j):
            return count_where(
                lambda key, kt: jnp.logical_and(key == thr, (kt * tk + lane_col) < j))

        nbits = max(seq - 1, 1).bit_length()

        def idx_step(i, j0):
            cand = j0 | jnp.left_shift(jnp.int32(1), nbits - 1 - i)
            return jnp.where(ties_below(cand) < need, cand, j0)

        j0 = lax.fori_loop(0, nbits, idx_step, jnp.zeros((tq, 1), jnp.int32))

        def demote(kt, carry):
            key = sc_ref[kt]
            late_tie = jnp.logical_and(key == thr, (kt * tk + lane_col) > j0)
            sc_ref[kt] = jnp.where(jnp.logical_and(excess, late_tie), INT_MIN, key)
            return carry

        lax.fori_loop(0, n_kt, demote, 0)

    thr = jnp.maximum(thr, INT_MIN + 1)

    for h in range(ATT_HEADS):
        qs_ref[h * tq:(h + 1) * tq, :] = aq_ref[:, h * HEAD_DIM:(h + 1) * HEAD_DIM]

    m_ref[...] = jnp.full(m_ref.shape, NEG_BIG, F32)
    acc_ref[...] = jnp.zeros(acc_ref.shape, F32)
    ones = jnp.ones((tk, HEAD_DIM), BF16)

    def kv_tile(kt, carry):
        r0 = pl.multiple_of(kt * tk, tk)
        keep = (sc_ref[kt] >= thr)[None]
        for g in range(ATT_KV_HEADS):
            q4 = qs_ref[g * group * tq:(g + 1) * group * tq, :]
            kblk = ak_ref[pl.ds(r0, tk), g * HEAD_DIM:(g + 1) * HEAD_DIM]
            vblk = av_ref[pl.ds(r0, tk), g * HEAD_DIM:(g + 1) * HEAD_DIM]
            v1 = jnp.concatenate([vblk, ones], axis=1)
            s = _nt_dot(q4, kblk).reshape(group, tq, tk)
            s = jnp.where(keep, s, NEG_BIG).reshape(group * tq, tk)
            m_old = m_ref[g]
            m_new = jnp.maximum(m_old, jnp.max(s, axis=-1, keepdims=True))
            alpha = jnp.exp2(m_old - m_new)
            p = jnp.exp2(s - jnp.concatenate([m_new] * (tk // LANES), axis=1))
            pv = jnp.dot(p.astype(BF16), v1, preferred_element_type=F32)
            acc_ref[g] = jnp.concatenate([alpha, alpha], axis=1) * acc_ref[g] + pv
            m_ref[g] = m_new
        return carry

    lax.fori_loop(0, n_kt, kv_tile, 0)

    for g in range(ATT_KV_HEADS):
        og = acc_ref[g]
        og = og[:, :HEAD_DIM] / og[:, HEAD_DIM:]
        for r in range(group):
            h = g * group + r
            o_ref[:, h * HEAD_DIM:(h + 1) * HEAD_DIM] = og[r * tq:(r + 1) * tq, :].astype(BF16)


def _attention(proj, ik, sm, batch, seq, cols):
    n = proj.shape[0]
    tq, tk = ATT_TQ, ATT_TK
    nq = seq // tq
    width = ATT_HEADS * HEAD_DIM
    kvw = ATT_KV_HEADS * HEAD_DIM
    topk = min(TOPK_MAX, seq // 4)
    group = ATT_HEADS // ATT_KV_HEADS
    kern = functools.partial(_attn_kernel, topk=topk, seq=seq)
    once = pl.Buffered(1)
    return pl.pallas_call(
        kern,
        grid=(batch, nq),
        in_specs=[
            pl.BlockSpec((tq, width), lambda b, q: (b * nq + q, cols["aq"] // width)),
            pl.BlockSpec((tq, width), lambda b, q: (b * nq + q, cols["iq"] // width)),
            pl.BlockSpec((seq, kvw), lambda b, q: (b, cols["ak"] // kvw), pipeline_mode=once),
            pl.BlockSpec((seq, kvw), lambda b, q: (b, cols["av"] // kvw), pipeline_mode=once),
            pl.BlockSpec((seq, LANES), lambda b, q: (b, 0), pipeline_mode=once),
            pl.BlockSpec((tq, LANES), lambda b, q: (b * nq + q, 0)),
        ],
        out_specs=pl.BlockSpec((tq, width), lambda b, q: (b * nq + q, 0)),
        out_shape=jax.ShapeDtypeStruct((n, width), BF16),
        scratch_shapes=[
            pltpu.VMEM((seq // tk, tq, tk), jnp.int32),
            pltpu.VMEM((ATT_HEADS * tq, HEAD_DIM), BF16),
            pltpu.VMEM((ATT_KV_HEADS, group * tq, LANES), F32),
            pltpu.VMEM((ATT_KV_HEADS, group * tq, 2 * HEAD_DIM), F32),
        ],
        compiler_params=pltpu.CompilerParams(
            dimension_semantics=("arbitrary", "arbitrary"), vmem_limit_bytes=VMEM_LIMIT),
        name="dsa_attention",
    )(proj, proj, proj, proj, ik, sm)


GLA_T = 512


def _gla_kernel(q_ref, k_ref, v_ref, r_ref, lr_ref, wg_ref, bg_ref, gn_ref, o_ref, st_ref, *, dk):
    c_len = GLA_CHUNK
    hi = lax.Precision.HIGHEST

    @pl.when(pl.program_id(2) == 0)
    def _():
        st_ref[...] = jnp.zeros(st_ref.shape, F32)

    ri = lax.broadcasted_iota(jnp.int32, (c_len, c_len), 0)
    ci = lax.broadcasted_iota(jnp.int32, (c_len, c_len), 1)
    lower = ri >= ci
    tril = jnp.where(lower, 1.0, 0.0).astype(F32)
    dv = v_ref.shape[1]
    qscale = dk ** -0.5

    for c in range(GLA_T // c_len):
        rows = slice(c * c_len, (c + 1) * c_len)
        gl = jnp.dot(lr_ref[rows, :], wg_ref[...], preferred_element_type=F32, precision=hi) + bg_ref[...]
        log_a = (jnp.minimum(gl, 0.0) - jnp.log(1.0 + jnp.exp(-jnp.abs(gl)))) / GLA_GATE_TAU
        b = jnp.dot(tril, log_a, preferred_element_type=F32, precision=hi)
        b_last = b[c_len - 1:c_len, :]
        q = q_ref[rows, :].astype(F32) * qscale
        k = k_ref[rows, :].astype(F32)
        v = v_ref[rows, :]
        q_dec = (q * jnp.exp(b)).astype(BF16)
        k_inv = (k * jnp.exp(-b)).astype(BF16)
        k_tail = k * jnp.exp(b_last - b)
        a = jnp.where(lower, _nt_dot(q_dec, k_inv), 0.0)
        state = st_ref[...]
        o = (jnp.dot(a.astype(BF16), v, preferred_element_type=F32)
             + jnp.dot(q_dec, state.astype(BF16), preferred_element_type=F32))
        dec = jnp.exp(b.T[:, c_len - 1:c_len])
        upd = jnp.dot(k_tail.T.astype(BF16), v, preferred_element_type=F32)
        for cc in range(dv // LANES):
            sl = slice(cc * LANES, (cc + 1) * LANES)
            st_ref[:, sl] = dec * state[:, sl] + upd[:, sl]
        r = r_ref[rows, :].astype(F32)
        o_ref[rows, :] = (_rms(o) * gn_ref[...] * (r * _sigmoid(r))).astype(BF16)


def _gla(proj, sm, wg, bg, gn, batch, seq, cols):
    n = proj.shape[0]
    dk = wg.shape[1] // GLA_HEADS
    dv = gn.shape[1]
    t = min(GLA_T, seq)
    assert t == GLA_T
    ns = seq // t
    kern = functools.partial(_gla_kernel, dk=dk)
    return pl.pallas_call(
        kern,
        grid=(batch, GLA_HEADS, ns),
        in_specs=[
            pl.BlockSpec((t, dk), lambda b, h, s: (b * ns + s, cols["gq"] // dk + h)),
            pl.BlockSpec((t, dk), lambda b, h, s: (b * ns + s, cols["gk"] // dk + h)),
            pl.BlockSpec((t, dv), lambda b, h, s: (b * ns + s, cols["gv"] // dv + h)),
            pl.BlockSpec((t, dv), lambda b, h, s: (b * ns + s, cols["gr"] // dv + h)),
            pl.BlockSpec((t, LANES), lambda b, h, s: (b * ns + s, 0)),
            pl.BlockSpec((LANES, dk), lambda b, h, s: (0, h)),
            pl.BlockSpec((1, dk), lambda b, h, s: (0, h)),
            pl.BlockSpec((1, dv), lambda b, h, s: (0, 0)),
        ],
        out_specs=pl.BlockSpec((t, dv), lambda b, h, s: (b * ns + s, h)),
        out_shape=jax.ShapeDtypeStruct((n, GLA_HEADS * dv), BF16),
        scratch_shapes=[pltpu.VMEM((dk, dv), F32)],
        compiler_params=pltpu.CompilerParams(
            dimension_semantics=("arbitrary", "arbitrary", "arbitrary"),
            vmem_limit_bytes=VMEM_LIMIT),
        name="gla",
    )(proj, proj, proj, proj, sm, wg, bg, gn)


def _merge_kernel(ya_ref, yg_ref, wa_ref, wg_ref, ma_ref, mg_ref, o_ref):
    a = jnp.dot(ya_ref[...], wa_ref[...], preferred_element_type=F32)
    g = jnp.dot(yg_ref[...], wg_ref[...], preferred_element_type=F32)
    o_ref[...] = (ma_ref[...].astype(F32) * a + mg_ref[...].astype(F32) * g).astype(BF16)


def _merge(y_att, y_gla, w_att, w_gla, proj, cols):
    n, d_att = y_att.shape
    d_gla = y_gla.shape[1]
    d = w_att.shape[1]
    tm, tn = 1024, PROJ_TN
    return pl.pallas_call(
        _merge_kernel,
        grid=(n // tm, d // tn),
        in_specs=[
            pl.BlockSpec((tm, d_att), lambda i, j: (i, 0)),
            pl.BlockSpec((tm, d_gla), lambda i, j: (i, 0)),
            pl.BlockSpec((d_att, tn), lambda i, j: (0, j)),
            pl.BlockSpec((d_gla, tn), lambda i, j: (0, j)),
            pl.BlockSpec((tm, tn), lambda i, j: (i, cols["m_att"] // tn + j)),
            pl.BlockSpec((tm, tn), lambda i, j: (i, cols["m_gla"] // tn + j)),
        ],
        out_specs=pl.BlockSpec((tm, tn), lambda i, j: (i, j)),
        out_shape=jax.ShapeDtypeStruct((n, d), BF16),
        compiler_params=pltpu.CompilerParams(
            dimension_semantics=("arbitrary", "arbitrary"), vmem_limit_bytes=VMEM_LIMIT),
        name="merge",
    )(y_att, y_gla, w_att, w_gla, proj, proj)


def _resproj_kernel(m_ref, w_ref, x_ref, gate_ref, o_ref):
    y = jnp.dot(m_ref[...], w_ref[...], preferred_element_type=F32)
    o_ref[...] = x_ref[...] + gate_ref[...] * y


def _resproj(merged, w, x2, gate, seq):
    n, d = x2.shape
    tm, tn = 1024, PROJ_TN
    per_b = seq // tm
    return pl.pallas_call(
        _resproj_kernel,
        grid=(n // tm, d // tn),
        in_specs=[
            pl.BlockSpec((tm, merged.shape[1]), lambda i, j: (i, 0)),
            pl.BlockSpec((merged.shape[1], tn), lambda i, j: (0, j)),
            pl.BlockSpec((tm, tn), lambda i, j: (i, j)),
            pl.BlockSpec((None, 1, tn), lambda i, j: (i // per_b, 0, j)),
        ],
        out_specs=pl.BlockSpec((tm, tn), lambda i, j: (i, j)),
        out_shape=jax.ShapeDtypeStruct((n, d), F32),
        compiler_params=pltpu.CompilerParams(
            dimension_semantics=("arbitrary", "arbitrary"), vmem_limit_bytes=VMEM_LIMIT),
        name="resproj",
    )(merged, w, x2, gate)


FFN_TM = 512
FFN_TF = 512


def _swiglu_step(hn, w1, w3, w2):
    a = jnp.dot(hn, w1, preferred_element_type=F32)
    b = jnp.dot(hn, w3, preferred_element_type=F32)
    return a * _sigmoid(a) * b, w2


def _ffn_kernel(x_ref, g_ref, sc_ref, sh_ref, w1_ref, w3_ref, w2_ref, gate_ref, o_ref,
                hn_ref, acc_ref):
    j = pl.program_id(1)

    @pl.when(j == 0)
    def _():
        hn_ref[...] = _modulate(x_ref[...], g_ref[...], sc_ref[...], sh_ref[...]).astype(BF16)
        acc_ref[...] = jnp.zeros(acc_ref.shape, F32)

    act, w2 = _swiglu_step(hn_ref[...], w1_ref[...], w3_ref[...], w2_ref[...])
    acc_ref[...] += jnp.dot(act.astype(BF16), w2, preferred_element_type=F32)

    @pl.when(j == pl.num_programs(1) - 1)
    def _():
        o_ref[...] = x_ref[...] + gate_ref[...] * acc_ref[...]


def _ffn(x2, g, scale, shift, w1, w3, w2, gate, seq):
    n, d = x2.shape
    dff = w1.shape[1]
    tm, tf = min(FFN_TM, seq), FFN_TF
    per_b = seq // tm
    vec = pl.BlockSpec((None, 1, d), lambda i, j: (i // per_b, 0, 0))
    return pl.pallas_call(
        _ffn_kernel,
        grid=(n // tm, dff // tf),
        in_specs=[
            pl.BlockSpec((tm, d), lambda i, j: (i, 0)),
            pl.BlockSpec((1, d), lambda i, j: (0, 0)),
            vec, vec,
            pl.BlockSpec((d, tf), lambda i, j: (0, j)),
            pl.BlockSpec((d, tf), lambda i, j: (0, j)),
            pl.BlockSpec((tf, d), lambda i, j: (j, 0)),
            vec,
        ],
        out_specs=pl.BlockSpec((tm, d), lambda i, j: (i, 0)),
        out_shape=jax.ShapeDtypeStruct((n, d), F32),
        scratch_shapes=[pltpu.VMEM((tm, d), BF16), pltpu.VMEM((tm, d), F32)],
        compiler_params=pltpu.CompilerParams(
            dimension_semantics=("arbitrary", "arbitrary"), vmem_limit_bytes=VMEM_LIMIT),
        name="ffn_swiglu",
    )(x2, g, scale, shift, w1, w3, w2, gate)


MOE_TM = 512
ROUTE_TM = 1024


def _router_kernel(x_ref, g_ref, sc_ref, sh_ref, wr_ref, hn_ref, rt_ref):
    lane = lax.broadcasted_iota(jnp.int32, (1, LANES), 1)
    lane_f = lane.astype(F32)
    h = _modulate(x_ref[...], g_ref[...], sc_ref[...], sh_ref[...])
    hn_ref[...] = h
    logits = jnp.dot(h, wr_ref[...], preferred_element_type=F32, precision=lax.Precision.HIGHEST)
    logits = jnp.where(lane < N_EXPERTS, logits, -jnp.inf)
    m1 = jnp.max(logits, axis=-1, keepdims=True)
    i1 = jnp.min(jnp.where(logits == m1, lane_f, float(LANES)), axis=-1, keepdims=True)
    rest = jnp.where(lane_f == i1, -jnp.inf, logits)
    m2 = jnp.max(rest, axis=-1, keepdims=True)
    i2 = jnp.min(jnp.where(rest == m2, lane_f, float(LANES)), axis=-1, keepdims=True)
    e2 = jnp.exp(m2 - m1)
    p1 = 1.0 / (1.0 + e2)
    p2 = e2 / (1.0 + e2)
    rt_ref[...] = (jnp.where(lane == 0, i1, 0.0) + jnp.where(lane == 1, i2, 0.0)
                   + jnp.where(lane == 2, p1, 0.0) + jnp.where(lane == 3, p2, 0.0))


def _router(x2, g, scale, shift, w_router, seq):
    n, d = x2.shape
    tm = min(ROUTE_TM, seq)
    per_b = seq // tm
    vec = pl.BlockSpec((None, 1, d), lambda i: (i // per_b, 0, 0))
    return pl.pallas_call(
        _router_kernel,
        grid=(n // tm,),
        in_specs=[
            pl.BlockSpec((tm, d), lambda i: (i, 0)),
            pl.BlockSpec((1, d), lambda i: (0, 0)),
            vec, vec,
            pl.BlockSpec((d, LANES), lambda i: (0, 0)),
        ],
        out_specs=[pl.BlockSpec((tm, d), lambda i: (i, 0)),
                   pl.BlockSpec((tm, LANES), lambda i: (i, 0))],
        out_shape=[jax.ShapeDtypeStruct((n, d), F32), jax.ShapeDtypeStruct((n, LANES), F32)],
        compiler_params=pltpu.CompilerParams(
            dimension_semantics=("arbitrary",), vmem_limit_bytes=VMEM_LIMIT),
        name="moe_router",
    )(x2, g, scale, shift, w_router)


def _route_tables(rt, n, n_exp, tm):
    rows = 2 * n + n_exp * tm
    n_tiles = rows // tm
    ef = rt[:, :2].astype(jnp.int32).T.reshape(-1)
    onehot = (ef[:, None] == jnp.arange(n_exp, dtype=jnp.int32)[None, :]).astype(jnp.int32)
    csum = jnp.cumsum(onehot, axis=0)
    rank = jnp.take_along_axis(csum, ef[:, None], axis=1)[:, 0] - 1
    cnt = csum[-1]
    padded = ((cnt + tm - 1) // tm) * tm
    ends = jnp.cumsum(padded)
    dest = (ends - padded)[ef] + rank
    total = ends[-1]
    tile_start = jnp.arange(n_tiles, dtype=jnp.int32) * tm
    tile_valid = (tile_start < total).astype(jnp.int32)
    tile_expert = jnp.sum((tile_start[:, None] >= ends[None, :]).astype(jnp.int32), axis=1)
    last_expert = tile_expert[jnp.maximum(total // tm - 1, 0)]
    tile_expert = jnp.where(tile_valid == 1, tile_expert, last_expert)
    pair = jnp.arange(2 * n, dtype=jnp.int32)
    src_token = jnp.zeros((rows,), jnp.int32).at[dest].set(pair % n, unique_indices=True)
    dst_real = jnp.full((rows,), -1, jnp.int32).at[dest].set(pair, unique_indices=True)
    is_pad = dst_real < 0
    pad_slot = 2 * n + jnp.cumsum(is_pad.astype(jnp.int32)) - 1
    dst_row = jnp.where(is_pad, pad_slot, dst_real)
    return tile_expert, tile_valid, src_token, dst_row, rows


def _moe_grouped_kernel(te_ref, tv_ref, src_ref, dst_ref, hn_hbm, w1_ref, w3_ref, w2_ref, y_hbm,
                        xg_ref, xb_ref, acc_ref, sem_in, sem_out):
    t = pl.program_id(0)
    j = pl.program_id(1)
    tm = xg_ref.shape[0]
    valid = tv_ref[t] == 1

    def gather_row(r):
        return pltpu.make_async_copy(hn_hbm.at[pl.ds(src_ref[t * tm + r], 1)],
                                     xg_ref.at[pl.ds(r, 1)], sem_in)

    def scatter_row(r):
        return pltpu.make_async_copy(acc_ref.at[pl.ds(r, 1)],
                                     y_hbm.at[pl.ds(dst_ref[t * tm + r], 1)], sem_out)

    def for_rows(fn):
        def body(r, carry):
            fn(r)
            return carry
        lax.fori_loop(0, tm, body, 0)

    @pl.when(jnp.logical_and(valid, j == 0))
    def _():
        for_rows(lambda r: gather_row(r).start())
        for_rows(lambda r: gather_row(r).wait())
        xb_ref[...] = xg_ref[...].astype(BF16)
        acc_ref[...] = jnp.zeros(acc_ref.shape, F32)

    @pl.when(valid)
    def _():
        act, w2 = _swiglu_step(xb_ref[...], w1_ref[...], w3_ref[...], w2_ref[...])
        acc_ref[...] += jnp.dot(act.astype(BF16), w2, preferred_element_type=F32)

    @pl.when(jnp.logical_and(valid, j == pl.num_programs(1) - 1))
    def _():
        for_rows(lambda r: scatter_row(r).start())
        for_rows(lambda r: scatter_row(r).wait())

    @pl.when(jnp.logical_and(jnp.logical_not(valid), j == 0))
    def _():
        acc_ref[...] = jnp.zeros(acc_ref.shape, F32)
        fill = pltpu.make_async_copy(
            acc_ref, y_hbm.at[pl.ds(pl.multiple_of(t * tm, tm), tm)], sem_out)
        fill.start()
        fill.wait()


def _moe_grouped(hn, tables, w1, w3, w2):
    tile_expert, tile_valid, src_token, dst_row, rows = tables
    n, d = hn.shape
    dff = w1.shape[2]
    tm, tf = MOE_TM, FFN_TF
    nj = dff // tf

    def jj(t, j, tv):
        return j * tv[t] + (nj - 1) * (1 - tv[t])

    grid_spec = pltpu.PrefetchScalarGridSpec(
        num_scalar_prefetch=4,
        grid=(rows // tm, nj),
        in_specs=[
            pl.BlockSpec(memory_space=pl.ANY),
            pl.BlockSpec((None, d, tf), lambda t, j, te, tv, s, dd: (te[t], 0, jj(t, j, tv))),
            pl.BlockSpec((None, d, tf), lambda t, j, te, tv, s, dd: (te[t], 0, jj(t, j, tv))),
            pl.BlockSpec((None, tf, d), lambda t, j, te, tv, s, dd: (te[t], jj(t, j, tv), 0)),
        ],
        out_specs=pl.BlockSpec(memory_space=pl.ANY),
        scratch_shapes=[pltpu.VMEM((tm, d), F32), pltpu.VMEM((tm, d), BF16),
                        pltpu.VMEM((tm, d), F32),
                        pltpu.SemaphoreType.DMA(()), pltpu.SemaphoreType.DMA(())],
    )
    return pl.pallas_call(
        _moe_grouped_kernel,
        grid_spec=grid_spec,
        out_shape=jax.ShapeDtypeStruct((rows, d), F32),
        compiler_params=pltpu.CompilerParams(
            dimension_semantics=("arbitrary", "arbitrary"), vmem_limit_bytes=VMEM_LIMIT),
        name="moe_grouped",
    )(tile_expert, tile_valid, src_token, dst_row, hn, w1, w3, w2)


def _moe_combine_kernel(x_ref, y0_ref, y1_ref, rt_ref, gate_ref, o_ref):
    p1 = rt_ref[:, 2:3]
    p2 = rt_ref[:, 3:4]
    o_ref[...] = x_ref[...] + gate_ref[...] * (p1 * y0_ref[...] + p2 * y1_ref[...])


def _moe_combine(x2, y, rt, gate, seq):
    n, d = x2.shape
    tm = min(FFN_TM, seq)
    per_b = seq // tm
    nb = n // tm
    return pl.pallas_call(
        _moe_combine_kernel,
        grid=(nb,),
        in_specs=[
            pl.BlockSpec((tm, d), lambda i: (i, 0)),
            pl.BlockSpec((tm, d), lambda i: (i, 0)),
            pl.BlockSpec((tm, d), lambda i: (nb + i, 0)),
            pl.BlockSpec((tm, LANES), lambda i: (i, 0)),
            pl.BlockSpec((None, 1, d), lambda i: (i // per_b, 0, 0)),
        ],
        out_specs=pl.BlockSpec((tm, d), lambda i: (i, 0)),
        out_shape=jax.ShapeDtypeStruct((n, d), F32),
        compiler_params=pltpu.CompilerParams(
            dimension_semantics=("arbitrary",), vmem_limit_bytes=VMEM_LIMIT),
        name="moe_combine",
    )(x2, y, y, rt, gate)


def _moe(x2, g, scale, shift, w_router, w1, w3, w2, gate, seq):
    n = x2.shape[0]
    hn, rt = _router(x2, g, scale, shift, w_router, seq)
    tables = _route_tables(rt, n, w1.shape[0], MOE_TM)
    y = _moe_grouped(hn, tables, w1, w3, w2)
    return _moe_combine(x2, y, rt, gate, seq)


def _prep_in_proj(w_in, q_norm, k_norm, idx_k_norm, d):
    sizes = dict(aq=ATT_HEADS * HEAD_DIM, ak=ATT_KV_HEADS * HEAD_DIM, av=ATT_KV_HEADS * HEAD_DIM,
                 iq=IDX_HEADS * HEAD_DIM, ik=HEAD_DIM, iw=IDX_HEADS,
                 gq=d // 2, gk=d // 2, gv=d, glr=GLA_GATE_RANK, gr=d, m_att=d, m_gla=d)
    src, acc = {}, 0
    for name in ("aq", "ak", "av", "iq", "ik", "iw", "gq", "gk", "gv", "glr", "gr", "m_att", "m_gla"):
        src[name] = (acc, sizes[name])
        acc += sizes[name]
    order = ("aq", "iq", "ak", "av", "gq", "gk", "gv", "gr", "m_att", "m_gla")
    cols, parts, off = {}, [], 0
    for name in order:
        s, width = src[name]
        cols[name] = off
        parts.append(w_in[:, s:s + width])
        off += width
    w_main = jnp.concatenate(parts, axis=1).astype(BF16)

    def seg(name):
        s, width = src[name]
        return w_in[:, s:s + width]

    pad = jnp.zeros((d, LANES - IDX_HEADS - GLA_GATE_RANK), w_in.dtype)
    w_small = jnp.concatenate([seg("ik"), seg("iw"), seg("glr"), pad], axis=1).astype(BF16)

    att_scale = HEAD_DIM ** -0.5 * LOG2_E
    idx_scale = (HEAD_DIM ** -0.5) * (IDX_HEADS ** -0.5)
    cg = jnp.ones((off,), F32)
    cg = cg.at[cols["aq"]:cols["aq"] + sizes["aq"]].set(jnp.tile(q_norm * att_scale, ATT_HEADS))
    cg = cg.at[cols["ak"]:cols["ak"] + sizes["ak"]].set(jnp.tile(k_norm, ATT_KV_HEADS))
    cg2 = jnp.concatenate([idx_k_norm, jnp.full((IDX_HEADS,), idx_scale, F32),
                           jnp.ones((LANES - IDX_HEADS,), F32)])
    norm_tiles = tuple(range(cols["aq"] // PROJ_TN, (cols["aq"] + sizes["aq"]) // PROJ_TN)) + \
        tuple(range(cols["ak"] // PROJ_TN, (cols["ak"] + sizes["ak"]) // PROJ_TN))
    sig_start = cols["m_att"] // PROJ_TN
    return w_main, w_small, cg.reshape(1, -1), cg2.reshape(1, -1), cols, norm_tiles, sig_start


def kernel(x, c, ada_w, ada_b, norm_mix, norm_ffn, w_in, q_norm, k_norm, idx_k_norm, w_gla_gate,
           b_gla_gate, gla_out_norm, w_out_attn, w_out_gla, w_out, ffn_w1, ffn_w3, ffn_w2,
           moe_router, moe_w1, moe_w3, moe_w2):
    batch, seq, d = x.shape
    depth = ada_w.shape[0]
    n = batch * seq
    x2 = x.reshape(n, d)

    c8 = jnp.zeros((8, d), F32).at[:batch].set(c)
    mod_all = _ada(c8, ada_w, ada_b)

    for layer in range(depth):
        mod = mod_all[layer, :batch].reshape(batch, 6, 1, d)
        shift_m, scale_m, gate_m = mod[:, 0], mod[:, 1], mod[:, 2]
        shift_f, scale_f, gate_f = mod[:, 3], mod[:, 4], mod[:, 5]

        w_main, w_small, cg, cg2, cols, norm_tiles, sig_start = _prep_in_proj(
            w_in[layer], q_norm[layer], k_norm[layer], idx_k_norm[layer], d)
        proj, ik, sm = _modproj(x2, norm_mix[layer].reshape(1, d), scale_m, shift_m,
                                w_main, cg, w_small, cg2, seq, norm_tiles, sig_start)
        y_att = _attention(proj, ik, sm, batch, seq, cols)
        wg = jnp.zeros((LANES, w_gla_gate.shape[2]), F32).at[
            GLA_GATE_RANK:2 * GLA_GATE_RANK].set(w_gla_gate[layer])
        y_gla = _gla(proj, sm, wg, b_gla_gate[layer].reshape(1, -1),
                     gla_out_norm[layer].reshape(1, -1), batch, seq, cols)
        merged = _merge(y_att, y_gla, w_out_attn[layer].astype(BF16),
                        w_out_gla[layer].astype(BF16), proj, cols)
        x2 = _resproj(merged, w_out[layer].astype(BF16), x2, gate_m, seq)

        g_f = norm_ffn[layer].reshape(1, d)
        i = layer // 2
        if layer % 2 == 0:
            x2 = _ffn(x2, g_f, scale_f, shift_f, ffn_w1[i].astype(BF16), ffn_w3[i].astype(BF16),
                      ffn_w2[i].astype(BF16), gate_f, seq)
        else:
            w_r = jnp.zeros((d, LANES), F32).at[:, :N_EXPERTS].set(moe_router[i])
            x2 = _moe(x2, g_f, scale_f, shift_f, w_r, moe_w1[i].astype(BF16),
                      moe_w3[i].astype(BF16), moe_w2[i].astype(BF16), gate_f, seq)
    return x2.reshape(batch, seq, d)
```

```python
import functools

import jax
import jax.numpy as jnp
from jax import lax
from jax.experimental import pallas as pl
from jax.experimental.pallas import tpu as pltpu

F32 = jnp.float32
BF16 = jnp.bfloat16

ATT_HEADS = 16
ATT_KV_HEADS = 4
HEAD_DIM = 128
IDX_HEADS = 16
TOPK_MAX = 256
GLA_HEADS = 4
GLA_GATE_RANK = 16
GLA_GATE_TAU = 16.0
GLA_CHUNK = 64
N_EXPERTS = 8
EPS = 1e-6
LANES = 128
NEG_BIG = -1e30
INT_MIN = -(2 ** 31)
LOG2_E = 1.4426950408889634

PROJ_TN = 512
VMEM_LIMIT = 56 * 1024 * 1024


def _nt_dot(a, b):
    return lax.dot_general(a, b, (((1,), (1,)), ((), ())), preferred_element_type=F32)


def _rms(a):
    return a * lax.rsqrt(jnp.mean(a * a, axis=-1, keepdims=True) + EPS)


def _sigmoid(a):
    return 1.0 / (1.0 + jnp.exp(-a))


def _modulate(x, g, scale, shift):
    return _rms(x) * g * (1.0 + scale) + shift


def _ada_kernel(c_ref, w_ref, b_ref, o_ref):
    c = c_ref[...]
    cond = c * _sigmoid(c)
    o_ref[...] = jnp.dot(cond, w_ref[...], preferred_element_type=F32,
                         precision=lax.Precision.HIGHEST) + b_ref[...]


def _ada(c8, ada_w, ada_b):
    depth, d, n = ada_w.shape
    tn = 1024
    return pl.pallas_call(
        _ada_kernel,
        grid=(depth, n // tn),
        in_specs=[
            pl.BlockSpec((8, d), lambda l, j: (0, 0)),
            pl.BlockSpec((None, d, tn), lambda l, j: (l, 0, j)),
            pl.BlockSpec((None, 1, tn), lambda l, j: (l, 0, j)),
        ],
        out_specs=pl.BlockSpec((None, 8, tn), lambda l, j: (l, 0, j)),
        out_shape=jax.ShapeDtypeStruct((depth, 8, n), F32),
        name="ada_mod",
    )(c8, ada_w, ada_b.reshape(depth, 1, n))


def _modproj_kernel(x_ref, g_ref, sc_ref, sh_ref, w_ref, cg_ref, w2_ref, cg2_ref,
                    o_ref, ik_ref, sm_ref, hn_ref, *, norm_tiles, sig_tiles):
    j = pl.program_id(1)

    @pl.when(j == 0)
    def _():
        h = _modulate(x_ref[...], g_ref[...], sc_ref[...], sh_ref[...]).astype(BF16)
        hn_ref[...] = h
        small = jnp.dot(h, w2_ref[...], preferred_element_type=F32)
        ik_ref[...] = (_rms(small[:, :LANES]) * cg2_ref[:, :LANES]).astype(BF16)
        sm_ref[...] = small[:, LANES:] * cg2_ref[:, LANES:]

    acc = jnp.dot(hn_ref[...], w_ref[...], preferred_element_type=F32)
    is_norm = functools.reduce(jnp.logical_or, [j == t for t in norm_tiles])
    is_sig = jnp.logical_and(j >= sig_tiles[0], j < sig_tiles[1])

    @pl.when(is_norm)
    def _():
        for c in range(PROJ_TN // LANES):
            sl = slice(c * LANES, (c + 1) * LANES)
            o_ref[:, sl] = (_rms(acc[:, sl]) * cg_ref[:, sl]).astype(BF16)

    @pl.when(is_sig)
    def _():
        o_ref[...] = _sigmoid(acc).astype(BF16)

    @pl.when(jnp.logical_not(jnp.logical_or(is_norm, is_sig)))
    def _():
        o_ref[...] = acc.astype(BF16)


def _modproj(x2, g, scale, shift, w, cg, w2, cg2, seq, norm_tiles, sig_tiles):
    n, d = x2.shape
    ncols = w.shape[1]
    tm = min(1024, seq)
    per_b = seq // tm
    kern = functools.partial(_modproj_kernel, norm_tiles=norm_tiles, sig_tiles=sig_tiles)
    return pl.pallas_call(
        kern,
        grid=(n // tm, ncols // PROJ_TN),
        in_specs=[
            pl.BlockSpec((tm, d), lambda i, j: (i, 0)),
            pl.BlockSpec((1, d), lambda i, j: (0, 0)),
            pl.BlockSpec((None, 1, d), lambda i, j: (i // per_b, 0, 0)),
            pl.BlockSpec((None, 1, d), lambda i, j: (i // per_b, 0, 0)),
            pl.BlockSpec((d, PROJ_TN), lambda i, j: (0, j)),
            pl.BlockSpec((1, PROJ_TN), lambda i, j: (0, j)),
            pl.BlockSpec((d, 2 * LANES), lambda i, j: (0, 0)),
            pl.BlockSpec((1, 2 * LANES), lambda i, j: (0, 0)),
        ],
        out_specs=[
            pl.BlockSpec((tm, PROJ_TN), lambda i, j: (i, j)),
            pl.BlockSpec((tm, LANES), lambda i, j: (i, 0)),
            pl.BlockSpec((tm, LANES), lambda i, j: (i, 0)),
        ],
        out_shape=[
            jax.ShapeDtypeStruct((n, ncols), BF16),
            jax.ShapeDtypeStruct((n, LANES), BF16),
            jax.ShapeDtypeStruct((n, LANES), F32),
        ],
        scratch_shapes=[pltpu.VMEM((tm, d), BF16)],
        compiler_params=pltpu.CompilerParams(
            dimension_semantics=("arbitrary", "arbitrary"), vmem_limit_bytes=VMEM_LIMIT),
        name="modproj",
    )(x2, g, scale, shift, w, cg, w2, cg2)


ATT_TQ = 128
ATT_TK = 256


def _sortable(a):
    a = jnp.where(a == 0.0, 0.0, a)
    bits = pltpu.bitcast(a, jnp.int32)
    return bits ^ ((bits >> 31) & 0x7FFFFFFF)


def _attn_kernel(aq_ref, iq_ref, ak_ref, av_ref, ik_ref, iw_ref, o_ref,
                 sc_ref, qs_ref, wb_ref, m_ref, acc_ref, *, topk, seq):
    tq, tk = ATT_TQ, ATT_TK
    group = ATT_HEADS // ATT_KV_HEADS
    qi = pl.program_id(1)
    n_kt = (qi * tq + tq + tk - 1) // tk
    row = qi * tq + lax.broadcasted_iota(jnp.int32, (tq, 1), 0)
    lane_col = lax.broadcasted_iota(jnp.int32, (1, tk), 1)
    for h in range(IDX_HEADS):
        wb_ref[h] = jnp.broadcast_to(iw_ref[:, h:h + 1], (tq, LANES))
    for h in range(ATT_HEADS):
        qs_ref[h * tq:(h + 1) * tq, :] = aq_ref[:, h * HEAD_DIM:(h + 1) * HEAD_DIM]

    n_pair = (n_kt + 1) // 2

    def score_pair(i, carry):
        kblk = ik_ref[pl.ds(pl.multiple_of(i * 2 * tk, 2 * tk), 2 * tk), :]
        acc = jnp.zeros((tq, 2 * tk), F32)
        for h in range(IDX_HEADS):
            z = _nt_dot(iq_ref[:, h * HEAD_DIM:(h + 1) * HEAD_DIM], kblk)
            wb = wb_ref[h]
            acc = acc + jnp.concatenate([wb] * (2 * tk // LANES), axis=1) * jnp.maximum(z, 0.0)
        for u in range(2):
            kt = 2 * i + u
            causal = (kt * tk + lane_col) <= row
            sc_ref[kt] = jnp.where(causal, _sortable(acc[:, u * tk:(u + 1) * tk]), INT_MIN)
        return carry

    lax.fori_loop(0, n_pair, score_pair, 0)

    def count_where(pred):
        def body(i, c):
            for u in range(2):
                kt = 2 * i + u
                hit = jnp.where(pred(sc_ref[kt], kt), 1.0, 0.0)
                for cc in range(tk // LANES):
                    c = c + hit[:, cc * LANES:(cc + 1) * LANES]
            return c
        c = lax.fori_loop(0, n_pair, body, jnp.zeros((tq, LANES), F32))
        return jnp.sum(c, axis=-1, keepdims=True)

    def count_ge(t):
        return count_where(lambda key, kt: key >= t)

    kf = float(topk)
    t0 = jnp.where(count_ge(jnp.zeros((tq, 1), jnp.int32)) >= kf, 0, INT_MIN).astype(jnp.int32)

    def bit_step(i, t):
        cand = t | jnp.left_shift(jnp.int32(1), 30 - i)
        return jnp.where(count_ge(cand) >= kf, cand, t)

    thr = lax.fori_loop(0, 31, bit_step, t0)

    excess = jnp.logical_and(count_ge(thr) > kf, thr > INT_MIN)

    @pl.when(jnp.max(jnp.where(excess, 1.0, 0.0)) > 0.0)
    def _():
        need = kf - count_where(lambda key, kt: key > thr)

        def ties_below(j):
            return count_where(
                lambda key, kt: jnp.logical_and(key == thr, (kt * tk + lane_col) < j))

        nbits = max(seq - 1, 1).bit_length()

        def idx_step(i, j0):
            cand = j0 | jnp.left_shift(jnp.int32(1), nbits - 1 - i)
            return jnp.where(ties_below(cand) < need, cand, j0)

        j0 = lax.fori_loop(0, nbits, idx_step, jnp.zeros((tq, 1), jnp.int32))

        def demote(kt, carry):
            key = sc_ref[kt]
            late_tie = jnp.logical_and(key == thr, (kt * tk + lane_col) > j0)
            sc_ref[kt] = jnp.where(jnp.logical_and(excess, late_tie), INT_MIN, key)
            return carry

        lax.fori_loop(0, n_kt, demote, 0)

    thr = jnp.maximum(thr, INT_MIN + 1)

    m_ref[...] = jnp.full(m_ref.shape, NEG_BIG, F32)
    acc_ref[...] = jnp.zeros(acc_ref.shape, F32)
    ones = jnp.ones((tk, HEAD_DIM), BF16)

    def kv_tile(kt, carry):
        r0 = pl.multiple_of(kt * tk, tk)
        keep = (sc_ref[kt] >= thr)[None]
        for g in range(ATT_KV_HEADS):
            q4 = qs_ref[g * group * tq:(g + 1) * group * tq, :]
            kblk = ak_ref[pl.ds(r0, tk), g * HEAD_DIM:(g + 1) * HEAD_DIM]
            vblk = av_ref[pl.ds(r0, tk), g * HEAD_DIM:(g + 1) * HEAD_DIM]
            v1 = jnp.concatenate([vblk, ones], axis=1)
            s = _nt_dot(q4, kblk).reshape(group, tq, tk)
            s = jnp.where(keep, s, NEG_BIG).reshape(group * tq, tk)
            m_old = m_ref[g]
            m_new = jnp.maximum(m_old, jnp.max(s, axis=-1, keepdims=True))
            alpha = jnp.exp2(m_old - m_new)
            p = jnp.exp2(s - jnp.concatenate([m_new] * (tk // LANES), axis=1))
            pv = jnp.dot(p.astype(BF16), v1, preferred_element_type=F32)
            acc_ref[g] = jnp.concatenate([alpha, alpha], axis=1) * acc_ref[g] + pv
            m_ref[g] = m_new
        return carry

    lax.fori_loop(0, n_kt, kv_tile, 0)

    for g in range(ATT_KV_HEADS):
        og = acc_ref[g]
        og = og[:, :HEAD_DIM] / og[:, HEAD_DIM:]
        for r in range(group):
            h = g * group + r
            o_ref[:, h * HEAD_DIM:(h + 1) * HEAD_DIM] = og[r * tq:(r + 1) * tq, :].astype(BF16)


def _attention(proj, ik, sm, batch, seq, cols):
    n = proj.shape[0]
    tq, tk = ATT_TQ, ATT_TK
    nq = seq // tq
    width = ATT_HEADS * HEAD_DIM
    kvw = ATT_KV_HEADS * HEAD_DIM
    topk = min(TOPK_MAX, seq // 4)
    group = ATT_HEADS // ATT_KV_HEADS
    kern = functools.partial(_attn_kernel, topk=topk, seq=seq)
    once = pl.Buffered(1)
    return pl.pallas_call(
        kern,
        grid=(batch, nq),
        in_specs=[
            pl.BlockSpec((tq, width), lambda b, q: (b * nq + q, cols["aq"] // width)),
            pl.BlockSpec((tq, width), lambda b, q: (b * nq + q, cols["iq"] // width)),
            pl.BlockSpec((seq, kvw), lambda b, q: (b, cols["ak"] // kvw), pipeline_mode=once),
            pl.BlockSpec((seq, kvw), lambda b, q: (b, cols["av"] // kvw), pipeline_mode=once),
            pl.BlockSpec((seq, LANES), lambda b, q: (b, 0), pipeline_mode=once),
            pl.BlockSpec((tq, LANES), lambda b, q: (b * nq + q, 0)),
        ],
        out_specs=pl.BlockSpec((tq, width), lambda b, q: (b * nq + q, 0)),
        out_shape=jax.ShapeDtypeStruct((n, width), BF16),
        scratch_shapes=[
            pltpu.VMEM((seq // tk, tq, tk), jnp.int32),
            pltpu.VMEM((ATT_HEADS * tq, HEAD_DIM), BF16),
            pltpu.VMEM((IDX_HEADS, tq, LANES), F32),
            pltpu.VMEM((ATT_KV_HEADS, group * tq, LANES), F32),
            pltpu.VMEM((ATT_KV_HEADS, group * tq, 2 * HEAD_DIM), F32),
        ],
        compiler_params=pltpu.CompilerParams(
            dimension_semantics=("arbitrary", "arbitrary"), vmem_limit_bytes=VMEM_LIMIT),
        name="dsa_attention",
    )(proj, proj, proj, proj, ik, sm)


GLA_T = 256


def _gla_kernel(q_ref, k_ref, v_ref, r_ref, lr_ref, wg_ref, bg_ref, gn_ref, o_ref, st_ref, *, dk, dv):
    c_len = GLA_CHUNK
    hi = lax.Precision.HIGHEST

    @pl.when(pl.program_id(1) == 0)
    def _():
        st_ref[...] = jnp.zeros(st_ref.shape, F32)

    ri = lax.broadcasted_iota(jnp.int32, (c_len, c_len), 0)
    ci = lax.broadcasted_iota(jnp.int32, (c_len, c_len), 1)
    lower = ri >= ci
    tril = jnp.where(lower, 1.0, 0.0).astype(F32)
    qscale = dk ** -0.5

    for c in range(GLA_T // c_len):
        rows = slice(c * c_len, (c + 1) * c_len)
        gl = jnp.dot(lr_ref[rows, :], wg_ref[...], preferred_element_type=F32, precision=hi) + bg_ref[...]
        log_a = (jnp.minimum(gl, 0.0) - jnp.log(1.0 + jnp.exp(-jnp.abs(gl)))) / GLA_GATE_TAU
        b = jnp.dot(tril, log_a, preferred_element_type=F32, precision=hi)
        b_last = b[c_len - 1:c_len, :]
        k = k_ref[rows, :].astype(F32)
        q_dec = (q_ref[rows, :].astype(F32) * qscale * jnp.exp(b)).astype(BF16)
        k_inv = (k * jnp.exp(-b)).astype(BF16)
        k_tail = k * jnp.exp(b_last - b)
        for h in range(GLA_HEADS):
            ks = slice(h * dk, (h + 1) * dk)
            vs = slice(h * dv, (h + 1) * dv)
            v = v_ref[rows, vs]
            a = jnp.where(lower, _nt_dot(q_dec[:, ks], k_inv[:, ks]), 0.0)
            state = st_ref[h]
            o = (jnp.dot(a.astype(BF16), v, preferred_element_type=F32)
                 + jnp.dot(q_dec[:, ks], state.astype(BF16), preferred_element_type=F32))
            dec = jnp.exp(b[:, ks].T[:, c_len - 1:c_len])
            upd = jnp.dot(k_tail[:, ks].T.astype(BF16), v, preferred_element_type=F32)
            for cc in range(dv // LANES):
                sl = slice(cc * LANES, (cc + 1) * LANES)
                st_ref[h, :, sl] = dec * state[:, sl] + upd[:, sl]
            r = r_ref[rows, vs].astype(F32)
            o_ref[rows, vs] = (_rms(o) * gn_ref[...] * (r * _sigmoid(r))).astype(BF16)


def _gla(proj, sm, wg, bg, gn, batch, seq, cols):
    n = proj.shape[0]
    dk = wg.shape[1] // GLA_HEADS
    dv = gn.shape[1]
    t = min(GLA_T, seq)
    assert t == GLA_T
    ns = seq // t
    kw, vw = GLA_HEADS * dk, GLA_HEADS * dv
    kern = functools.partial(_gla_kernel, dk=dk, dv=dv)
    return pl.pallas_call(
        kern,
        grid=(batch, ns),
        in_specs=[
            pl.BlockSpec((t, kw), lambda b, s: (b * ns + s, cols["gq"] // kw)),
            pl.BlockSpec((t, kw), lambda b, s: (b * ns + s, cols["gk"] // kw)),
            pl.BlockSpec((t, vw), lambda b, s: (b * ns + s, cols["gv"] // vw)),
            pl.BlockSpec((t, vw), lambda b, s: (b * ns + s, cols["gr"] // vw)),
            pl.BlockSpec((t, LANES), lambda b, s: (b * ns + s, 0)),
            pl.BlockSpec((LANES, kw), lambda b, s: (0, 0)),
            pl.BlockSpec((1, kw), lambda b, s: (0, 0)),
            pl.BlockSpec((1, dv), lambda b, s: (0, 0)),
        ],
        out_specs=pl.BlockSpec((t, vw), lambda b, s: (b * ns + s, 0)),
        out_shape=jax.ShapeDtypeStruct((n, vw), BF16),
        scratch_shapes=[pltpu.VMEM((GLA_HEADS, dk, dv), F32)],
        compiler_params=pltpu.CompilerParams(
            dimension_semantics=("arbitrary", "arbitrary"),
            vmem_limit_bytes=VMEM_LIMIT),
        name="gla",
    )(proj, proj, proj, proj, sm, wg, bg, gn)


def _merge_kernel(ya_ref, yg_ref, wa_ref, wg_ref, ma_ref, mg_ref, o_ref):
    a = jnp.dot(ya_ref[...], wa_ref[...], preferred_element_type=F32)
    g = jnp.dot(yg_ref[...], wg_ref[...], preferred_element_type=F32)
    o_ref[...] = (ma_ref[...].astype(F32) * a + mg_ref[...].astype(F32) * g).astype(BF16)


def _merge(y_att, y_gla, w_att, w_gla, proj, cols):
    n, d_att = y_att.shape
    d_gla = y_gla.shape[1]
    d = w_att.shape[1]
    tm, tn = 1024, PROJ_TN
    return pl.pallas_call(
        _merge_kernel,
        grid=(n // tm, d // tn),
        in_specs=[
            pl.BlockSpec((tm, d_att), lambda i, j: (i, 0)),
            pl.BlockSpec((tm, d_gla), lambda i, j: (i, 0)),
            pl.BlockSpec((d_att, tn), lambda i, j: (0, j)),
            pl.BlockSpec((d_gla, tn), lambda i, j: (0, j)),
            pl.BlockSpec((tm, tn), lambda i, j: (i, cols["m_att"] // tn + j)),
            pl.BlockSpec((tm, tn), lambda i, j: (i, cols["m_gla"] // tn + j)),
        ],
        out_specs=pl.BlockSpec((tm, tn), lambda i, j: (i, j)),
        out_shape=jax.ShapeDtypeStruct((n, d), BF16),
        compiler_params=pltpu.CompilerParams(
            dimension_semantics=("arbitrary", "arbitrary"), vmem_limit_bytes=VMEM_LIMIT),
        name="merge",
    )(y_att, y_gla, w_att, w_gla, proj, proj)


def _resproj_kernel(m_ref, w_ref, x_ref, gate_ref, o_ref):
    y = jnp.dot(m_ref[...], w_ref[...], preferred_element_type=F32)
    o_ref[...] = x_ref[...] + gate_ref[...] * y


def _resproj(merged, w, x2, gate, seq):
    n, d = x2.shape
    tm, tn = 1024, PROJ_TN
    per_b = seq // tm
    return pl.pallas_call(
        _resproj_kernel,
        grid=(n // tm, d // tn),
        in_specs=[
            pl.BlockSpec((tm, merged.shape[1]), lambda i, j: (i, 0)),
            pl.BlockSpec((merged.shape[1], tn), lambda i, j: (0, j)),
            pl.BlockSpec((tm, tn), lambda i, j: (i, j)),
            pl.BlockSpec((None, 1, tn), lambda i, j: (i // per_b, 0, j)),
        ],
        out_specs=pl.BlockSpec((tm, tn), lambda i, j: (i, j)),
        out_shape=jax.ShapeDtypeStruct((n, d), F32),
        compiler_params=pltpu.CompilerParams(
            dimension_semantics=("arbitrary", "arbitrary"), vmem_limit_bytes=VMEM_LIMIT),
        name="resproj",
    )(merged, w, x2, gate)


FFN_TM = 512
FFN_TF = 512


def _swiglu_step(hn, w1, w3, w2):
    a = jnp.dot(hn, w1, preferred_element_type=F32)
    b = jnp.dot(hn, w3, preferred_element_type=F32)
    return a * _sigmoid(a) * b, w2


def _ffn_kernel(x_ref, g_ref, sc_ref, sh_ref, w1_ref, w3_ref, w2_ref, gate_ref, o_ref,
                hn_ref, acc_ref):
    j = pl.program_id(1)

    @pl.when(j == 0)
    def _():
        hn_ref[...] = _modulate(x_ref[...], g_ref[...], sc_ref[...], sh_ref[...]).astype(BF16)
        acc_ref[...] = jnp.zeros(acc_ref.shape, F32)

    act, w2 = _swiglu_step(hn_ref[...], w1_ref[...], w3_ref[...], w2_ref[...])
    acc_ref[...] += jnp.dot(act.astype(BF16), w2, preferred_element_type=F32)

    @pl.when(j == pl.num_programs(1) - 1)
    def _():
        o_ref[...] = x_ref[...] + gate_ref[...] * acc_ref[...]


def _ffn(x2, g, scale, shift, w1, w3, w2, gate, seq):
    n, d = x2.shape
    dff = w1.shape[1]
    tm, tf = min(FFN_TM, seq), FFN_TF
    per_b = seq // tm
    vec = pl.BlockSpec((None, 1, d), lambda i, j: (i // per_b, 0, 0))
    return pl.pallas_call(
        _ffn_kernel,
        grid=(n // tm, dff // tf),
        in_specs=[
            pl.BlockSpec((tm, d), lambda i, j: (i, 0)),
            pl.BlockSpec((1, d), lambda i, j: (0, 0)),
            vec, vec,
            pl.BlockSpec((d, tf), lambda i, j: (0, j)),
            pl.BlockSpec((d, tf), lambda i, j: (0, j)),
            pl.BlockSpec((tf, d), lambda i, j: (j, 0)),
            vec,
        ],
        out_specs=pl.BlockSpec((tm, d), lambda i, j: (i, 0)),
        out_shape=jax.ShapeDtypeStruct((n, d), F32),
        scratch_shapes=[pltpu.VMEM((tm, d), BF16), pltpu.VMEM((tm, d), F32)],
        compiler_params=pltpu.CompilerParams(
            dimension_semantics=("arbitrary", "arbitrary"), vmem_limit_bytes=VMEM_LIMIT),
        name="ffn_swiglu",
    )(x2, g, scale, shift, w1, w3, w2, gate)


MOE_TM = 512
ROUTE_TM = 1024


def _router_kernel(x_ref, g_ref, sc_ref, sh_ref, wr_ref, hn_ref, rt_ref):
    lane = lax.broadcasted_iota(jnp.int32, (1, LANES), 1)
    lane_f = lane.astype(F32)
    h = _modulate(x_ref[...], g_ref[...], sc_ref[...], sh_ref[...])
    hn_ref[...] = h
    logits = jnp.dot(h, wr_ref[...], preferred_element_type=F32, precision=lax.Precision.HIGHEST)
    logits = jnp.where(lane < N_EXPERTS, logits, -jnp.inf)
    m1 = jnp.max(logits, axis=-1, keepdims=True)
    i1 = jnp.min(jnp.where(logits == m1, lane_f, float(LANES)), axis=-1, keepdims=True)
    rest = jnp.where(lane_f == i1, -jnp.inf, logits)
    m2 = jnp.max(rest, axis=-1, keepdims=True)
    i2 = jnp.min(jnp.where(rest == m2, lane_f, float(LANES)), axis=-1, keepdims=True)
    e2 = jnp.exp(m2 - m1)
    p1 = 1.0 / (1.0 + e2)
    p2 = e2 / (1.0 + e2)
    rt_ref[...] = (jnp.where(lane == 0, i1, 0.0) + jnp.where(lane == 1, i2, 0.0)
                   + jnp.where(lane == 2, p1, 0.0) + jnp.where(lane == 3, p2, 0.0))


def _router(x2, g, scale, shift, w_router, seq):
    n, d = x2.shape
    tm = min(ROUTE_TM, seq)
    per_b = seq // tm
    vec = pl.BlockSpec((None, 1, d), lambda i: (i // per_b, 0, 0))
    return pl.pallas_call(
        _router_kernel,
        grid=(n // tm,),
        in_specs=[
            pl.BlockSpec((tm, d), lambda i: (i, 0)),
            pl.BlockSpec((1, d), lambda i: (0, 0)),
            vec, vec,
            pl.BlockSpec((d, LANES), lambda i: (0, 0)),
        ],
        out_specs=[pl.BlockSpec((tm, d), lambda i: (i, 0)),
                   pl.BlockSpec((tm, LANES), lambda i: (i, 0))],
        out_shape=[jax.ShapeDtypeStruct((n, d), F32), jax.ShapeDtypeStruct((n, LANES), F32)],
        compiler_params=pltpu.CompilerParams(
            dimension_semantics=("arbitrary",), vmem_limit_bytes=VMEM_LIMIT),
        name="moe_router",
    )(x2, g, scale, shift, w_router)


def _route_tables(rt, n, n_exp, tm):
    rows = 2 * n + n_exp * tm
    n_tiles = rows // tm
    ef = rt[:, :2].astype(jnp.int32).T.reshape(-1)
    onehot = (ef[:, None] == jnp.arange(n_exp, dtype=jnp.int32)[None, :]).astype(jnp.int32)
    csum = jnp.cumsum(onehot, axis=0)
    rank = jnp.take_along_axis(csum, ef[:, None], axis=1)[:, 0] - 1
    cnt = csum[-1]
    padded = ((cnt + tm - 1) // tm) * tm
    ends = jnp.cumsum(padded)
    dest = (ends - padded)[ef] + rank
    total = ends[-1]
    tile_start = jnp.arange(n_tiles, dtype=jnp.int32) * tm
    tile_valid = (tile_start < total).astype(jnp.int32)
    tile_expert = jnp.sum((tile_start[:, None] >= ends[None, :]).astype(jnp.int32), axis=1)
    last_expert = tile_expert[jnp.maximum(total // tm - 1, 0)]
    tile_expert = jnp.where(tile_valid == 1, tile_expert, last_expert)
    pair = jnp.arange(2 * n, dtype=jnp.int32)
    src_token = jnp.zeros((rows,), jnp.int32).at[dest].set(pair % n, unique_indices=True)
    dst_real = jnp.full((rows,), -1, jnp.int32).at[dest].set(pair, unique_indices=True)
    is_pad = dst_real < 0
    pad_slot = 2 * n + jnp.cumsum(is_pad.astype(jnp.int32)) - 1
    dst_row = jnp.where(is_pad, pad_slot, dst_real)
    return tile_expert, tile_valid, src_token, dst_row, rows


def _moe_grouped_kernel(te_ref, tv_ref, src_ref, dst_ref, hn_hbm, w1_ref, w3_ref, w2_ref, y_hbm,
                        xg_ref, xb_ref, acc_ref, sem_in, sem_out):
    t = pl.program_id(0)
    j = pl.program_id(1)
    tm = xg_ref.shape[0]
    valid = tv_ref[t] == 1

    def gather_row(r):
        return pltpu.make_async_copy(hn_hbm.at[pl.ds(src_ref[t * tm + r], 1)],
                                     xg_ref.at[pl.ds(r, 1)], sem_in)

    def scatter_row(r):
        return pltpu.make_async_copy(acc_ref.at[pl.ds(r, 1)],
                                     y_hbm.at[pl.ds(dst_ref[t * tm + r], 1)], sem_out)

    def start_rows(copy):
        def body(r, carry):
            copy(r).start()
            return carry
        lax.fori_loop(0, tm, body, 0, unroll=8)

    @pl.when(jnp.logical_and(valid, j == 0))
    def _():
        start_rows(gather_row)
        pltpu.make_async_copy(hn_hbm.at[pl.ds(0, tm)], xg_ref, sem_in).wait()
        xb_ref[...] = xg_ref[...].astype(BF16)
        acc_ref[...] = jnp.zeros(acc_ref.shape, F32)

    @pl.when(valid)
    def _():
        act, w2 = _swiglu_step(xb_ref[...], w1_ref[...], w3_ref[...], w2_ref[...])
        acc_ref[...] += jnp.dot(act.astype(BF16), w2, preferred_element_type=F32)

    @pl.when(jnp.logical_and(valid, j == pl.num_programs(1) - 1))
    def _():
        start_rows(scatter_row)
        pltpu.make_async_copy(acc_ref, y_hbm.at[pl.ds(0, tm)], sem_out).wait()

    @pl.when(jnp.logical_and(jnp.logical_not(valid), j == 0))
    def _():
        acc_ref[...] = jnp.zeros(acc_ref.shape, F32)
        fill = pltpu.make_async_copy(
            acc_ref, y_hbm.at[pl.ds(pl.multiple_of(t * tm, tm), tm)], sem_out)
        fill.start()
        fill.wait()


def _moe_grouped(hn, tables, w1, w3, w2):
    tile_expert, tile_valid, src_token, dst_row, rows = tables
    n, d = hn.shape
    dff = w1.shape[2]
    tm, tf = MOE_TM, FFN_TF
    nj = dff // tf

    def jj(t, j, tv):
        return j * tv[t] + (nj - 1) * (1 - tv[t])

    grid_spec = pltpu.PrefetchScalarGridSpec(
        num_scalar_prefetch=4,
        grid=(rows // tm, nj),
        in_specs=[
            pl.BlockSpec(memory_space=pl.ANY),
            pl.BlockSpec((None, d, tf), lambda t, j, te, tv, s, dd: (te[t], 0, jj(t, j, tv))),
            pl.BlockSpec((None, d, tf), lambda t, j, te, tv, s, dd: (te[t], 0, jj(t, j, tv))),
            pl.BlockSpec((None, tf, d), lambda t, j, te, tv, s, dd: (te[t], jj(t, j, tv), 0)),
        ],
        out_specs=pl.BlockSpec(memory_space=pl.ANY),
        scratch_shapes=[pltpu.VMEM((tm, d), F32), pltpu.VMEM((tm, d), BF16),
                        pltpu.VMEM((tm, d), F32),
                        pltpu.SemaphoreType.DMA(()), pltpu.SemaphoreType.DMA(())],
    )
    return pl.pallas_call(
        _moe_grouped_kernel,
        grid_spec=grid_spec,
        out_shape=jax.ShapeDtypeStruct((rows, d), F32),
        compiler_params=pltpu.CompilerParams(
            dimension_semantics=("arbitrary", "arbitrary"), vmem_limit_bytes=VMEM_LIMIT),
        name="moe_grouped",
    )(tile_expert, tile_valid, src_token, dst_row, hn, w1, w3, w2)


def _moe_combine_kernel(x_ref, y0_ref, y1_ref, rt_ref, gate_ref, o_ref):
    p1 = rt_ref[:, 2:3]
    p2 = rt_ref[:, 3:4]
    o_ref[...] = x_ref[...] + gate_ref[...] * (p1 * y0_ref[...] + p2 * y1_ref[...])


def _moe_combine(x2, y, rt, gate, seq):
    n, d = x2.shape
    tm = min(FFN_TM, seq)
    per_b = seq // tm
    nb = n // tm
    return pl.pallas_call(
        _moe_combine_kernel,
        grid=(nb,),
        in_specs=[
            pl.BlockSpec((tm, d), lambda i: (i, 0)),
            pl.BlockSpec((tm, d), lambda i: (i, 0)),
            pl.BlockSpec((tm, d), lambda i: (nb + i, 0)),
            pl.BlockSpec((tm, LANES), lambda i: (i, 0)),
            pl.BlockSpec((None, 1, d), lambda i: (i // per_b, 0, 0)),
        ],
        out_specs=pl.BlockSpec((tm, d), lambda i: (i, 0)),
        out_shape=jax.ShapeDtypeStruct((n, d), F32),
        compiler_params=pltpu.CompilerParams(
            dimension_semantics=("arbitrary",), vmem_limit_bytes=VMEM_LIMIT),
        name="moe_combine",
    )(x2, y, y, rt, gate)


def _moe(x2, g, scale, shift, w_router, w1, w3, w2, gate, seq):
    n = x2.shape[0]
    hn, rt = _router(x2, g, scale, shift, w_router, seq)
    tables = _route_tables(rt, n, w1.shape[0], MOE_TM)
    y = _moe_grouped(hn, tables, w1, w3, w2)
    return _moe_combine(x2, y, rt, gate, seq)


def _prep_in_proj(w_in, q_norm, k_norm, idx_k_norm, d):
    sizes = dict(aq=ATT_HEADS * HEAD_DIM, ak=ATT_KV_HEADS * HEAD_DIM, av=ATT_KV_HEADS * HEAD_DIM,
                 iq=IDX_HEADS * HEAD_DIM, ik=HEAD_DIM, iw=IDX_HEADS,
                 gq=d // 2, gk=d // 2, gv=d, glr=GLA_GATE_RANK, gr=d, m_att=d, m_gla=d)
    src, acc = {}, 0
    for name in ("aq", "ak", "av", "iq", "ik", "iw", "gq", "gk", "gv", "glr", "gr", "m_att", "m_gla"):
        src[name] = (acc, sizes[name])
        acc += sizes[name]
    order = ("aq", "iq", "gv", "gr", "m_att", "m_gla", "ak", "av", "gq", "gk")
    cols, parts, off = {}, [], 0
    for name in order:
        s, width = src[name]
        cols[name] = off
        parts.append(w_in[:, s:s + width])
        off += width
    w_main = jnp.concatenate(parts, axis=1).astype(BF16)

    def seg(name):
        s, width = src[name]
        return w_in[:, s:s + width]

    pad = jnp.zeros((d, LANES - IDX_HEADS - GLA_GATE_RANK), w_in.dtype)
    w_small = jnp.concatenate([seg("ik"), seg("iw"), seg("glr"), pad], axis=1).astype(BF16)

    att_scale = HEAD_DIM ** -0.5 * LOG2_E
    idx_scale = (HEAD_DIM ** -0.5) * (IDX_HEADS ** -0.5)
    cg = jnp.ones((off,), F32)
    cg = cg.at[cols["aq"]:cols["aq"] + sizes["aq"]].set(jnp.tile(q_norm * att_scale, ATT_HEADS))
    cg = cg.at[cols["ak"]:cols["ak"] + sizes["ak"]].set(jnp.tile(k_norm, ATT_KV_HEADS))
    cg2 = jnp.concatenate([idx_k_norm, jnp.full((IDX_HEADS,), idx_scale, F32),
                           jnp.ones((LANES - IDX_HEADS,), F32)])
    norm_tiles = tuple(range(cols["aq"] // PROJ_TN, (cols["aq"] + sizes["aq"]) // PROJ_TN)) + \
        tuple(range(cols["ak"] // PROJ_TN, (cols["ak"] + sizes["ak"]) // PROJ_TN))
    sig_tiles = (cols["m_att"] // PROJ_TN, (cols["m_gla"] + sizes["m_gla"]) // PROJ_TN)
    return w_main, w_small, cg.reshape(1, -1), cg2.reshape(1, -1), cols, norm_tiles, sig_tiles


def kernel(x, c, ada_w, ada_b, norm_mix, norm_ffn, w_in, q_norm, k_norm, idx_k_norm, w_gla_gate,
           b_gla_gate, gla_out_norm, w_out_attn, w_out_gla, w_out, ffn_w1, ffn_w3, ffn_w2,
           moe_router, moe_w1, moe_w3, moe_w2):
    batch, seq, d = x.shape
    depth = ada_w.shape[0]
    n = batch * seq
    x2 = x.reshape(n, d)

    c8 = jnp.zeros((8, d), F32).at[:batch].set(c)
    mod_all = _ada(c8, ada_w, ada_b)

    for layer in range(depth):
        mod = mod_all[layer, :batch].reshape(batch, 6, 1, d)
        shift_m, scale_m, gate_m = mod[:, 0], mod[:, 1], mod[:, 2]
        shift_f, scale_f, gate_f = mod[:, 3], mod[:, 4], mod[:, 5]

        w_main, w_small, cg, cg2, cols, norm_tiles, sig_tiles = _prep_in_proj(
            w_in[layer], q_norm[layer], k_norm[layer], idx_k_norm[layer], d)
        proj, ik, sm = _modproj(x2, norm_mix[layer].reshape(1, d), scale_m, shift_m,
                                w_main, cg, w_small, cg2, seq, norm_tiles, sig_tiles)
        y_att = _attention(proj, ik, sm, batch, seq, cols)
        wg = jnp.zeros((LANES, w_gla_gate.shape[2]), F32).at[
            GLA_GATE_RANK:2 * GLA_GATE_RANK].set(w_gla_gate[layer])
        y_gla = _gla(proj, sm, wg, b_gla_gate[layer].reshape(1, -1),
                     gla_out_norm[layer].reshape(1, -1), batch, seq, cols)
        merged = _merge(y_att, y_gla, w_out_attn[layer].astype(BF16),
                        w_out_gla[layer].astype(BF16), proj, cols)
        x2 = _resproj(merged, w_out[layer].astype(BF16), x2, gate_m, seq)

        g_f = norm_ffn[layer].reshape(1, d)
        i = layer // 2
        if layer % 2 == 0:
            x2 = _ffn(x2, g_f, scale_f, shift_f, ffn_w1[i].astype(BF16), ffn_w3[i].astype(BF16),
                      ffn_w2[i].astype(BF16), gate_f, seq)
        else:
            w_r = jnp.zeros((d, LANES), F32).at[:, :N_EXPERTS].set(moe_router[i])
            x2 = _moe(x2, g_f, scale_f, shift_f, w_r, moe_w1[i].astype(BF16),
                      moe_w3[i].astype(BF16), moe_w2[i].astype(BF16), gate_f, seq)
    return x2.reshape(batch, seq, d)
```

```python
import functools

import jax
import jax.numpy as jnp
from jax import lax
from jax.experimental import pallas as pl
from jax.experimental.pallas import tpu as pltpu

F32 = jnp.float32
BF16 = jnp.bfloat16

ATT_HEADS = 16
ATT_KV_HEADS = 4
HEAD_DIM = 128
IDX_HEADS = 16
TOPK_MAX = 256
GLA_HEADS = 4
GLA_GATE_RANK = 16
GLA_GATE_TAU = 16.0
GLA_CHUNK = 64
N_EXPERTS = 8
EPS = 1e-6
LANES = 128
NEG_BIG = -1e30
INT_MIN = -(2 ** 31)
LOG2_E = 1.4426950408889634

PROJ_TN = 512
VMEM_LIMIT = 56 * 1024 * 1024


def _nt_dot(a, b):
    return lax.dot_general(a, b, (((1,), (1,)), ((), ())), preferred_element_type=F32)


def _rms(a):
    return a * lax.rsqrt(jnp.mean(a * a, axis=-1, keepdims=True) + EPS)


def _sigmoid(a):
    return 1.0 / (1.0 + jnp.exp(-a))


def _modulate(x, g, scale, shift):
    return _rms(x) * g * (1.0 + scale) + shift


def _ada_kernel(c_ref, w_ref, b_ref, o_ref):
    c = c_ref[...]
    cond = c * _sigmoid(c)
    o_ref[...] = jnp.dot(cond, w_ref[...], preferred_element_type=F32,
                         precision=lax.Precision.HIGHEST) + b_ref[...]


def _ada(c8, ada_w, ada_b):
    depth, d, n = ada_w.shape
    tn = 1024
    return pl.pallas_call(
        _ada_kernel,
        grid=(depth, n // tn),
        in_specs=[
            pl.BlockSpec((8, d), lambda l, j: (0, 0)),
            pl.BlockSpec((None, d, tn), lambda l, j: (l, 0, j)),
            pl.BlockSpec((None, 1, tn), lambda l, j: (l, 0, j)),
        ],
        out_specs=pl.BlockSpec((None, 8, tn), lambda l, j: (l, 0, j)),
        out_shape=jax.ShapeDtypeStruct((depth, 8, n), F32),
        name="ada_mod",
    )(c8, ada_w, ada_b.reshape(depth, 1, n))


def _modproj_kernel(x_ref, g_ref, sc_ref, sh_ref, w_ref, cg_ref, w2_ref, cg2_ref,
                    o_ref, ik_ref, sm_ref, hn_ref, *, norm_tiles, sig_tiles):
    j = pl.program_id(1)

    @pl.when(j == 0)
    def _():
        h = _modulate(x_ref[...], g_ref[...], sc_ref[...], sh_ref[...]).astype(BF16)
        hn_ref[...] = h
        small = jnp.dot(h, w2_ref[...], preferred_element_type=F32)
        ik_ref[...] = (_rms(small[:, :LANES]) * cg2_ref[:, :LANES]).astype(BF16)
        sm_ref[...] = small[:, LANES:] * cg2_ref[:, LANES:]

    acc = jnp.dot(hn_ref[...], w_ref[...], preferred_element_type=F32)
    is_norm = functools.reduce(jnp.logical_or, [j == t for t in norm_tiles])
    is_sig = jnp.logical_and(j >= sig_tiles[0], j < sig_tiles[1])

    @pl.when(is_norm)
    def _():
        for c in range(PROJ_TN // LANES):
            sl = slice(c * LANES, (c + 1) * LANES)
            o_ref[:, sl] = (_rms(acc[:, sl]) * cg_ref[:, sl]).astype(BF16)

    @pl.when(is_sig)
    def _():
        o_ref[...] = _sigmoid(acc).astype(BF16)

    @pl.when(jnp.logical_not(jnp.logical_or(is_norm, is_sig)))
    def _():
        o_ref[...] = acc.astype(BF16)


def _modproj(x2, g, scale, shift, w, cg, w2, cg2, seq, norm_tiles, sig_tiles):
    n, d = x2.shape
    ncols = w.shape[1]
    tm = min(1024, seq)
    per_b = seq // tm
    kern = functools.partial(_modproj_kernel, norm_tiles=norm_tiles, sig_tiles=sig_tiles)
    return pl.pallas_call(
        kern,
        grid=(n // tm, ncols // PROJ_TN),
        in_specs=[
            pl.BlockSpec((tm, d), lambda i, j: (i, 0)),
            pl.BlockSpec((1, d), lambda i, j: (0, 0)),
            pl.BlockSpec((None, 1, d), lambda i, j: (i // per_b, 0, 0)),
            pl.BlockSpec((None, 1, d), lambda i, j: (i // per_b, 0, 0)),
            pl.BlockSpec((d, PROJ_TN), lambda i, j: (0, j)),
            pl.BlockSpec((1, PROJ_TN), lambda i, j: (0, j)),
            pl.BlockSpec((d, 2 * LANES), lambda i, j: (0, 0)),
            pl.BlockSpec((1, 2 * LANES), lambda i, j: (0, 0)),
        ],
        out_specs=[
            pl.BlockSpec((tm, PROJ_TN), lambda i, j: (i, j)),
            pl.BlockSpec((tm, LANES), lambda i, j: (i, 0)),
            pl.BlockSpec((tm, LANES), lambda i, j: (i, 0)),
        ],
        out_shape=[
            jax.ShapeDtypeStruct((n, ncols), BF16),
            jax.ShapeDtypeStruct((n, LANES), BF16),
            jax.ShapeDtypeStruct((n, LANES), F32),
        ],
        scratch_shapes=[pltpu.VMEM((tm, d), BF16)],
        compiler_params=pltpu.CompilerParams(
            dimension_semantics=("arbitrary", "arbitrary"), vmem_limit_bytes=VMEM_LIMIT),
        name="modproj",
    )(x2, g, scale, shift, w, cg, w2, cg2)


ATT_TQ = 128
ATT_TK = 256
BISECT_UNCHECKED_BITS = 18


def _sortable(a):
    a = jnp.where(a == 0.0, 0.0, a)
    bits = pltpu.bitcast(a, jnp.int32)
    return bits ^ ((bits >> 31) & 0x7FFFFFFF)


def _attn_kernel(aq_ref, iq_ref, ak_ref, av_ref, ik_ref, iw_ref, o_ref,
                 sc_ref, qs_ref, wb_ref, m_ref, acc_ref, *, topk, seq):
    tq, tk = ATT_TQ, ATT_TK
    group = ATT_HEADS // ATT_KV_HEADS
    qi = pl.program_id(1)
    n_kt = (qi * tq + tq + tk - 1) // tk
    row = qi * tq + lax.broadcasted_iota(jnp.int32, (tq, 1), 0)
    lane_col = lax.broadcasted_iota(jnp.int32, (1, tk), 1)
    for h in range(IDX_HEADS):
        wb_ref[h] = jnp.broadcast_to(iw_ref[:, h:h + 1], (tq, LANES))
    for h in range(ATT_HEADS):
        qs_ref[h * tq:(h + 1) * tq, :] = aq_ref[:, h * HEAD_DIM:(h + 1) * HEAD_DIM]

    n_pair = (n_kt + 1) // 2

    def score_pair(i, carry):
        kblk = ik_ref[pl.ds(pl.multiple_of(i * 2 * tk, 2 * tk), 2 * tk), :]
        acc = jnp.zeros((tq, 2 * tk), F32)
        for h in range(IDX_HEADS):
            z = _nt_dot(iq_ref[:, h * HEAD_DIM:(h + 1) * HEAD_DIM], kblk)
            wb = wb_ref[h]
            acc = acc + jnp.concatenate([wb] * (2 * tk // LANES), axis=1) * jnp.maximum(z, 0.0)
        for u in range(2):
            kt = 2 * i + u
            causal = (kt * tk + lane_col) <= row
            sc_ref[kt] = jnp.where(causal, _sortable(acc[:, u * tk:(u + 1) * tk]), INT_MIN)
        return carry

    lax.fori_loop(0, n_pair, score_pair, 0)

    def count_where(pred):
        def body(i, c):
            for u in range(2):
                kt = 2 * i + u
                hit = jnp.where(pred(sc_ref[kt], kt), 1.0, 0.0)
                for cc in range(tk // LANES):
                    c = c + hit[:, cc * LANES:(cc + 1) * LANES]
            return c
        c = lax.fori_loop(0, n_pair, body, jnp.zeros((tq, LANES), F32))
        return jnp.sum(c, axis=-1, keepdims=True)

    def count_ge(t):
        return count_where(lambda key, kt: key >= t)

    kf = float(topk)
    c0 = count_ge(jnp.zeros((tq, 1), jnp.int32))
    t0 = jnp.where(c0 >= kf, 0, INT_MIN).astype(jnp.int32)
    n_all = jnp.full((tq, 1), float(seq), F32)

    def bit_step(i, state):
        t, ct, open_rows = state
        cand = t | jnp.left_shift(jnp.int32(1), 30 - i)
        c = count_ge(cand)
        take = jnp.logical_and(c >= kf, open_rows > 0.0)
        t = jnp.where(take, cand, t)
        ct = jnp.where(take, c, ct)
        open_rows = jnp.where(c == kf, 0.0, open_rows)
        return t, ct, open_rows

    def any_open(open_rows):
        return (jnp.max(open_rows) > 0.0).astype(jnp.int32)

    def tail_cond(state):
        i, _, go = state
        return jnp.logical_and(i < 31, go == 1)

    def tail_step(state):
        i, inner, _ = state
        inner = bit_step(i, inner)
        return i + 1, inner, any_open(inner[2])

    state = (t0, jnp.where(c0 >= kf, c0, n_all), jnp.where(c0 == kf, 0.0, 1.0))
    state = lax.fori_loop(0, BISECT_UNCHECKED_BITS, bit_step, state)
    _, (thr, cnt_thr, open_rows), _ = lax.while_loop(
        tail_cond, tail_step, (jnp.int32(BISECT_UNCHECKED_BITS), state, any_open(state[2])))

    excess = jnp.logical_and(jnp.logical_and(open_rows > 0.0, cnt_thr > kf), thr > INT_MIN)

    @pl.when(jnp.max(jnp.where(excess, 1.0, 0.0)) > 0.0)
    def _():
        need = kf - count_where(lambda key, kt: key > thr)

        def ties_below(j):
            return count_where(
                lambda key, kt: jnp.logical_and(key == thr, (kt * tk + lane_col) < j))

        nbits = max(seq - 1, 1).bit_length()

        def idx_step(i, j0):
            cand = j0 | jnp.left_shift(jnp.int32(1), nbits - 1 - i)
            return jnp.where(ties_below(cand) < need, cand, j0)

        j0 = lax.fori_loop(0, nbits, idx_step, jnp.zeros((tq, 1), jnp.int32))

        def demote(kt, carry):
            key = sc_ref[kt]
            late_tie = jnp.logical_and(key == thr, (kt * tk + lane_col) > j0)
            sc_ref[kt] = jnp.where(jnp.logical_and(excess, late_tie), INT_MIN, key)
            return carry

        lax.fori_loop(0, n_kt, demote, 0)

    thr = jnp.maximum(thr, INT_MIN + 1)

    m_ref[...] = jnp.full(m_ref.shape, NEG_BIG, F32)
    acc_ref[...] = jnp.zeros(acc_ref.shape, F32)
    tk2 = 2 * tk
    ones = jnp.ones((tk2, HEAD_DIM), BF16)

    def kv_pair(i, carry):
        r0 = pl.multiple_of(i * tk2, tk2)
        keep = jnp.concatenate([sc_ref[2 * i] >= thr, sc_ref[2 * i + 1] >= thr], axis=1)[None]
        for g in range(ATT_KV_HEADS):
            q4 = qs_ref[g * group * tq:(g + 1) * group * tq, :]
            kblk = ak_ref[pl.ds(r0, tk2), g * HEAD_DIM:(g + 1) * HEAD_DIM]
            vblk = av_ref[pl.ds(r0, tk2), g * HEAD_DIM:(g + 1) * HEAD_DIM]
            v1 = jnp.concatenate([vblk, ones], axis=1)
            s = _nt_dot(q4, kblk).reshape(group, tq, tk2)
            s = jnp.where(keep, s, NEG_BIG).reshape(group * tq, tk2)
            m_old = m_ref[g]
            m_new = jnp.maximum(m_old, jnp.max(s, axis=-1, keepdims=True))
            alpha = jnp.exp2(m_old - m_new)
            p = jnp.exp2(s - jnp.concatenate([m_new] * (tk2 // LANES), axis=1))
            pv = jnp.dot(p.astype(BF16), v1, preferred_element_type=F32)
            acc_ref[g] = jnp.concatenate([alpha, alpha], axis=1) * acc_ref[g] + pv
            m_ref[g] = m_new
        return carry

    lax.fori_loop(0, n_pair, kv_pair, 0)

    for g in range(ATT_KV_HEADS):
        og = acc_ref[g]
        og = og[:, :HEAD_DIM] / og[:, HEAD_DIM:]
        for r in range(group):
            h = g * group + r
            o_ref[:, h * HEAD_DIM:(h + 1) * HEAD_DIM] = og[r * tq:(r + 1) * tq, :].astype(BF16)


def _attention(proj, ik, sm, batch, seq, cols):
    n = proj.shape[0]
    tq, tk = ATT_TQ, ATT_TK
    nq = seq // tq
    width = ATT_HEADS * HEAD_DIM
    kvw = ATT_KV_HEADS * HEAD_DIM
    topk = min(TOPK_MAX, seq // 4)
    group = ATT_HEADS // ATT_KV_HEADS
    kern = functools.partial(_attn_kernel, topk=topk, seq=seq)
    once = pl.Buffered(1)
    return pl.pallas_call(
        kern,
        grid=(batch, nq),
        in_specs=[
            pl.BlockSpec((tq, width), lambda b, q: (b * nq + q, cols["aq"] // width)),
            pl.BlockSpec((tq, width), lambda b, q: (b * nq + q, cols["iq"] // width)),
            pl.BlockSpec((seq, kvw), lambda b, q: (b, cols["ak"] // kvw), pipeline_mode=once),
            pl.BlockSpec((seq, kvw), lambda b, q: (b, cols["av"] // kvw), pipeline_mode=once),
            pl.BlockSpec((seq, LANES), lambda b, q: (b, 0), pipeline_mode=once),
            pl.BlockSpec((tq, LANES), lambda b, q: (b * nq + q, 0)),
        ],
        out_specs=pl.BlockSpec((tq, width), lambda b, q: (b * nq + q, 0)),
        out_shape=jax.ShapeDtypeStruct((n, width), BF16),
        scratch_shapes=[
            pltpu.VMEM((seq // tk, tq, tk), jnp.int32),
            pltpu.VMEM((ATT_HEADS * tq, HEAD_DIM), BF16),
            pltpu.VMEM((IDX_HEADS, tq, LANES), F32),
            pltpu.VMEM((ATT_KV_HEADS, group * tq, LANES), F32),
            pltpu.VMEM((ATT_KV_HEADS, group * tq, 2 * HEAD_DIM), F32),
        ],
        compiler_params=pltpu.CompilerParams(
            dimension_semantics=("arbitrary", "arbitrary"), vmem_limit_bytes=VMEM_LIMIT),
        name="dsa_attention",
    )(proj, proj, proj, proj, ik, sm)


GLA_T = 256


def _gla_kernel(q_ref, k_ref, v_ref, r_ref, lr_ref, wg_ref, bg_ref, gn_ref, o_ref, st_ref, *, dk, dv):
    c_len = GLA_CHUNK
    hi = lax.Precision.HIGHEST

    @pl.when(pl.program_id(1) == 0)
    def _():
        st_ref[...] = jnp.zeros(st_ref.shape, F32)

    ri = lax.broadcasted_iota(jnp.int32, (c_len, c_len), 0)
    ci = lax.broadcasted_iota(jnp.int32, (c_len, c_len), 1)
    lower = ri >= ci
    tril = jnp.where(lower, 1.0, 0.0).astype(F32)
    qscale = dk ** -0.5

    for c in range(GLA_T // c_len):
        rows = slice(c * c_len, (c + 1) * c_len)
        gl = jnp.dot(lr_ref[rows, :], wg_ref[...], preferred_element_type=F32, precision=hi) + bg_ref[...]
        log_a = (jnp.minimum(gl, 0.0) - jnp.log(1.0 + jnp.exp(-jnp.abs(gl)))) / GLA_GATE_TAU
        b = jnp.dot(tril, log_a, preferred_element_type=F32, precision=hi)
        b_last = b[c_len - 1:c_len, :]
        k = k_ref[rows, :].astype(F32)
        q_dec = (q_ref[rows, :].astype(F32) * qscale * jnp.exp(b)).astype(BF16)
        k_inv = (k * jnp.exp(-b)).astype(BF16)
        k_tail = k * jnp.exp(b_last - b)
        for h in range(GLA_HEADS):
            ks = slice(h * dk, (h + 1) * dk)
            vs = slice(h * dv, (h + 1) * dv)
            v = v_ref[rows, vs]
            a = jnp.where(lower, _nt_dot(q_dec[:, ks], k_inv[:, ks]), 0.0)
            state = st_ref[h]
            o = (jnp.dot(a.astype(BF16), v, preferred_element_type=F32)
                 + jnp.dot(q_dec[:, ks], state.astype(BF16), preferred_element_type=F32))
            dec = jnp.exp(b[:, ks].T[:, c_len - 1:c_len])
            upd = jnp.dot(k_tail[:, ks].T.astype(BF16), v, preferred_element_type=F32)
            for cc in range(dv // LANES):
                sl = slice(cc * LANES, (cc + 1) * LANES)
                st_ref[h, :, sl] = dec * state[:, sl] + upd[:, sl]
            r = r_ref[rows, vs].astype(F32)
            o_ref[rows, vs] = (_rms(o) * gn_ref[...] * (r * _sigmoid(r))).astype(BF16)


def _gla(proj, sm, wg, bg, gn, batch, seq, cols):
    n = proj.shape[0]
    dk = wg.shape[1] // GLA_HEADS
    dv = gn.shape[1]
    t = min(GLA_T, seq)
    assert t == GLA_T
    ns = seq // t
    kw, vw = GLA_HEADS * dk, GLA_HEADS * dv
    kern = functools.partial(_gla_kernel, dk=dk, dv=dv)
    return pl.pallas_call(
        kern,
        grid=(batch, ns),
        in_specs=[
            pl.BlockSpec((t, kw), lambda b, s: (b * ns + s, cols["gq"] // kw)),
            pl.BlockSpec((t, kw), lambda b, s: (b * ns + s, cols["gk"] // kw)),
            pl.BlockSpec((t, vw), lambda b, s: (b * ns + s, cols["gv"] // vw)),
            pl.BlockSpec((t, vw), lambda b, s: (b * ns + s, cols["gr"] // vw)),
            pl.BlockSpec((t, LANES), lambda b, s: (b * ns + s, 0)),
            pl.BlockSpec((LANES, kw), lambda b, s: (0, 0)),
            pl.BlockSpec((1, kw), lambda b, s: (0, 0)),
            pl.BlockSpec((1, dv), lambda b, s: (0, 0)),
        ],
        out_specs=pl.BlockSpec((t, vw), lambda b, s: (b * ns + s, 0)),
        out_shape=jax.ShapeDtypeStruct((n, vw), BF16),
        scratch_shapes=[pltpu.VMEM((GLA_HEADS, dk, dv), F32)],
        compiler_params=pltpu.CompilerParams(
            dimension_semantics=("arbitrary", "arbitrary"),
            vmem_limit_bytes=VMEM_LIMIT),
        name="gla",
    )(proj, proj, proj, proj, sm, wg, bg, gn)


def _merge_kernel(ya_ref, yg_ref, wa_ref, wg_ref, ma_ref, mg_ref, o_ref):
    a = jnp.dot(ya_ref[...], wa_ref[...], preferred_element_type=F32)
    g = jnp.dot(yg_ref[...], wg_ref[...], preferred_element_type=F32)
    o_ref[...] = (ma_ref[...].astype(F32) * a + mg_ref[...].astype(F32) * g).astype(BF16)


def _merge(y_att, y_gla, w_att, w_gla, proj, cols):
    n, d_att = y_att.shape
    d_gla = y_gla.shape[1]
    d = w_att.shape[1]
    tm, tn = 1024, PROJ_TN
    return pl.pallas_call(
        _merge_kernel,
        grid=(n // tm, d // tn),
        in_specs=[
            pl.BlockSpec((tm, d_att), lambda i, j: (i, 0)),
            pl.BlockSpec((tm, d_gla), lambda i, j: (i, 0)),
            pl.BlockSpec((d_att, tn), lambda i, j: (0, j)),
            pl.BlockSpec((d_gla, tn), lambda i, j: (0, j)),
            pl.BlockSpec((tm, tn), lambda i, j: (i, cols["m_att"] // tn + j)),
            pl.BlockSpec((tm, tn), lambda i, j: (i, cols["m_gla"] // tn + j)),
        ],
        out_specs=pl.BlockSpec((tm, tn), lambda i, j: (i, j)),
        out_shape=jax.ShapeDtypeStruct((n, d), BF16),
        compiler_params=pltpu.CompilerParams(
            dimension_semantics=("arbitrary", "arbitrary"), vmem_limit_bytes=VMEM_LIMIT),
        name="merge",
    )(y_att, y_gla, w_att, w_gla, proj, proj)


def _resproj_kernel(m_ref, w_ref, x_ref, gate_ref, o_ref):
    y = jnp.dot(m_ref[...], w_ref[...], preferred_element_type=F32)
    o_ref[...] = x_ref[...] + gate_ref[...] * y


def _resproj(merged, w, x2, gate, seq):
    n, d = x2.shape
    tm, tn = 1024, PROJ_TN
    per_b = seq // tm
    return pl.pallas_call(
        _resproj_kernel,
        grid=(n // tm, d // tn),
        in_specs=[
            pl.BlockSpec((tm, merged.shape[1]), lambda i, j: (i, 0)),
            pl.BlockSpec((merged.shape[1], tn), lambda i, j: (0, j)),
            pl.BlockSpec((tm, tn), lambda i, j: (i, j)),
            pl.BlockSpec((None, 1, tn), lambda i, j: (i // per_b, 0, j)),
        ],
        out_specs=pl.BlockSpec((tm, tn), lambda i, j: (i, j)),
        out_shape=jax.ShapeDtypeStruct((n, d), F32),
        compiler_params=pltpu.CompilerParams(
            dimension_semantics=("arbitrary", "arbitrary"), vmem_limit_bytes=VMEM_LIMIT),
        name="resproj",
    )(merged, w, x2, gate)


FFN_TM = 512
FFN_TF = 512


def _swiglu_step(hn, w1, w3, w2):
    a = jnp.dot(hn, w1, preferred_element_type=F32)
    b = jnp.dot(hn, w3, preferred_element_type=F32)
    return a * _sigmoid(a) * b, w2


def _ffn_kernel(x_ref, g_ref, sc_ref, sh_ref, w1_ref, w3_ref, w2_ref, gate_ref, o_ref,
                hn_ref, acc_ref):
    j = pl.program_id(1)

    @pl.when(j == 0)
    def _():
        hn_ref[...] = _modulate(x_ref[...], g_ref[...], sc_ref[...], sh_ref[...]).astype(BF16)
        acc_ref[...] = jnp.zeros(acc_ref.shape, F32)

    act, w2 = _swiglu_step(hn_ref[...], w1_ref[...], w3_ref[...], w2_ref[...])
    acc_ref[...] += jnp.dot(act.astype(BF16), w2, preferred_element_type=F32)

    @pl.when(j == pl.num_programs(1) - 1)
    def _():
        o_ref[...] = x_ref[...] + gate_ref[...] * acc_ref[...]


def _ffn(x2, g, scale, shift, w1, w3, w2, gate, seq):
    n, d = x2.shape
    dff = w1.shape[1]
    tm, tf = min(FFN_TM, seq), FFN_TF
    per_b = seq // tm
    vec = pl.BlockSpec((None, 1, d), lambda i, j: (i // per_b, 0, 0))
    return pl.pallas_call(
        _ffn_kernel,
        grid=(n // tm, dff // tf),
        in_specs=[
            pl.BlockSpec((tm, d), lambda i, j: (i, 0)),
            pl.BlockSpec((1, d), lambda i, j: (0, 0)),
            vec, vec,
            pl.BlockSpec((d, tf), lambda i, j: (0, j)),
            pl.BlockSpec((d, tf), lambda i, j: (0, j)),
            pl.BlockSpec((tf, d), lambda i, j: (j, 0)),
            vec,
        ],
        out_specs=pl.BlockSpec((tm, d), lambda i, j: (i, 0)),
        out_shape=jax.ShapeDtypeStruct((n, d), F32),
        scratch_shapes=[pltpu.VMEM((tm, d), BF16), pltpu.VMEM((tm, d), F32)],
        compiler_params=pltpu.CompilerParams(
            dimension_semantics=("arbitrary", "arbitrary"), vmem_limit_bytes=VMEM_LIMIT),
        name="ffn_swiglu",
    )(x2, g, scale, shift, w1, w3, w2, gate)


MOE_TM = 512
ROUTE_TM = 1024


def _router_kernel(x_ref, g_ref, sc_ref, sh_ref, wr_ref, hn_ref, rt_ref):
    lane = lax.broadcasted_iota(jnp.int32, (1, LANES), 1)
    lane_f = lane.astype(F32)
    h = _modulate(x_ref[...], g_ref[...], sc_ref[...], sh_ref[...])
    hn_ref[...] = h
    logits = jnp.dot(h, wr_ref[...], preferred_element_type=F32, precision=lax.Precision.HIGHEST)
    logits = jnp.where(lane < N_EXPERTS, logits, -jnp.inf)
    m1 = jnp.max(logits, axis=-1, keepdims=True)
    i1 = jnp.min(jnp.where(logits == m1, lane_f, float(LANES)), axis=-1, keepdims=True)
    rest = jnp.where(lane_f == i1, -jnp.inf, logits)
    m2 = jnp.max(rest, axis=-1, keepdims=True)
    i2 = jnp.min(jnp.where(rest == m2, lane_f, float(LANES)), axis=-1, keepdims=True)
    e2 = jnp.exp(m2 - m1)
    p1 = 1.0 / (1.0 + e2)
    p2 = e2 / (1.0 + e2)
    rt_ref[...] = (jnp.where(lane == 0, i1, 0.0) + jnp.where(lane == 1, i2, 0.0)
                   + jnp.where(lane == 2, p1, 0.0) + jnp.where(lane == 3, p2, 0.0))


def _router(x2, g, scale, shift, w_router, seq):
    n, d = x2.shape
    tm = min(ROUTE_TM, seq)
    per_b = seq // tm
    vec = pl.BlockSpec((None, 1, d), lambda i: (i // per_b, 0, 0))
    return pl.pallas_call(
        _router_kernel,
        grid=(n // tm,),
        in_specs=[
            pl.BlockSpec((tm, d), lambda i: (i, 0)),
            pl.BlockSpec((1, d), lambda i: (0, 0)),
            vec, vec,
            pl.BlockSpec((d, LANES), lambda i: (0, 0)),
        ],
        out_specs=[pl.BlockSpec((tm, d), lambda i: (i, 0)),
                   pl.BlockSpec((tm, LANES), lambda i: (i, 0))],
        out_shape=[jax.ShapeDtypeStruct((n, d), F32), jax.ShapeDtypeStruct((n, LANES), F32)],
        compiler_params=pltpu.CompilerParams(
            dimension_semantics=("arbitrary",), vmem_limit_bytes=VMEM_LIMIT),
        name="moe_router",
    )(x2, g, scale, shift, w_router)


def _route_tables(rt, n, n_exp, tm):
    rows = 2 * n + n_exp * tm
    n_tiles = rows // tm
    ef = rt[:, :2].astype(jnp.int32).T.reshape(-1)
    onehot = (ef[:, None] == jnp.arange(n_exp, dtype=jnp.int32)[None, :]).astype(jnp.int32)
    csum = jnp.cumsum(onehot, axis=0)
    rank = jnp.take_along_axis(csum, ef[:, None], axis=1)[:, 0] - 1
    cnt = csum[-1]
    padded = ((cnt + tm - 1) // tm) * tm
    ends = jnp.cumsum(padded)
    dest = (ends - padded)[ef] + rank
    total = ends[-1]
    tile_start = jnp.arange(n_tiles, dtype=jnp.int32) * tm
    tile_valid = (tile_start < total).astype(jnp.int32)
    tile_expert = jnp.sum((tile_start[:, None] >= ends[None, :]).astype(jnp.int32), axis=1)
    last_tile = jnp.maximum(total // tm - 1, 0)
    tile_expert = jnp.where(tile_valid == 1, tile_expert, tile_expert[last_tile])
    tile_rows = jnp.where(tile_valid == 1, jnp.arange(n_tiles, dtype=jnp.int32), last_tile)
    spare_lo = jnp.concatenate([ends - padded + cnt, total[None]]).astype(jnp.int32)
    spare_hi = jnp.concatenate([ends, jnp.full((1,), rows, ends.dtype)]).astype(jnp.int32)
    return dict(tile_expert=tile_expert, tile_valid=tile_valid, tile_rows=tile_rows, dest=dest,
                spare_lo=spare_lo, spare_hi=spare_hi, rows=rows)


def _row_copies(copy, count):
    def body(r, carry):
        copy(r).start()
        return carry
    lax.fori_loop(0, count, body, 0, unroll=8)


def _dispatch_kernel(dest_ref, lo_ref, hi_ref, hn_ref, xs_hbm, zero_ref, sem, zsem, *, n, n_spare):
    i = pl.program_id(0)
    tm = hn_ref.shape[0]

    @pl.when(i == 0)
    def _():
        zero_ref[...] = jnp.zeros(zero_ref.shape, F32)
        for e in range(n_spare):
            def zero_row(r):
                return pltpu.make_async_copy(zero_ref.at[pl.ds(0, 1)], xs_hbm.at[pl.ds(r, 1)], zsem)

            def start(r, carry):
                zero_row(r).start()
                return carry

            def drain(r, carry):
                zero_row(r).wait()
                return carry
            lax.fori_loop(lo_ref[e], hi_ref[e], start, 0)
            lax.fori_loop(lo_ref[e], hi_ref[e], drain, 0)

    for k in range(2):
        _row_copies(lambda r, k=k: pltpu.make_async_copy(
            hn_ref.at[pl.ds(r, 1)], xs_hbm.at[pl.ds(dest_ref[k * n + i * tm + r], 1)], sem), tm)
    for k in range(2):
        pltpu.make_async_copy(hn_ref, xs_hbm.at[pl.ds(0, tm)], sem).wait()


def _dispatch(hn, tables):
    n, d = hn.shape
    tm = MOE_TM
    kern = functools.partial(_dispatch_kernel, n=n, n_spare=tables["spare_lo"].shape[0])
    grid_spec = pltpu.PrefetchScalarGridSpec(
        num_scalar_prefetch=3,
        grid=(n // tm,),
        in_specs=[pl.BlockSpec((tm, d), lambda i, dest, lo, hi: (i, 0))],
        out_specs=pl.BlockSpec(memory_space=pl.ANY),
        scratch_shapes=[pltpu.VMEM((8, d), F32), pltpu.SemaphoreType.DMA(()),
                        pltpu.SemaphoreType.DMA(())],
    )
    return pl.pallas_call(
        kern,
        grid_spec=grid_spec,
        out_shape=jax.ShapeDtypeStruct((tables["rows"], d), F32),
        compiler_params=pltpu.CompilerParams(
            dimension_semantics=("arbitrary",), vmem_limit_bytes=VMEM_LIMIT),
        name="moe_dispatch",
    )(tables["dest"], tables["spare_lo"], tables["spare_hi"], hn)


def _moe_grouped_kernel(te_ref, tv_ref, tr_ref, x_ref, w1_ref, w3_ref, w2_ref, y_ref, xb_ref):
    t = pl.program_id(0)
    j = pl.program_id(1)
    valid = tv_ref[t] == 1

    @pl.when(j == 0)
    def _():
        xb_ref[...] = x_ref[...].astype(BF16)
        y_ref[...] = jnp.zeros(y_ref.shape, F32)

    @pl.when(valid)
    def _():
        act, w2 = _swiglu_step(xb_ref[...], w1_ref[...], w3_ref[...], w2_ref[...])
        y_ref[...] += jnp.dot(act.astype(BF16), w2, preferred_element_type=F32)


def _moe_grouped(xs, tables, w1, w3, w2):
    rows, d = xs.shape
    dff = w1.shape[2]
    tm, tf = MOE_TM, FFN_TF
    nj = dff // tf

    def jj(t, j, tv):
        return j * tv[t] + (nj - 1) * (1 - tv[t])

    grid_spec = pltpu.PrefetchScalarGridSpec(
        num_scalar_prefetch=3,
        grid=(rows // tm, nj),
        in_specs=[
            pl.BlockSpec((tm, d), lambda t, j, te, tv, tr: (tr[t], 0)),
            pl.BlockSpec((None, d, tf), lambda t, j, te, tv, tr: (te[t], 0, jj(t, j, tv))),
            pl.BlockSpec((None, d, tf), lambda t, j, te, tv, tr: (te[t], 0, jj(t, j, tv))),
            pl.BlockSpec((None, tf, d), lambda t, j, te, tv, tr: (te[t], jj(t, j, tv), 0)),
        ],
        out_specs=pl.BlockSpec((tm, d), lambda t, j, te, tv, tr: (t, 0)),
        scratch_shapes=[pltpu.VMEM((tm, d), BF16)],
    )
    return pl.pallas_call(
        _moe_grouped_kernel,
        grid_spec=grid_spec,
        out_shape=jax.ShapeDtypeStruct((rows, d), F32),
        compiler_params=pltpu.CompilerParams(
            dimension_semantics=("arbitrary", "arbitrary"), vmem_limit_bytes=VMEM_LIMIT),
        name="moe_grouped",
    )(tables["tile_expert"], tables["tile_valid"], tables["tile_rows"], xs, w1, w3, w2)


def _moe_combine_kernel(dest_ref, x_ref, rt_ref, gate_ref, y_hbm, o_ref, y0_ref, y1_ref, sem, *, n):
    i = pl.program_id(0)
    tm = x_ref.shape[0]
    for k, yk_ref in enumerate((y0_ref, y1_ref)):
        _row_copies(lambda r, k=k, yk_ref=yk_ref: pltpu.make_async_copy(
            y_hbm.at[pl.ds(dest_ref[k * n + i * tm + r], 1)], yk_ref.at[pl.ds(r, 1)], sem), tm)
    for yk_ref in (y0_ref, y1_ref):
        pltpu.make_async_copy(y_hbm.at[pl.ds(0, tm)], yk_ref, sem).wait()
    p1 = rt_ref[:, 2:3]
    p2 = rt_ref[:, 3:4]
    o_ref[...] = x_ref[...] + gate_ref[...] * (p1 * y0_ref[...] + p2 * y1_ref[...])


def _moe_combine(x2, y, rt, gate, dest, seq):
    n, d = x2.shape
    tm = min(FFN_TM, seq)
    per_b = seq // tm
    kern = functools.partial(_moe_combine_kernel, n=n)
    grid_spec = pltpu.PrefetchScalarGridSpec(
        num_scalar_prefetch=1,
        grid=(n // tm,),
        in_specs=[
            pl.BlockSpec((tm, d), lambda i, dest: (i, 0)),
            pl.BlockSpec((tm, LANES), lambda i, dest: (i, 0)),
            pl.BlockSpec((None, 1, d), lambda i, dest: (i // per_b, 0, 0)),
            pl.BlockSpec(memory_space=pl.ANY),
        ],
        out_specs=pl.BlockSpec((tm, d), lambda i, dest: (i, 0)),
        scratch_shapes=[pltpu.VMEM((tm, d), F32), pltpu.VMEM((tm, d), F32),
                        pltpu.SemaphoreType.DMA(())],
    )
    return pl.pallas_call(
        kern,
        grid_spec=grid_spec,
        out_shape=jax.ShapeDtypeStruct((n, d), F32),
        compiler_params=pltpu.CompilerParams(
            dimension_semantics=("arbitrary",), vmem_limit_bytes=VMEM_LIMIT),
        name="moe_combine",
    )(dest, x2, rt, gate, y)


def _moe(x2, g, scale, shift, w_router, w1, w3, w2, gate, seq):
    n = x2.shape[0]
    hn, rt = _router(x2, g, scale, shift, w_router, seq)
    tables = _route_tables(rt, n, w1.shape[0], MOE_TM)
    xs = _dispatch(hn, tables)
    y = _moe_grouped(xs, tables, w1, w3, w2)
    return _moe_combine(x2, y, rt, gate, tables["dest"], seq)


def _prep_in_proj(w_in, q_norm, k_norm, idx_k_norm, d):
    sizes = dict(aq=ATT_HEADS * HEAD_DIM, ak=ATT_KV_HEADS * HEAD_DIM, av=ATT_KV_HEADS * HEAD_DIM,
                 iq=IDX_HEADS * HEAD_DIM, ik=HEAD_DIM, iw=IDX_HEADS,
                 gq=d // 2, gk=d // 2, gv=d, glr=GLA_GATE_RANK, gr=d, m_att=d, m_gla=d)
    src, acc = {}, 0
    for name in ("aq", "ak", "av", "iq", "ik", "iw", "gq", "gk", "gv", "glr", "gr", "m_att", "m_gla"):
        src[name] = (acc, sizes[name])
        acc += sizes[name]
    order = ("aq", "iq", "gv", "gr", "m_att", "m_gla", "ak", "av", "gq", "gk")
    cols, parts, off = {}, [], 0
    for name in order:
        s, width = src[name]
        cols[name] = off
        parts.append(w_in[:, s:s + width])
        off += width
    w_main = jnp.concatenate(parts, axis=1).astype(BF16)

    def seg(name):
        s, width = src[name]
        return w_in[:, s:s + width]

    pad = jnp.zeros((d, LANES - IDX_HEADS - GLA_GATE_RANK), w_in.dtype)
    w_small = jnp.concatenate([seg("ik"), seg("iw"), seg("glr"), pad], axis=1).astype(BF16)

    att_scale = HEAD_DIM ** -0.5 * LOG2_E
    idx_scale = (HEAD_DIM ** -0.5) * (IDX_HEADS ** -0.5)
    cg = jnp.ones((off,), F32)
    cg = cg.at[cols["aq"]:cols["aq"] + sizes["aq"]].set(jnp.tile(q_norm * att_scale, ATT_HEADS))
    cg = cg.at[cols["ak"]:cols["ak"] + sizes["ak"]].set(jnp.tile(k_norm, ATT_KV_HEADS))
    cg2 = jnp.concatenate([idx_k_norm, jnp.full((IDX_HEADS,), idx_scale, F32),
                           jnp.ones((LANES - IDX_HEADS,), F32)])
    norm_tiles = tuple(range(cols["aq"] // PROJ_TN, (cols["aq"] + sizes["aq"]) // PROJ_TN)) + \
        tuple(range(cols["ak"] // PROJ_TN, (cols["ak"] + sizes["ak"]) // PROJ_TN))
    sig_tiles = (cols["m_att"] // PROJ_TN, (cols["m_gla"] + sizes["m_gla"]) // PROJ_TN)
    return w_main, w_small, cg.reshape(1, -1), cg2.reshape(1, -1), cols, norm_tiles, sig_tiles


def kernel(x, c, ada_w, ada_b, norm_mix, norm_ffn, w_in, q_norm, k_norm, idx_k_norm, w_gla_gate,
           b_gla_gate, gla_out_norm, w_out_attn, w_out_gla, w_out, ffn_w1, ffn_w3, ffn_w2,
           moe_router, moe_w1, moe_w3, moe_w2):
    batch, seq, d = x.shape
    depth = ada_w.shape[0]
    n = batch * seq
    x2 = x.reshape(n, d)

    c8 = jnp.zeros((8, d), F32).at[:batch].set(c)
    mod_all = _ada(c8, ada_w, ada_b)

    for layer in range(depth):
        mod = mod_all[layer, :batch].reshape(batch, 6, 1, d)
        shift_m, scale_m, gate_m = mod[:, 0], mod[:, 1], mod[:, 2]
        shift_f, scale_f, gate_f = mod[:, 3], mod[:, 4], mod[:, 5]

        w_main, w_small, cg, cg2, cols, norm_tiles, sig_tiles = _prep_in_proj(
            w_in[layer], q_norm[layer], k_norm[layer], idx_k_norm[layer], d)
        proj, ik, sm = _modproj(x2, norm_mix[layer].reshape(1, d), scale_m, shift_m,
                                w_main, cg, w_small, cg2, seq, norm_tiles, sig_tiles)
        y_att = _attention(proj, ik, sm, batch, seq, cols)
        wg = jnp.zeros((LANES, w_gla_gate.shape[2]), F32).at[
            GLA_GATE_RANK:2 * GLA_GATE_RANK].set(w_gla_gate[layer])
        y_gla = _gla(proj, sm, wg, b_gla_gate[layer].reshape(1, -1),
                     gla_out_norm[layer].reshape(1, -1), batch, seq, cols)
        merged = _merge(y_att, y_gla, w_out_attn[layer].astype(BF16),
                        w_out_gla[layer].astype(BF16), proj, cols)
        x2 = _resproj(merged, w_out[layer].astype(BF16), x2, gate_m, seq)

        g_f = norm_ffn[layer].reshape(1, d)
        i = layer // 2
        if layer % 2 == 0:
            x2 = _ffn(x2, g_f, scale_f, shift_f, ffn_w1[i].astype(BF16), ffn_w3[i].astype(BF16),
                      ffn_w2[i].astype(BF16), gate_f, seq)
        else:
            w_r = jnp.zeros((d, LANES), F32).at[:, :N_EXPERTS].set(moe_router[i])
            x2 = _moe(x2, g_f, scale_f, shift_f, w_r, moe_w1[i].astype(BF16),
                      moe_w3[i].astype(BF16), moe_w2[i].astype(BF16), gate_f, seq)
    return x2.reshape(batch, seq, d)
```

```python
import functools

import jax
import jax.numpy as jnp
from jax import lax
from jax.experimental import pallas as pl
from jax.experimental.pallas import tpu as pltpu

F32 = jnp.float32
BF16 = jnp.bfloat16

ATT_HEADS = 16
ATT_KV_HEADS = 4
HEAD_DIM = 128
IDX_HEADS = 16
TOPK_MAX = 256
GLA_HEADS = 4
GLA_GATE_RANK = 16
GLA_GATE_TAU = 16.0
GLA_CHUNK = 64
N_EXPERTS = 8
EPS = 1e-6
LANES = 128
NEG_BIG = -1e30
INT_MIN = -(2 ** 31)
LOG2_E = 1.4426950408889634

PROJ_TN = 512
VMEM_LIMIT = 56 * 1024 * 1024


def _nt_dot(a, b):
    return lax.dot_general(a, b, (((1,), (1,)), ((), ())), preferred_element_type=F32)


def _rms(a):
    return a * lax.rsqrt(jnp.mean(a * a, axis=-1, keepdims=True) + EPS)


def _sigmoid(a):
    return 1.0 / (1.0 + jnp.exp(-a))


def _modulate(x, g, scale, shift):
    return _rms(x) * g * (1.0 + scale) + shift


def _ada_kernel(c_ref, w_ref, b_ref, o_ref):
    c = c_ref[...]
    cond = c * _sigmoid(c)
    o_ref[...] = jnp.dot(cond, w_ref[...], preferred_element_type=F32,
                         precision=lax.Precision.HIGHEST) + b_ref[...]


def _ada(c8, ada_w, ada_b):
    depth, d, n = ada_w.shape
    tn = 1024
    return pl.pallas_call(
        _ada_kernel,
        grid=(depth, n // tn),
        in_specs=[
            pl.BlockSpec((8, d), lambda l, j: (0, 0)),
            pl.BlockSpec((None, d, tn), lambda l, j: (l, 0, j)),
            pl.BlockSpec((None, 1, tn), lambda l, j: (l, 0, j)),
        ],
        out_specs=pl.BlockSpec((None, 8, tn), lambda l, j: (l, 0, j)),
        out_shape=jax.ShapeDtypeStruct((depth, 8, n), F32),
        name="ada_mod",
    )(c8, ada_w, ada_b.reshape(depth, 1, n))


def _modproj_kernel(x_ref, g_ref, sc_ref, sh_ref, w_ref, cg_ref, w2_ref, cg2_ref,
                    o_ref, ik_ref, sm_ref, hn_ref, *, norm_tiles, sig_tiles):
    j = pl.program_id(1)

    @pl.when(j == 0)
    def _():
        h = _modulate(x_ref[...], g_ref[...], sc_ref[...], sh_ref[...]).astype(BF16)
        hn_ref[...] = h
        small = jnp.dot(h, w2_ref[...], preferred_element_type=F32)
        ik_ref[...] = (_rms(small[:, :LANES]) * cg2_ref[:, :LANES]).astype(BF16)
        sm_ref[...] = small[:, LANES:] * cg2_ref[:, LANES:]

    acc = jnp.dot(hn_ref[...], w_ref[...], preferred_element_type=F32)
    is_norm = functools.reduce(jnp.logical_or, [j == t for t in norm_tiles])
    is_sig = jnp.logical_and(j >= sig_tiles[0], j < sig_tiles[1])

    @pl.when(is_norm)
    def _():
        for c in range(PROJ_TN // LANES):
            sl = slice(c * LANES, (c + 1) * LANES)
            o_ref[:, sl] = (_rms(acc[:, sl]) * cg_ref[:, sl]).astype(BF16)

    @pl.when(is_sig)
    def _():
        o_ref[...] = _sigmoid(acc).astype(BF16)

    @pl.when(jnp.logical_not(jnp.logical_or(is_norm, is_sig)))
    def _():
        o_ref[...] = acc.astype(BF16)


def _modproj(x2, g, scale, shift, w, cg, w2, cg2, seq, norm_tiles, sig_tiles):
    n, d = x2.shape
    ncols = w.shape[1]
    tm = min(1024, seq)
    per_b = seq // tm
    kern = functools.partial(_modproj_kernel, norm_tiles=norm_tiles, sig_tiles=sig_tiles)
    return pl.pallas_call(
        kern,
        grid=(n // tm, ncols // PROJ_TN),
        in_specs=[
            pl.BlockSpec((tm, d), lambda i, j: (i, 0)),
            pl.BlockSpec((1, d), lambda i, j: (0, 0)),
            pl.BlockSpec((None, 1, d), lambda i, j: (i // per_b, 0, 0)),
            pl.BlockSpec((None, 1, d), lambda i, j: (i // per_b, 0, 0)),
            pl.BlockSpec((d, PROJ_TN), lambda i, j: (0, j)),
            pl.BlockSpec((1, PROJ_TN), lambda i, j: (0, j)),
            pl.BlockSpec((d, 2 * LANES), lambda i, j: (0, 0)),
            pl.BlockSpec((1, 2 * LANES), lambda i, j: (0, 0)),
        ],
        out_specs=[
            pl.BlockSpec((tm, PROJ_TN), lambda i, j: (i, j)),
            pl.BlockSpec((tm, LANES), lambda i, j: (i, 0)),
            pl.BlockSpec((tm, LANES), lambda i, j: (i, 0)),
        ],
        out_shape=[
            jax.ShapeDtypeStruct((n, ncols), BF16),
            jax.ShapeDtypeStruct((n, LANES), BF16),
            jax.ShapeDtypeStruct((n, LANES), F32),
        ],
        scratch_shapes=[pltpu.VMEM((tm, d), BF16)],
        compiler_params=pltpu.CompilerParams(
            dimension_semantics=("arbitrary", "arbitrary"), vmem_limit_bytes=VMEM_LIMIT),
        name="modproj",
    )(x2, g, scale, shift, w, cg, w2, cg2)


ATT_TQ = 128
ATT_TK = 256
BISECT_UNCHECKED_BITS = 19
ATT_SPLIT = 4


def _sortable(a):
    a = jnp.where(a == 0.0, 0.0, a)
    bits = pltpu.bitcast(a, jnp.int32)
    return bits ^ ((bits >> 31) & 0x7FFFFFFF)


def _attn_kernel(aq_ref, iq_ref, ak_ref, av_ref, ik_ref, iw_ref, o_ref,
                 sc_ref, sct_ref, qs_ref, wb_ref, m_ref, acc_ref, *, topk, seq):
    tq, tk = ATT_TQ, ATT_TK
    group = ATT_HEADS // ATT_KV_HEADS
    qi = pl.program_id(1)
    n_kt = (qi * tq + tq + tk - 1) // tk
    row = qi * tq + lax.broadcasted_iota(jnp.int32, (tq, 1), 0)
    lane_col = lax.broadcasted_iota(jnp.int32, (1, tk), 1)
    for h in range(IDX_HEADS):
        wb_ref[h] = jnp.broadcast_to(iw_ref[:, h:h + 1], (tq, LANES))
    for h in range(ATT_HEADS):
        qs_ref[h * tq:(h + 1) * tq, :] = aq_ref[:, h * HEAD_DIM:(h + 1) * HEAD_DIM]

    n_pair = (n_kt + 1) // 2

    def score_pair(i, carry):
        kblk = ik_ref[pl.ds(pl.multiple_of(i * 2 * tk, 2 * tk), 2 * tk), :]
        acc = jnp.zeros((tq, 2 * tk), F32)
        for h in range(IDX_HEADS):
            z = _nt_dot(iq_ref[:, h * HEAD_DIM:(h + 1) * HEAD_DIM], kblk)
            wb = wb_ref[h]
            acc = acc + jnp.concatenate([wb] * (2 * tk // LANES), axis=1) * jnp.maximum(z, 0.0)
        keys = []
        for u in range(2):
            kt = 2 * i + u
            causal = (kt * tk + lane_col) <= row
            keys.append(jnp.where(causal, _sortable(acc[:, u * tk:(u + 1) * tk]), INT_MIN))
            sc_ref[kt] = keys[u]
        sct_ref[i] = jnp.concatenate(keys, axis=1).T
        return carry

    lax.fori_loop(0, n_pair, score_pair, 0)

    def count_where(pred):
        def body(i, c):
            for u in range(2):
                kt = 2 * i + u
                hit = jnp.where(pred(sc_ref[kt], kt), 1.0, 0.0)
                for cc in range(tk // LANES):
                    c = c + hit[:, cc * LANES:(cc + 1) * LANES]
            return c
        c = lax.fori_loop(0, n_pair, body, jnp.zeros((tq, LANES), F32))
        return jnp.sum(c, axis=-1, keepdims=True)

    kf = float(topk)

    def count_ge_t(t):
        def body(i, c):
            hit = jnp.where(sct_ref[i] >= t, 1.0, 0.0)
            return c + jnp.sum(hit.reshape(2 * tk // 64, 8, 8, tq), axis=0)
        c = lax.fori_loop(0, n_pair, body, jnp.zeros((8, 8, tq), F32))
        return jnp.sum(jnp.sum(c, axis=0), axis=0, keepdims=True)

    c0 = count_ge_t(jnp.zeros((1, tq), jnp.int32))
    t0 = jnp.where(c0 >= kf, 0, INT_MIN).astype(jnp.int32)
    n_all = jnp.full((1, tq), float(seq), F32)

    def bit_step(i, state):
        t, ct, open_rows = state
        cand = t | jnp.left_shift(jnp.int32(1), 30 - i)
        c = count_ge_t(cand)
        take = jnp.logical_and(c >= kf, open_rows > 0.0)
        t = jnp.where(take, cand, t)
        ct = jnp.where(take, c, ct)
        open_rows = jnp.where(c == kf, 0.0, open_rows)
        return t, ct, open_rows

    def any_open(open_rows):
        return (jnp.max(open_rows) > 0.0).astype(jnp.int32)

    def tail_cond(state):
        i, _, go = state
        return jnp.logical_and(i < 31, go == 1)

    def tail_step(state):
        i, inner, _ = state
        inner = bit_step(i + 1, bit_step(i, inner))
        return i + 2, inner, any_open(inner[2])

    state = (t0, jnp.where(c0 >= kf, c0, n_all), jnp.where(c0 == kf, 0.0, 1.0))
    state = lax.fori_loop(0, BISECT_UNCHECKED_BITS, bit_step, state)
    _, (thr_t, cnt_t, open_t), _ = lax.while_loop(
        tail_cond, tail_step, (jnp.int32(BISECT_UNCHECKED_BITS), state, any_open(state[2])))

    eye = lax.broadcasted_iota(jnp.int32, (tq, tq), 0) == lax.broadcasted_iota(jnp.int32, (tq, tq), 1)

    def to_col(v):
        return jnp.sum(jnp.where(eye, v, 0.0), axis=1, keepdims=True)

    thr = ((to_col((thr_t >> 16).astype(F32)).astype(jnp.int32) << 16)
           | to_col((thr_t & 0xFFFF).astype(F32)).astype(jnp.int32))
    cnt_thr = to_col(cnt_t)
    open_rows = to_col(open_t)

    excess = jnp.logical_and(jnp.logical_and(open_rows > 0.0, cnt_thr > kf), thr > INT_MIN)

    @pl.when(jnp.max(jnp.where(excess, 1.0, 0.0)) > 0.0)
    def _():
        need = kf - count_where(lambda key, kt: key > thr)

        def ties_below(j):
            return count_where(
                lambda key, kt: jnp.logical_and(key == thr, (kt * tk + lane_col) < j))

        nbits = max(seq - 1, 1).bit_length()

        def idx_step(i, j0):
            cand = j0 | jnp.left_shift(jnp.int32(1), nbits - 1 - i)
            return jnp.where(ties_below(cand) < need, cand, j0)

        j0 = lax.fori_loop(0, nbits, idx_step, jnp.zeros((tq, 1), jnp.int32))

        def demote(kt, carry):
            key = sc_ref[kt]
            late_tie = jnp.logical_and(key == thr, (kt * tk + lane_col) > j0)
            sc_ref[kt] = jnp.where(jnp.logical_and(excess, late_tie), INT_MIN, key)
            return carry

        lax.fori_loop(0, n_kt, demote, 0)

    thr = jnp.maximum(thr, INT_MIN + 1)

    m_ref[...] = jnp.full(m_ref.shape, NEG_BIG, F32)
    acc_ref[...] = jnp.zeros(acc_ref.shape, F32)
    tk2 = 2 * tk
    ones = jnp.ones((tk2, HEAD_DIM), BF16)

    def kv_pair(i, carry):
        r0 = pl.multiple_of(i * tk2, tk2)
        keep = jnp.concatenate([sc_ref[2 * i] >= thr, sc_ref[2 * i + 1] >= thr], axis=1)[None]
        for g in range(ATT_KV_HEADS):
            kblk = ak_ref[pl.ds(r0, tk2), g * HEAD_DIM:(g + 1) * HEAD_DIM]
            vblk = av_ref[pl.ds(r0, tk2), g * HEAD_DIM:(g + 1) * HEAD_DIM]
            v1 = jnp.concatenate([vblk, ones], axis=1)
            for part in range(ATT_SPLIT):
                nh = group // ATT_SPLIT
                rows = slice(part * nh * tq, (part + 1) * nh * tq)
                q4 = qs_ref[(g * group + part * nh) * tq:(g * group + (part + 1) * nh) * tq, :]
                s = _nt_dot(q4, kblk).reshape(nh, tq, tk2)
                s = jnp.where(keep, s, NEG_BIG).reshape(nh * tq, tk2)
                m_old = m_ref[g, rows, :]
                m_new = jnp.maximum(m_old, jnp.max(s, axis=-1, keepdims=True))
                alpha = jnp.exp2(m_old - m_new)
                p = jnp.exp2(s - jnp.concatenate([m_new] * (tk2 // LANES), axis=1))
                pv = jnp.dot(p.astype(BF16), v1, preferred_element_type=F32)
                acc_ref[g, rows, :] = jnp.concatenate([alpha, alpha], axis=1) * acc_ref[g, rows, :] + pv
                m_ref[g, rows, :] = m_new
        return carry

    lax.fori_loop(0, n_pair, kv_pair, 0)

    for g in range(ATT_KV_HEADS):
        og = acc_ref[g]
        og = og[:, :HEAD_DIM] / og[:, HEAD_DIM:]
        for r in range(group):
            h = g * group + r
            o_ref[:, h * HEAD_DIM:(h + 1) * HEAD_DIM] = og[r * tq:(r + 1) * tq, :].astype(BF16)


def _attention(proj, ik, sm, batch, seq, cols):
    n = proj.shape[0]
    tq, tk = ATT_TQ, ATT_TK
    nq = seq // tq
    width = ATT_HEADS * HEAD_DIM
    kvw = ATT_KV_HEADS * HEAD_DIM
    topk = min(TOPK_MAX, seq // 4)
    group = ATT_HEADS // ATT_KV_HEADS
    kern = functools.partial(_attn_kernel, topk=topk, seq=seq)
    once = pl.Buffered(1)
    return pl.pallas_call(
        kern,
        grid=(batch, nq),
        in_specs=[
            pl.BlockSpec((tq, width), lambda b, q: (b * nq + q, cols["aq"] // width)),
            pl.BlockSpec((tq, width), lambda b, q: (b * nq + q, cols["iq"] // width)),
            pl.BlockSpec((seq, kvw), lambda b, q: (b, cols["ak"] // kvw), pipeline_mode=once),
            pl.BlockSpec((seq, kvw), lambda b, q: (b, cols["av"] // kvw), pipeline_mode=once),
            pl.BlockSpec((seq, LANES), lambda b, q: (b, 0), pipeline_mode=once),
            pl.BlockSpec((tq, LANES), lambda b, q: (b * nq + q, 0)),
        ],
        out_specs=pl.BlockSpec((tq, width), lambda b, q: (b * nq + q, 0)),
        out_shape=jax.ShapeDtypeStruct((n, width), BF16),
        scratch_shapes=[
            pltpu.VMEM((seq // tk, tq, tk), jnp.int32),
            pltpu.VMEM((seq // (2 * tk), 2 * tk, tq), jnp.int32),
            pltpu.VMEM((ATT_HEADS * tq, HEAD_DIM), BF16),
            pltpu.VMEM((IDX_HEADS, tq, LANES), F32),
            pltpu.VMEM((ATT_KV_HEADS, group * tq, LANES), F32),
            pltpu.VMEM((ATT_KV_HEADS, group * tq, 2 * HEAD_DIM), F32),
        ],
        compiler_params=pltpu.CompilerParams(
            dimension_semantics=("arbitrary", "arbitrary"), vmem_limit_bytes=VMEM_LIMIT),
        name="dsa_attention",
    )(proj, proj, proj, proj, ik, sm)


GLA_T = 256


def _gla_kernel(q_ref, k_ref, v_ref, r_ref, lr_ref, wg_ref, bg_ref, gn_ref, o_ref, st_ref, *, dk, dv):
    c_len = GLA_CHUNK
    hi = lax.Precision.HIGHEST

    @pl.when(pl.program_id(1) == 0)
    def _():
        st_ref[...] = jnp.zeros(st_ref.shape, F32)

    ri = lax.broadcasted_iota(jnp.int32, (c_len, c_len), 0)
    ci = lax.broadcasted_iota(jnp.int32, (c_len, c_len), 1)
    lower = ri >= ci
    tril = jnp.where(lower, 1.0, 0.0).astype(F32)
    qscale = dk ** -0.5

    for c in range(GLA_T // c_len):
        rows = slice(c * c_len, (c + 1) * c_len)
        gl = jnp.dot(lr_ref[rows, :], wg_ref[...], preferred_element_type=F32, precision=hi) + bg_ref[...]
        log_a = (jnp.minimum(gl, 0.0) - jnp.log(1.0 + jnp.exp(-jnp.abs(gl)))) / GLA_GATE_TAU
        b = jnp.dot(tril, log_a, preferred_element_type=F32, precision=hi)
        b_last = b[c_len - 1:c_len, :]
        k = k_ref[rows, :].astype(F32)
        q_dec = (q_ref[rows, :].astype(F32) * qscale * jnp.exp(b)).astype(BF16)
        k_inv = (k * jnp.exp(-b)).astype(BF16)
        k_tail = k * jnp.exp(b_last - b)
        for h in range(GLA_HEADS):
            ks = slice(h * dk, (h + 1) * dk)
            vs = slice(h * dv, (h + 1) * dv)
            v = v_ref[rows, vs]
            a = jnp.where(lower, _nt_dot(q_dec[:, ks], k_inv[:, ks]), 0.0)
            state = st_ref[h]
            o = (jnp.dot(a.astype(BF16), v, preferred_element_type=F32)
                 + jnp.dot(q_dec[:, ks], state.astype(BF16), preferred_element_type=F32))
            dec = jnp.exp(b[:, ks].T[:, c_len - 1:c_len])
            upd = jnp.dot(k_tail[:, ks].T.astype(BF16), v, preferred_element_type=F32)
            for cc in range(dv // LANES):
                sl = slice(cc * LANES, (cc + 1) * LANES)
                st_ref[h, :, sl] = dec * state[:, sl] + upd[:, sl]
            r = r_ref[rows, vs].astype(F32)
            o_ref[rows, vs] = (_rms(o) * gn_ref[...] * (r * _sigmoid(r))).astype(BF16)


def _gla(proj, sm, wg, bg, gn, batch, seq, cols):
    n = proj.shape[0]
    dk = wg.shape[1] // GLA_HEADS
    dv = gn.shape[1]
    t = min(GLA_T, seq)
    assert t == GLA_T
    ns = seq // t
    kw, vw = GLA_HEADS * dk, GLA_HEADS * dv
    kern = functools.partial(_gla_kernel, dk=dk, dv=dv)
    return pl.pallas_call(
        kern,
        grid=(batch, ns),
        in_specs=[
            pl.BlockSpec((t, kw), lambda b, s: (b * ns + s, cols["gq"] // kw)),
            pl.BlockSpec((t, kw), lambda b, s: (b * ns + s, cols["gk"] // kw)),
            pl.BlockSpec((t, vw), lambda b, s: (b * ns + s, cols["gv"] // vw)),
            pl.BlockSpec((t, vw), lambda b, s: (b * ns + s, cols["gr"] // vw)),
            pl.BlockSpec((t, LANES), lambda b, s: (b * ns + s, 0)),
            pl.BlockSpec((LANES, kw), lambda b, s: (0, 0)),
            pl.BlockSpec((1, kw), lambda b, s: (0, 0)),
            pl.BlockSpec((1, dv), lambda b, s: (0, 0)),
        ],
        out_specs=pl.BlockSpec((t, vw), lambda b, s: (b * ns + s, 0)),
        out_shape=jax.ShapeDtypeStruct((n, vw), BF16),
        scratch_shapes=[pltpu.VMEM((GLA_HEADS, dk, dv), F32)],
        compiler_params=pltpu.CompilerParams(
            dimension_semantics=("arbitrary", "arbitrary"),
            vmem_limit_bytes=VMEM_LIMIT),
        name="gla",
    )(proj, proj, proj, proj, sm, wg, bg, gn)


def _merge_kernel(ya_ref, yg_ref, wa_ref, wg_ref, ma_ref, mg_ref, o_ref):
    a = jnp.dot(ya_ref[...], wa_ref[...], preferred_element_type=F32)
    g = jnp.dot(yg_ref[...], wg_ref[...], preferred_element_type=F32)
    o_ref[...] = (ma_ref[...].astype(F32) * a + mg_ref[...].astype(F32) * g).astype(BF16)


def _merge(y_att, y_gla, w_att, w_gla, proj, cols):
    n, d_att = y_att.shape
    d_gla = y_gla.shape[1]
    d = w_att.shape[1]
    tm, tn = 1024, PROJ_TN
    return pl.pallas_call(
        _merge_kernel,
        grid=(n // tm, d // tn),
        in_specs=[
            pl.BlockSpec((tm, d_att), lambda i, j: (i, 0)),
            pl.BlockSpec((tm, d_gla), lambda i, j: (i, 0)),
            pl.BlockSpec((d_att, tn), lambda i, j: (0, j)),
            pl.BlockSpec((d_gla, tn), lambda i, j: (0, j)),
            pl.BlockSpec((tm, tn), lambda i, j: (i, cols["m_att"] // tn + j)),
            pl.BlockSpec((tm, tn), lambda i, j: (i, cols["m_gla"] // tn + j)),
        ],
        out_specs=pl.BlockSpec((tm, tn), lambda i, j: (i, j)),
        out_shape=jax.ShapeDtypeStruct((n, d), BF16),
        compiler_params=pltpu.CompilerParams(
            dimension_semantics=("arbitrary", "arbitrary"), vmem_limit_bytes=VMEM_LIMIT),
        name="merge",
    )(y_att, y_gla, w_att, w_gla, proj, proj)


def _resproj_kernel(m_ref, w_ref, x_ref, gate_ref, o_ref):
    y = jnp.dot(m_ref[...], w_ref[...], preferred_element_type=F32)
    o_ref[...] = x_ref[...] + gate_ref[...] * y


def _resproj(merged, w, x2, gate, seq):
    n, d = x2.shape
    tm, tn = 1024, PROJ_TN
    per_b = seq // tm
    return pl.pallas_call(
        _resproj_kernel,
        grid=(n // tm, d // tn),
        in_specs=[
            pl.BlockSpec((tm, merged.shape[1]), lambda i, j: (i, 0)),
            pl.BlockSpec((merged.shape[1], tn), lambda i, j: (0, j)),
            pl.BlockSpec((tm, tn), lambda i, j: (i, j)),
            pl.BlockSpec((None, 1, tn), lambda i, j: (i // per_b, 0, j)),
        ],
        out_specs=pl.BlockSpec((tm, tn), lambda i, j: (i, j)),
        out_shape=jax.ShapeDtypeStruct((n, d), F32),
        compiler_params=pltpu.CompilerParams(
            dimension_semantics=("arbitrary", "arbitrary"), vmem_limit_bytes=VMEM_LIMIT),
        name="resproj",
    )(merged, w, x2, gate)


FFN_TM = 512
FFN_TF = 512


def _swiglu_step(hn, w1, w3, w2):
    a = jnp.dot(hn, w1, preferred_element_type=F32)
    b = jnp.dot(hn, w3, preferred_element_type=F32)
    return a * _sigmoid(a) * b, w2


def _ffn_kernel(x_ref, g_ref, sc_ref, sh_ref, w1_ref, w3_ref, w2_ref, gate_ref, o_ref,
                hn_ref, acc_ref):
    j = pl.program_id(1)

    @pl.when(j == 0)
    def _():
        hn_ref[...] = _modulate(x_ref[...], g_ref[...], sc_ref[...], sh_ref[...]).astype(BF16)
        acc_ref[...] = jnp.zeros(acc_ref.shape, F32)

    act, w2 = _swiglu_step(hn_ref[...], w1_ref[...], w3_ref[...], w2_ref[...])
    acc_ref[...] += jnp.dot(act.astype(BF16), w2, preferred_element_type=F32)

    @pl.when(j == pl.num_programs(1) - 1)
    def _():
        o_ref[...] = x_ref[...] + gate_ref[...] * acc_ref[...]


def _ffn(x2, g, scale, shift, w1, w3, w2, gate, seq):
    n, d = x2.shape
    dff = w1.shape[1]
    tm, tf = min(FFN_TM, seq), FFN_TF
    per_b = seq // tm
    vec = pl.BlockSpec((None, 1, d), lambda i, j: (i // per_b, 0, 0))
    return pl.pallas_call(
        _ffn_kernel,
        grid=(n // tm, dff // tf),
        in_specs=[
            pl.BlockSpec((tm, d), lambda i, j: (i, 0)),
            pl.BlockSpec((1, d), lambda i, j: (0, 0)),
            vec, vec,
            pl.BlockSpec((d, tf), lambda i, j: (0, j)),
            pl.BlockSpec((d, tf), lambda i, j: (0, j)),
            pl.BlockSpec((tf, d), lambda i, j: (j, 0)),
            vec,
        ],
        out_specs=pl.BlockSpec((tm, d), lambda i, j: (i, 0)),
        out_shape=jax.ShapeDtypeStruct((n, d), F32),
        scratch_shapes=[pltpu.VMEM((tm, d), BF16), pltpu.VMEM((tm, d), F32)],
        compiler_params=pltpu.CompilerParams(
            dimension_semantics=("arbitrary", "arbitrary"), vmem_limit_bytes=VMEM_LIMIT),
        name="ffn_swiglu",
    )(x2, g, scale, shift, w1, w3, w2, gate)


MOE_TM = 512
ROUTE_TM = 1024


def _router_kernel(x_ref, g_ref, sc_ref, sh_ref, wr_ref, hn_ref, rt_ref):
    lane = lax.broadcasted_iota(jnp.int32, (1, LANES), 1)
    lane_f = lane.astype(F32)
    h = _modulate(x_ref[...], g_ref[...], sc_ref[...], sh_ref[...])
    hn_ref[...] = h
    logits = jnp.dot(h, wr_ref[...], preferred_element_type=F32, precision=lax.Precision.HIGHEST)
    logits = jnp.where(lane < N_EXPERTS, logits, -jnp.inf)
    m1 = jnp.max(logits, axis=-1, keepdims=True)
    i1 = jnp.min(jnp.where(logits == m1, lane_f, float(LANES)), axis=-1, keepdims=True)
    rest = jnp.where(lane_f == i1, -jnp.inf, logits)
    m2 = jnp.max(rest, axis=-1, keepdims=True)
    i2 = jnp.min(jnp.where(rest == m2, lane_f, float(LANES)), axis=-1, keepdims=True)
    e2 = jnp.exp(m2 - m1)
    p1 = 1.0 / (1.0 + e2)
    p2 = e2 / (1.0 + e2)
    rt_ref[...] = (jnp.where(lane == 0, i1, 0.0) + jnp.where(lane == 1, i2, 0.0)
                   + jnp.where(lane == 2, p1, 0.0) + jnp.where(lane == 3, p2, 0.0))


def _router(x2, g, scale, shift, w_router, seq):
    n, d = x2.shape
    tm = min(ROUTE_TM, seq)
    per_b = seq // tm
    vec = pl.BlockSpec((None, 1, d), lambda i: (i // per_b, 0, 0))
    return pl.pallas_call(
        _router_kernel,
        grid=(n // tm,),
        in_specs=[
            pl.BlockSpec((tm, d), lambda i: (i, 0)),
            pl.BlockSpec((1, d), lambda i: (0, 0)),
            vec, vec,
            pl.BlockSpec((d, LANES), lambda i: (0, 0)),
        ],
        out_specs=[pl.BlockSpec((tm, d), lambda i: (i, 0)),
                   pl.BlockSpec((tm, LANES), lambda i: (i, 0))],
        out_shape=[jax.ShapeDtypeStruct((n, d), F32), jax.ShapeDtypeStruct((n, LANES), F32)],
        compiler_params=pltpu.CompilerParams(
            dimension_semantics=("arbitrary",), vmem_limit_bytes=VMEM_LIMIT),
        name="moe_router",
    )(x2, g, scale, shift, w_router)


def _route_tables(rt, n, n_exp, tm):
    rows = 2 * n + n_exp * tm
    n_tiles = rows // tm
    ef = rt[:, :2].astype(jnp.int32).T.reshape(-1)
    onehot = (ef[:, None] == jnp.arange(n_exp, dtype=jnp.int32)[None, :]).astype(jnp.int32)
    csum = jnp.cumsum(onehot, axis=0)
    rank = jnp.take_along_axis(csum, ef[:, None], axis=1)[:, 0] - 1
    cnt = csum[-1]
    padded = ((cnt + tm - 1) // tm) * tm
    ends = jnp.cumsum(padded)
    dest = (ends - padded)[ef] + rank
    total = ends[-1]
    tile_start = jnp.arange(n_tiles, dtype=jnp.int32) * tm
    tile_valid = (tile_start < total).astype(jnp.int32)
    tile_expert = jnp.sum((tile_start[:, None] >= ends[None, :]).astype(jnp.int32), axis=1)
    last_tile = jnp.maximum(total // tm - 1, 0)
    tile_expert = jnp.where(tile_valid == 1, tile_expert, tile_expert[last_tile])
    tile_rows = jnp.where(tile_valid == 1, jnp.arange(n_tiles, dtype=jnp.int32), last_tile)
    spare_lo = jnp.concatenate([ends - padded + cnt, total[None]]).astype(jnp.int32)
    spare_hi = jnp.concatenate([ends, jnp.full((1,), rows, ends.dtype)]).astype(jnp.int32)
    return dict(tile_expert=tile_expert, tile_valid=tile_valid, tile_rows=tile_rows, dest=dest,
                spare_lo=spare_lo, spare_hi=spare_hi, rows=rows)


def _row_copies(copy, count):
    def body(r, carry):
        copy(r).start()
        return carry
    lax.fori_loop(0, count, body, 0, unroll=8)


def _dispatch_kernel(dest_ref, lo_ref, hi_ref, hn_ref, xs_hbm, zero_ref, sem, zsem, *, n, n_spare):
    i = pl.program_id(0)
    tm = hn_ref.shape[0]

    @pl.when(i == 0)
    def _():
        zero_ref[...] = jnp.zeros(zero_ref.shape, F32)
        for e in range(n_spare):
            def zero_row(r):
                return pltpu.make_async_copy(zero_ref.at[pl.ds(0, 1)], xs_hbm.at[pl.ds(r, 1)], zsem)

            def start(r, carry):
                zero_row(r).start()
                return carry

            def drain(r, carry):
                zero_row(r).wait()
                return carry
            lax.fori_loop(lo_ref[e], hi_ref[e], start, 0)
            lax.fori_loop(lo_ref[e], hi_ref[e], drain, 0)

    for k in range(2):
        _row_copies(lambda r, k=k: pltpu.make_async_copy(
            hn_ref.at[pl.ds(r, 1)], xs_hbm.at[pl.ds(dest_ref[k * n + i * tm + r], 1)], sem), tm)
    for k in range(2):
        pltpu.make_async_copy(hn_ref, xs_hbm.at[pl.ds(0, tm)], sem).wait()


def _dispatch(hn, tables):
    n, d = hn.shape
    tm = MOE_TM
    kern = functools.partial(_dispatch_kernel, n=n, n_spare=tables["spare_lo"].shape[0])
    grid_spec = pltpu.PrefetchScalarGridSpec(
        num_scalar_prefetch=3,
        grid=(n // tm,),
        in_specs=[pl.BlockSpec((tm, d), lambda i, dest, lo, hi: (i, 0))],
        out_specs=pl.BlockSpec(memory_space=pl.ANY),
        scratch_shapes=[pltpu.VMEM((8, d), F32), pltpu.SemaphoreType.DMA(()),
                        pltpu.SemaphoreType.DMA(())],
    )
    return pl.pallas_call(
        kern,
        grid_spec=grid_spec,
        out_shape=jax.ShapeDtypeStruct((tables["rows"], d), F32),
        compiler_params=pltpu.CompilerParams(
            dimension_semantics=("arbitrary",), vmem_limit_bytes=VMEM_LIMIT),
        name="moe_dispatch",
    )(tables["dest"], tables["spare_lo"], tables["spare_hi"], hn)


def _moe_grouped_kernel(te_ref, tv_ref, tr_ref, x_ref, w1_ref, w3_ref, w2_ref, y_ref, xb_ref):
    t = pl.program_id(0)
    j = pl.program_id(1)
    valid = tv_ref[t] == 1

    @pl.when(j == 0)
    def _():
        xb_ref[...] = x_ref[...].astype(BF16)
        y_ref[...] = jnp.zeros(y_ref.shape, F32)

    @pl.when(valid)
    def _():
        act, w2 = _swiglu_step(xb_ref[...], w1_ref[...], w3_ref[...], w2_ref[...])
        y_ref[...] += jnp.dot(act.astype(BF16), w2, preferred_element_type=F32)


def _moe_grouped(xs, tables, w1, w3, w2):
    rows, d = xs.shape
    dff = w1.shape[2]
    tm, tf = MOE_TM, FFN_TF
    nj = dff // tf

    def jj(t, j, tv):
        return j * tv[t] + (nj - 1) * (1 - tv[t])

    grid_spec = pltpu.PrefetchScalarGridSpec(
        num_scalar_prefetch=3,
        grid=(rows // tm, nj),
        in_specs=[
            pl.BlockSpec((tm, d), lambda t, j, te, tv, tr: (tr[t], 0)),
            pl.BlockSpec((None, d, tf), lambda t, j, te, tv, tr: (te[t], 0, jj(t, j, tv))),
            pl.BlockSpec((None, d, tf), lambda t, j, te, tv, tr: (te[t], 0, jj(t, j, tv))),
            pl.BlockSpec((None, tf, d), lambda t, j, te, tv, tr: (te[t], jj(t, j, tv), 0)),
        ],
        out_specs=pl.BlockSpec((tm, d), lambda t, j, te, tv, tr: (t, 0)),
        scratch_shapes=[pltpu.VMEM((tm, d), BF16)],
    )
    return pl.pallas_call(
        _moe_grouped_kernel,
        grid_spec=grid_spec,
        out_shape=jax.ShapeDtypeStruct((rows, d), F32),
        compiler_params=pltpu.CompilerParams(
            dimension_semantics=("arbitrary", "arbitrary"), vmem_limit_bytes=VMEM_LIMIT),
        name="moe_grouped",
    )(tables["tile_expert"], tables["tile_valid"], tables["tile_rows"], xs, w1, w3, w2)


def _moe_combine_kernel(dest_ref, x_ref, rt_ref, gate_ref, y_hbm, o_ref, y0_ref, y1_ref, sem, *, n):
    i = pl.program_id(0)
    tm = x_ref.shape[0]
    for k, yk_ref in enumerate((y0_ref, y1_ref)):
        _row_copies(lambda r, k=k, yk_ref=yk_ref: pltpu.make_async_copy(
            y_hbm.at[pl.ds(dest_ref[k * n + i * tm + r], 1)], yk_ref.at[pl.ds(r, 1)], sem), tm)
    for yk_ref in (y0_ref, y1_ref):
        pltpu.make_async_copy(y_hbm.at[pl.ds(0, tm)], yk_ref, sem).wait()
    p1 = rt_ref[:, 2:3]
    p2 = rt_ref[:, 3:4]
    o_ref[...] = x_ref[...] + gate_ref[...] * (p1 * y0_ref[...] + p2 * y1_ref[...])


def _moe_combine(x2, y, rt, gate, dest, seq):
    n, d = x2.shape
    tm = min(FFN_TM, seq)
    per_b = seq // tm
    kern = functools.partial(_moe_combine_kernel, n=n)
    grid_spec = pltpu.PrefetchScalarGridSpec(
        num_scalar_prefetch=1,
        grid=(n // tm,),
        in_specs=[
            pl.BlockSpec((tm, d), lambda i, dest: (i, 0)),
            pl.BlockSpec((tm, LANES), lambda i, dest: (i, 0)),
            pl.BlockSpec((None, 1, d), lambda i, dest: (i // per_b, 0, 0)),
            pl.BlockSpec(memory_space=pl.ANY),
        ],
        out_specs=pl.BlockSpec((tm, d), lambda i, dest: (i, 0)),
        scratch_shapes=[pltpu.VMEM((tm, d), F32), pltpu.VMEM((tm, d), F32),
                        pltpu.SemaphoreType.DMA(())],
    )
    return pl.pallas_call(
        kern,
        grid_spec=grid_spec,
        out_shape=jax.ShapeDtypeStruct((n, d), F32),
        compiler_params=pltpu.CompilerParams(
            dimension_semantics=("arbitrary",), vmem_limit_bytes=VMEM_LIMIT),
        name="moe_combine",
    )(dest, x2, rt, gate, y)


def _moe(x2, g, scale, shift, w_router, w1, w3, w2, gate, seq):
    n = x2.shape[0]
    hn, rt = _router(x2, g, scale, shift, w_router, seq)
    tables = _route_tables(rt, n, w1.shape[0], MOE_TM)
    xs = _dispatch(hn, tables)
    y = _moe_grouped(xs, tables, w1, w3, w2)
    return _moe_combine(x2, y, rt, gate, tables["dest"], seq)


def _prep_in_proj(w_in, q_norm, k_norm, idx_k_norm, d):
    sizes = dict(aq=ATT_HEADS * HEAD_DIM, ak=ATT_KV_HEADS * HEAD_DIM, av=ATT_KV_HEADS * HEAD_DIM,
                 iq=IDX_HEADS * HEAD_DIM, ik=HEAD_DIM, iw=IDX_HEADS,
                 gq=d // 2, gk=d // 2, gv=d, glr=GLA_GATE_RANK, gr=d, m_att=d, m_gla=d)
    src, acc = {}, 0
    for name in ("aq", "ak", "av", "iq", "ik", "iw", "gq", "gk", "gv", "glr", "gr", "m_att", "m_gla"):
        src[name] = (acc, sizes[name])
        acc += sizes[name]
    order = ("aq", "iq", "gv", "gr", "m_att", "m_gla", "ak", "av", "gq", "gk")
    cols, parts, off = {}, [], 0
    for name in order:
        s, width = src[name]
        cols[name] = off
        parts.append(w_in[:, s:s + width])
        off += width
    w_main = jnp.concatenate(parts, axis=1).astype(BF16)

    def seg(name):
        s, width = src[name]
        return w_in[:, s:s + width]

    pad = jnp.zeros((d, LANES - IDX_HEADS - GLA_GATE_RANK), w_in.dtype)
    w_small = jnp.concatenate([seg("ik"), seg("iw"), seg("glr"), pad], axis=1).astype(BF16)

    att_scale = HEAD_DIM ** -0.5 * LOG2_E
    idx_scale = (HEAD_DIM ** -0.5) * (IDX_HEADS ** -0.5)
    cg = jnp.ones((off,), F32)
    cg = cg.at[cols["aq"]:cols["aq"] + sizes["aq"]].set(jnp.tile(q_norm * att_scale, ATT_HEADS))
    cg = cg.at[cols["ak"]:cols["ak"] + sizes["ak"]].set(jnp.tile(k_norm, ATT_KV_HEADS))
    cg2 = jnp.concatenate([idx_k_norm, jnp.full((IDX_HEADS,), idx_scale, F32),
                           jnp.ones((LANES - IDX_HEADS,), F32)])
    norm_tiles = tuple(range(cols["aq"] // PROJ_TN, (cols["aq"] + sizes["aq"]) // PROJ_TN)) + \
        tuple(range(cols["ak"] // PROJ_TN, (cols["ak"] + sizes["ak"]) // PROJ_TN))
    sig_tiles = (cols["m_att"] // PROJ_TN, (cols["m_gla"] + sizes["m_gla"]) // PROJ_TN)
    return w_main, w_small, cg.reshape(1, -1), cg2.reshape(1, -1), cols, norm_tiles, sig_tiles


def kernel(x, c, ada_w, ada_b, norm_mix, norm_ffn, w_in, q_norm, k_norm, idx_k_norm, w_gla_gate,
           b_gla_gate, gla_out_norm, w_out_attn, w_out_gla, w_out, ffn_w1, ffn_w3, ffn_w2,
           moe_router, moe_w1, moe_w3, moe_w2):
    batch, seq, d = x.shape
    depth = ada_w.shape[0]
    n = batch * seq
    x2 = x.reshape(n, d)

    c8 = jnp.zeros((8, d), F32).at[:batch].set(c)
    mod_all = _ada(c8, ada_w, ada_b)

    for layer in range(depth):
        mod = mod_all[layer, :batch].reshape(batch, 6, 1, d)
        shift_m, scale_m, gate_m = mod[:, 0], mod[:, 1], mod[:, 2]
        shift_f, scale_f, gate_f = mod[:, 3], mod[:, 4], mod[:, 5]

        w_main, w_small, cg, cg2, cols, norm_tiles, sig_tiles = _prep_in_proj(
            w_in[layer], q_norm[layer], k_norm[layer], idx_k_norm[layer], d)
        proj, ik, sm = _modproj(x2, norm_mix[layer].reshape(1, d), scale_m, shift_m,
                                w_main, cg, w_small, cg2, seq, norm_tiles, sig_tiles)
        y_att = _attention(proj, ik, sm, batch, seq, cols)
        wg = jnp.zeros((LANES, w_gla_gate.shape[2]), F32).at[
            GLA_GATE_RANK:2 * GLA_GATE_RANK].set(w_gla_gate[layer])
        y_gla = _gla(proj, sm, wg, b_gla_gate[layer].reshape(1, -1),
                     gla_out_norm[layer].reshape(1, -1), batch, seq, cols)
        merged = _merge(y_att, y_gla, w_out_attn[layer].astype(BF16),
                        w_out_gla[layer].astype(BF16), proj, cols)
        x2 = _resproj(merged, w_out[layer].astype(BF16), x2, gate_m, seq)

        g_f = norm_ffn[layer].reshape(1, d)
        i = layer // 2
        if layer % 2 == 0:
            x2 = _ffn(x2, g_f, scale_f, shift_f, ffn_w1[i].astype(BF16), ffn_w3[i].astype(BF16),
                      ffn_w2[i].astype(BF16), gate_f, seq)
        else:
            w_r = jnp.zeros((d, LANES), F32).at[:, :N_EXPERTS].set(moe_router[i])
            x2 = _moe(x2, g_f, scale_f, shift_f, w_r, moe_w1[i].astype(BF16),
                      moe_w3[i].astype(BF16), moe_w2[i].astype(BF16), gate_f, seq)
    return x2.reshape(batch, seq, d)
```

```python
import functools

import jax
import jax.numpy as jnp
from jax import lax
from jax.experimental import pallas as pl
from jax.experimental.pallas import tpu as pltpu

F32 = jnp.float32
BF16 = jnp.bfloat16

ATT_HEADS = 16
ATT_KV_HEADS = 4
HEAD_DIM = 128
IDX_HEADS = 16
TOPK_MAX = 256
GLA_HEADS = 4
GLA_GATE_RANK = 16
GLA_GATE_TAU = 16.0
GLA_CHUNK = 64
N_EXPERTS = 8
EPS = 1e-6
LANES = 128
NEG_BIG = -1e30
INT_MIN = -(2 ** 31)
LOG2_E = 1.4426950408889634

PROJ_SEG = 512
PROJ_TN = 1024
VMEM_LIMIT = 56 * 1024 * 1024


def _nt_dot(a, b):
    return lax.dot_general(a, b, (((1,), (1,)), ((), ())), preferred_element_type=F32)


def _rms(a):
    return a * lax.rsqrt(jnp.mean(a * a, axis=-1, keepdims=True) + EPS)


def _sigmoid(a):
    return 1.0 / (1.0 + jnp.exp(-a))


def _modulate(x, g, scale, shift):
    return _rms(x) * g * (1.0 + scale) + shift


def _ada_kernel(c_ref, w_ref, b_ref, o_ref):
    c = c_ref[...]
    cond = c * _sigmoid(c)
    o_ref[...] = jnp.dot(cond, w_ref[...], preferred_element_type=F32,
                         precision=lax.Precision.HIGHEST) + b_ref[...]


def _ada(c8, ada_w, ada_b):
    depth, d, n = ada_w.shape
    tn = 1024
    return pl.pallas_call(
        _ada_kernel,
        grid=(depth, n // tn),
        in_specs=[
            pl.BlockSpec((8, d), lambda l, j: (0, 0)),
            pl.BlockSpec((None, d, tn), lambda l, j: (l, 0, j)),
            pl.BlockSpec((None, 1, tn), lambda l, j: (l, 0, j)),
        ],
        out_specs=pl.BlockSpec((None, 8, tn), lambda l, j: (l, 0, j)),
        out_shape=jax.ShapeDtypeStruct((depth, 8, n), F32),
        name="ada_mod",
    )(c8, ada_w, ada_b.reshape(depth, 1, n))


def _modproj_kernel(x_ref, g_ref, sc_ref, sh_ref, w_ref, cg_ref, w2_ref, cg2_ref,
                    o_ref, ik_ref, sm_ref, hn_ref, *, norm_tiles, sig_tiles):
    j = pl.program_id(1)

    @pl.when(j == 0)
    def _():
        h = _modulate(x_ref[...], g_ref[...], sc_ref[...], sh_ref[...]).astype(BF16)
        hn_ref[...] = h
        small = jnp.dot(h, w2_ref[...], preferred_element_type=F32)
        ik_ref[...] = (_rms(small[:, :LANES]) * cg2_ref[:, :LANES]).astype(BF16)
        sm_ref[...] = small[:, LANES:] * cg2_ref[:, LANES:]

    acc = jnp.dot(hn_ref[...], w_ref[...], preferred_element_type=F32)

    for part in range(PROJ_TN // PROJ_SEG):
        seg = j * (PROJ_TN // PROJ_SEG) + part
        cols = slice(part * PROJ_SEG, (part + 1) * PROJ_SEG)
        is_norm = functools.reduce(jnp.logical_or, [seg == t for t in norm_tiles])
        is_sig = jnp.logical_and(seg >= sig_tiles[0], seg < sig_tiles[1])

        @pl.when(is_norm)
        def _(cols=cols):
            for c in range(cols.start, cols.stop, LANES):
                sl = slice(c, c + LANES)
                o_ref[:, sl] = (_rms(acc[:, sl]) * cg_ref[:, sl]).astype(BF16)

        @pl.when(is_sig)
        def _(cols=cols):
            o_ref[:, cols] = _sigmoid(acc[:, cols]).astype(BF16)

        @pl.when(jnp.logical_not(jnp.logical_or(is_norm, is_sig)))
        def _(cols=cols):
            o_ref[:, cols] = acc[:, cols].astype(BF16)


def _modproj(x2, g, scale, shift, w, cg, w2, cg2, seq, norm_tiles, sig_tiles):
    n, d = x2.shape
    ncols = w.shape[1]
    tm = min(1024, seq)
    per_b = seq // tm
    kern = functools.partial(_modproj_kernel, norm_tiles=norm_tiles, sig_tiles=sig_tiles)
    return pl.pallas_call(
        kern,
        grid=(n // tm, ncols // PROJ_TN),
        in_specs=[
            pl.BlockSpec((tm, d), lambda i, j: (i, 0)),
            pl.BlockSpec((1, d), lambda i, j: (0, 0)),
            pl.BlockSpec((None, 1, d), lambda i, j: (i // per_b, 0, 0)),
            pl.BlockSpec((None, 1, d), lambda i, j: (i // per_b, 0, 0)),
            pl.BlockSpec((d, PROJ_TN), lambda i, j: (0, j)),
            pl.BlockSpec((1, PROJ_TN), lambda i, j: (0, j)),
            pl.BlockSpec((d, 2 * LANES), lambda i, j: (0, 0)),
            pl.BlockSpec((1, 2 * LANES), lambda i, j: (0, 0)),
        ],
        out_specs=[
            pl.BlockSpec((tm, PROJ_TN), lambda i, j: (i, j)),
            pl.BlockSpec((tm, LANES), lambda i, j: (i, 0)),
            pl.BlockSpec((tm, LANES), lambda i, j: (i, 0)),
        ],
        out_shape=[
            jax.ShapeDtypeStruct((n, ncols), BF16),
            jax.ShapeDtypeStruct((n, LANES), BF16),
            jax.ShapeDtypeStruct((n, LANES), F32),
        ],
        scratch_shapes=[pltpu.VMEM((tm, d), BF16)],
        compiler_params=pltpu.CompilerParams(
            dimension_semantics=("arbitrary", "arbitrary"), vmem_limit_bytes=VMEM_LIMIT),
        name="modproj",
    )(x2, g, scale, shift, w, cg, w2, cg2)


ATT_TQ = 128
ATT_TK = 256
BISECT_UNCHECKED_BITS = 19
ATT_SPLIT = 4


def _sortable(a):
    a = jnp.where(a == 0.0, 0.0, a)
    bits = pltpu.bitcast(a, jnp.int32)
    return bits ^ ((bits >> 31) & 0x7FFFFFFF)


def _attn_kernel(aq_ref, iq_ref, ak_ref, av_ref, ik_ref, iw_ref, o_ref,
                 sc_ref, sct_ref, qs_ref, wb_ref, m_ref, acc_ref, *, topk, seq):
    tq, tk = ATT_TQ, ATT_TK
    group = ATT_HEADS // ATT_KV_HEADS
    qi = pl.program_id(1)
    n_kt = (qi * tq + tq + tk - 1) // tk
    row = qi * tq + lax.broadcasted_iota(jnp.int32, (tq, 1), 0)
    lane_col = lax.broadcasted_iota(jnp.int32, (1, tk), 1)
    for h in range(IDX_HEADS):
        wb_ref[h] = jnp.broadcast_to(iw_ref[:, h:h + 1], (tq, LANES))
    for h in range(ATT_HEADS):
        qs_ref[h * tq:(h + 1) * tq, :] = aq_ref[:, h * HEAD_DIM:(h + 1) * HEAD_DIM]

    n_pair = (n_kt + 1) // 2

    def score_pair(i, carry):
        kblk = ik_ref[pl.ds(pl.multiple_of(i * 2 * tk, 2 * tk), 2 * tk), :]
        acc = jnp.zeros((tq, 2 * tk), F32)
        for h in range(IDX_HEADS):
            z = _nt_dot(iq_ref[:, h * HEAD_DIM:(h + 1) * HEAD_DIM], kblk)
            wb = wb_ref[h]
            acc = acc + jnp.concatenate([wb] * (2 * tk // LANES), axis=1) * jnp.maximum(z, 0.0)
        keys = []
        for u in range(2):
            kt = 2 * i + u
            causal = (kt * tk + lane_col) <= row
            keys.append(jnp.where(causal, _sortable(acc[:, u * tk:(u + 1) * tk]), INT_MIN))
            sc_ref[kt] = keys[u]
        sct_ref[i] = jnp.concatenate(keys, axis=1).T
        return carry

    lax.fori_loop(0, n_pair, score_pair, 0)

    def count_where(pred):
        def body(i, c):
            for u in range(2):
                kt = 2 * i + u
                hit = jnp.where(pred(sc_ref[kt], kt), 1.0, 0.0)
                for cc in range(tk // LANES):
                    c = c + hit[:, cc * LANES:(cc + 1) * LANES]
            return c
        c = lax.fori_loop(0, n_pair, body, jnp.zeros((tq, LANES), F32))
        return jnp.sum(c, axis=-1, keepdims=True)

    kf = float(topk)

    def count_ge_t(t):
        def body(i, c):
            hit = jnp.where(sct_ref[i] >= t, 1.0, 0.0)
            return c + jnp.sum(hit.reshape(2 * tk // 64, 8, 8, tq), axis=0)
        c = lax.fori_loop(0, n_pair, body, jnp.zeros((8, 8, tq), F32))
        return jnp.sum(jnp.sum(c, axis=0), axis=0, keepdims=True)

    c0 = count_ge_t(jnp.zeros((1, tq), jnp.int32))
    t0 = jnp.where(c0 >= kf, 0, INT_MIN).astype(jnp.int32)
    n_all = jnp.full((1, tq), float(seq), F32)

    def bit_step(i, state):
        t, ct, open_rows = state
        cand = t | jnp.left_shift(jnp.int32(1), 30 - i)
        c = count_ge_t(cand)
        take = jnp.logical_and(c >= kf, open_rows > 0.0)
        t = jnp.where(take, cand, t)
        ct = jnp.where(take, c, ct)
        open_rows = jnp.where(c == kf, 0.0, open_rows)
        return t, ct, open_rows

    def any_open(open_rows):
        return (jnp.max(open_rows) > 0.0).astype(jnp.int32)

    def tail_cond(state):
        i, _, go = state
        return jnp.logical_and(i < 31, go == 1)

    def tail_step(state):
        i, inner, _ = state
        inner = bit_step(i + 1, bit_step(i, inner))
        return i + 2, inner, any_open(inner[2])

    state = (t0, jnp.where(c0 >= kf, c0, n_all), jnp.where(c0 == kf, 0.0, 1.0))
    state = lax.fori_loop(0, BISECT_UNCHECKED_BITS, bit_step, state)
    _, (thr_t, cnt_t, open_t), _ = lax.while_loop(
        tail_cond, tail_step, (jnp.int32(BISECT_UNCHECKED_BITS), state, any_open(state[2])))

    eye = lax.broadcasted_iota(jnp.int32, (tq, tq), 0) == lax.broadcasted_iota(jnp.int32, (tq, tq), 1)

    def to_col(v):
        return jnp.sum(jnp.where(eye, v, 0.0), axis=1, keepdims=True)

    thr = ((to_col((thr_t >> 16).astype(F32)).astype(jnp.int32) << 16)
           | to_col((thr_t & 0xFFFF).astype(F32)).astype(jnp.int32))
    cnt_thr = to_col(cnt_t)
    open_rows = to_col(open_t)

    excess = jnp.logical_and(jnp.logical_and(open_rows > 0.0, cnt_thr > kf), thr > INT_MIN)

    @pl.when(jnp.max(jnp.where(excess, 1.0, 0.0)) > 0.0)
    def _():
        need = kf - count_where(lambda key, kt: key > thr)

        def ties_below(j):
            return count_where(
                lambda key, kt: jnp.logical_and(key == thr, (kt * tk + lane_col) < j))

        nbits = max(seq - 1, 1).bit_length()

        def idx_step(i, j0):
            cand = j0 | jnp.left_shift(jnp.int32(1), nbits - 1 - i)
            return jnp.where(ties_below(cand) < need, cand, j0)

        j0 = lax.fori_loop(0, nbits, idx_step, jnp.zeros((tq, 1), jnp.int32))

        def demote(kt, carry):
            key = sc_ref[kt]
            late_tie = jnp.logical_and(key == thr, (kt * tk + lane_col) > j0)
            sc_ref[kt] = jnp.where(jnp.logical_and(excess, late_tie), INT_MIN, key)
            return carry

        lax.fori_loop(0, n_kt, demote, 0)

    thr = jnp.maximum(thr, INT_MIN + 1)

    m_ref[...] = jnp.full(m_ref.shape, NEG_BIG, F32)
    acc_ref[...] = jnp.zeros(acc_ref.shape, F32)
    tk2 = 2 * tk
    ones = jnp.ones((tk2, HEAD_DIM), BF16)

    def kv_pair(i, carry):
        r0 = pl.multiple_of(i * tk2, tk2)
        keep = jnp.concatenate([sc_ref[2 * i] >= thr, sc_ref[2 * i + 1] >= thr], axis=1)[None]
        for g in range(ATT_KV_HEADS):
            kblk = ak_ref[pl.ds(r0, tk2), g * HEAD_DIM:(g + 1) * HEAD_DIM]
            vblk = av_ref[pl.ds(r0, tk2), g * HEAD_DIM:(g + 1) * HEAD_DIM]
            v1 = jnp.concatenate([vblk, ones], axis=1)
            for part in range(ATT_SPLIT):
                nh = group // ATT_SPLIT
                rows = slice(part * nh * tq, (part + 1) * nh * tq)
                q4 = qs_ref[(g * group + part * nh) * tq:(g * group + (part + 1) * nh) * tq, :]
                s = _nt_dot(q4, kblk).reshape(nh, tq, tk2)
                s = jnp.where(keep, s, NEG_BIG).reshape(nh * tq, tk2)
                m_old = m_ref[g, rows, :]
                m_new = jnp.maximum(m_old, jnp.max(s, axis=-1, keepdims=True))
                alpha = jnp.exp2(m_old - m_new)
                p = jnp.exp2(s - jnp.concatenate([m_new] * (tk2 // LANES), axis=1))
                pv = jnp.dot(p.astype(BF16), v1, preferred_element_type=F32)
                acc_ref[g, rows, :] = jnp.concatenate([alpha, alpha], axis=1) * acc_ref[g, rows, :] + pv
                m_ref[g, rows, :] = m_new
        return carry

    lax.fori_loop(0, n_pair, kv_pair, 0)

    for g in range(ATT_KV_HEADS):
        og = acc_ref[g]
        og = og[:, :HEAD_DIM] / og[:, HEAD_DIM:]
        for r in range(group):
            h = g * group + r
            o_ref[:, h * HEAD_DIM:(h + 1) * HEAD_DIM] = og[r * tq:(r + 1) * tq, :].astype(BF16)


def _attention(proj, ik, sm, batch, seq, cols):
    n = proj.shape[0]
    tq, tk = ATT_TQ, ATT_TK
    nq = seq // tq
    width = ATT_HEADS * HEAD_DIM
    kvw = ATT_KV_HEADS * HEAD_DIM
    topk = min(TOPK_MAX, seq // 4)
    group = ATT_HEADS // ATT_KV_HEADS
    kern = functools.partial(_attn_kernel, topk=topk, seq=seq)
    once = pl.Buffered(1)
    return pl.pallas_call(
        kern,
        grid=(batch, nq),
        in_specs=[
            pl.BlockSpec((tq, width), lambda b, q: (b * nq + q, cols["aq"] // width)),
            pl.BlockSpec((tq, width), lambda b, q: (b * nq + q, cols["iq"] // width)),
            pl.BlockSpec((seq, kvw), lambda b, q: (b, cols["ak"] // kvw), pipeline_mode=once),
            pl.BlockSpec((seq, kvw), lambda b, q: (b, cols["av"] // kvw), pipeline_mode=once),
            pl.BlockSpec((seq, LANES), lambda b, q: (b, 0), pipeline_mode=once),
            pl.BlockSpec((tq, LANES), lambda b, q: (b * nq + q, 0)),
        ],
        out_specs=pl.BlockSpec((tq, width), lambda b, q: (b * nq + q, 0)),
        out_shape=jax.ShapeDtypeStruct((n, width), BF16),
        scratch_shapes=[
            pltpu.VMEM((seq // tk, tq, tk), jnp.int32),
            pltpu.VMEM((seq // (2 * tk), 2 * tk, tq), jnp.int32),
            pltpu.VMEM((ATT_HEADS * tq, HEAD_DIM), BF16),
            pltpu.VMEM((IDX_HEADS, tq, LANES), F32),
            pltpu.VMEM((ATT_KV_HEADS, group * tq, LANES), F32),
            pltpu.VMEM((ATT_KV_HEADS, group * tq, 2 * HEAD_DIM), F32),
        ],
        compiler_params=pltpu.CompilerParams(
            dimension_semantics=("arbitrary", "arbitrary"), vmem_limit_bytes=VMEM_LIMIT),
        name="dsa_attention",
    )(proj, proj, proj, proj, ik, sm)


GLA_T = 256


def _gla_kernel(q_ref, k_ref, v_ref, r_ref, lr_ref, wg_ref, bg_ref, gn_ref, o_ref, st_ref, *, dk, dv):
    c_len = GLA_CHUNK
    hi = lax.Precision.HIGHEST

    @pl.when(pl.program_id(1) == 0)
    def _():
        st_ref[...] = jnp.zeros(st_ref.shape, F32)

    ri = lax.broadcasted_iota(jnp.int32, (c_len, c_len), 0)
    ci = lax.broadcasted_iota(jnp.int32, (c_len, c_len), 1)
    lower = ri >= ci
    tril = jnp.where(lower, 1.0, 0.0).astype(F32)
    qscale = dk ** -0.5

    for c in range(GLA_T // c_len):
        rows = slice(c * c_len, (c + 1) * c_len)
        gl = jnp.dot(lr_ref[rows, :], wg_ref[...], preferred_element_type=F32, precision=hi) + bg_ref[...]
        log_a = (jnp.minimum(gl, 0.0) - jnp.log(1.0 + jnp.exp(-jnp.abs(gl)))) / GLA_GATE_TAU
        b = jnp.dot(tril, log_a, preferred_element_type=F32, precision=hi)
        b_last = b[c_len - 1:c_len, :]
        k = k_ref[rows, :].astype(F32)
        q_dec = (q_ref[rows, :].astype(F32) * qscale * jnp.exp(b)).astype(BF16)
        k_inv = (k * jnp.exp(-b)).astype(BF16)
        k_tail = k * jnp.exp(b_last - b)
        for h in range(GLA_HEADS):
            ks = slice(h * dk, (h + 1) * dk)
            vs = slice(h * dv, (h + 1) * dv)
            v = v_ref[rows, vs]
            a = jnp.where(lower, _nt_dot(q_dec[:, ks], k_inv[:, ks]), 0.0)
            state = st_ref[h]
            o = (jnp.dot(a.astype(BF16), v, preferred_element_type=F32)
                 + jnp.dot(q_dec[:, ks], state.astype(BF16), preferred_element_type=F32))
            dec = jnp.exp(b[:, ks].T[:, c_len - 1:c_len])
            upd = jnp.dot(k_tail[:, ks].T.astype(BF16), v, preferred_element_type=F32)
            for cc in range(dv // LANES):
                sl = slice(cc * LANES, (cc + 1) * LANES)
                st_ref[h, :, sl] = dec * state[:, sl] + upd[:, sl]
            r = r_ref[rows, vs].astype(F32)
            o_ref[rows, vs] = (_rms(o) * gn_ref[...] * (r * _sigmoid(r))).astype(BF16)


def _gla(proj, sm, wg, bg, gn, batch, seq, cols):
    n = proj.shape[0]
    dk = wg.shape[1] // GLA_HEADS
    dv = gn.shape[1]
    t = min(GLA_T, seq)
    assert t == GLA_T
    ns = seq // t
    kw, vw = GLA_HEADS * dk, GLA_HEADS * dv
    kern = functools.partial(_gla_kernel, dk=dk, dv=dv)
    return pl.pallas_call(
        kern,
        grid=(batch, ns),
        in_specs=[
            pl.BlockSpec((t, kw), lambda b, s: (b * ns + s, cols["gq"] // kw)),
            pl.BlockSpec((t, kw), lambda b, s: (b * ns + s, cols["gk"] // kw)),
            pl.BlockSpec((t, vw), lambda b, s: (b * ns + s, cols["gv"] // vw)),
            pl.BlockSpec((t, vw), lambda b, s: (b * ns + s, cols["gr"] // vw)),
            pl.BlockSpec((t, LANES), lambda b, s: (b * ns + s, 0)),
            pl.BlockSpec((LANES, kw), lambda b, s: (0, 0)),
            pl.BlockSpec((1, kw), lambda b, s: (0, 0)),
            pl.BlockSpec((1, dv), lambda b, s: (0, 0)),
        ],
        out_specs=pl.BlockSpec((t, vw), lambda b, s: (b * ns + s, 0)),
        out_shape=jax.ShapeDtypeStruct((n, vw), BF16),
        scratch_shapes=[pltpu.VMEM((GLA_HEADS, dk, dv), F32)],
        compiler_params=pltpu.CompilerParams(
            dimension_semantics=("arbitrary", "arbitrary"),
            vmem_limit_bytes=VMEM_LIMIT),
        name="gla",
    )(proj, proj, proj, proj, sm, wg, bg, gn)


def _merge_kernel(ya_ref, yg_ref, wa_ref, wg_ref, ma_ref, mg_ref, o_ref):
    a = jnp.dot(ya_ref[...], wa_ref[...], preferred_element_type=F32)
    g = jnp.dot(yg_ref[...], wg_ref[...], preferred_element_type=F32)
    o_ref[...] = (ma_ref[...].astype(F32) * a + mg_ref[...].astype(F32) * g).astype(BF16)


def _merge(y_att, y_gla, w_att, w_gla, proj, cols):
    n, d_att = y_att.shape
    d_gla = y_gla.shape[1]
    d = w_att.shape[1]
    tm, tn = 1024, PROJ_TN
    return pl.pallas_call(
        _merge_kernel,
        grid=(n // tm, d // tn),
        in_specs=[
            pl.BlockSpec((tm, d_att), lambda i, j: (i, 0)),
            pl.BlockSpec((tm, d_gla), lambda i, j: (i, 0)),
            pl.BlockSpec((d_att, tn), lambda i, j: (0, j)),
            pl.BlockSpec((d_gla, tn), lambda i, j: (0, j)),
            pl.BlockSpec((tm, tn), lambda i, j: (i, cols["m_att"] // tn + j)),
            pl.BlockSpec((tm, tn), lambda i, j: (i, cols["m_gla"] // tn + j)),
        ],
        out_specs=pl.BlockSpec((tm, tn), lambda i, j: (i, j)),
        out_shape=jax.ShapeDtypeStruct((n, d), BF16),
        compiler_params=pltpu.CompilerParams(
            dimension_semantics=("arbitrary", "arbitrary"), vmem_limit_bytes=VMEM_LIMIT),
        name="merge",
    )(y_att, y_gla, w_att, w_gla, proj, proj)


def _resproj_kernel(m_ref, w_ref, x_ref, gate_ref, o_ref):
    y = jnp.dot(m_ref[...], w_ref[...], preferred_element_type=F32)
    o_ref[...] = x_ref[...] + gate_ref[...] * y


def _resproj(merged, w, x2, gate, seq):
    n, d = x2.shape
    tm, tn = 1024, PROJ_TN
    per_b = seq // tm
    return pl.pallas_call(
        _resproj_kernel,
        grid=(n // tm, d // tn),
        in_specs=[
            pl.BlockSpec((tm, merged.shape[1]), lambda i, j: (i, 0)),
            pl.BlockSpec((merged.shape[1], tn), lambda i, j: (0, j)),
            pl.BlockSpec((tm, tn), lambda i, j: (i, j)),
            pl.BlockSpec((None, 1, tn), lambda i, j: (i // per_b, 0, j)),
        ],
        out_specs=pl.BlockSpec((tm, tn), lambda i, j: (i, j)),
        out_shape=jax.ShapeDtypeStruct((n, d), F32),
        compiler_params=pltpu.CompilerParams(
            dimension_semantics=("arbitrary", "arbitrary"), vmem_limit_bytes=VMEM_LIMIT),
        name="resproj",
    )(merged, w, x2, gate)


FFN_TM = 512
FFN_TF = 1024


def _swiglu_step(hn, w1, w3, w2):
    a = jnp.dot(hn, w1, preferred_element_type=F32)
    b = jnp.dot(hn, w3, preferred_element_type=F32)
    return a * _sigmoid(a) * b, w2


def _ffn_kernel(x_ref, g_ref, sc_ref, sh_ref, w1_ref, w3_ref, w2_ref, gate_ref, o_ref,
                hn_ref, acc_ref):
    j = pl.program_id(1)

    @pl.when(j == 0)
    def _():
        hn_ref[...] = _modulate(x_ref[...], g_ref[...], sc_ref[...], sh_ref[...]).astype(BF16)
        acc_ref[...] = jnp.zeros(acc_ref.shape, F32)

    act, w2 = _swiglu_step(hn_ref[...], w1_ref[...], w3_ref[...], w2_ref[...])
    acc_ref[...] += jnp.dot(act.astype(BF16), w2, preferred_element_type=F32)

    @pl.when(j == pl.num_programs(1) - 1)
    def _():
        o_ref[...] = x_ref[...] + gate_ref[...] * acc_ref[...]


def _ffn(x2, g, scale, shift, w1, w3, w2, gate, seq):
    n, d = x2.shape
    dff = w1.shape[1]
    tm, tf = min(FFN_TM, seq), FFN_TF
    per_b = seq // tm
    vec = pl.BlockSpec((None, 1, d), lambda i, j: (i // per_b, 0, 0))
    return pl.pallas_call(
        _ffn_kernel,
        grid=(n // tm, dff // tf),
        in_specs=[
            pl.BlockSpec((tm, d), lambda i, j: (i, 0)),
            pl.BlockSpec((1, d), lambda i, j: (0, 0)),
            vec, vec,
            pl.BlockSpec((d, tf), lambda i, j: (0, j)),
            pl.BlockSpec((d, tf), lambda i, j: (0, j)),
            pl.BlockSpec((tf, d), lambda i, j: (j, 0)),
            vec,
        ],
        out_specs=pl.BlockSpec((tm, d), lambda i, j: (i, 0)),
        out_shape=jax.ShapeDtypeStruct((n, d), F32),
        scratch_shapes=[pltpu.VMEM((tm, d), BF16), pltpu.VMEM((tm, d), F32)],
        compiler_params=pltpu.CompilerParams(
            dimension_semantics=("arbitrary", "arbitrary"), vmem_limit_bytes=VMEM_LIMIT),
        name="ffn_swiglu",
    )(x2, g, scale, shift, w1, w3, w2, gate)


MOE_TM = 512
ROUTE_TM = 1024


def _router_kernel(x_ref, g_ref, sc_ref, sh_ref, wr_ref, hn_ref, rt_ref):
    lane = lax.broadcasted_iota(jnp.int32, (1, LANES), 1)
    lane_f = lane.astype(F32)
    h = _modulate(x_ref[...], g_ref[...], sc_ref[...], sh_ref[...])
    hn_ref[...] = h
    logits = jnp.dot(h, wr_ref[...], preferred_element_type=F32, precision=lax.Precision.HIGHEST)
    logits = jnp.where(lane < N_EXPERTS, logits, -jnp.inf)
    m1 = jnp.max(logits, axis=-1, keepdims=True)
    i1 = jnp.min(jnp.where(logits == m1, lane_f, float(LANES)), axis=-1, keepdims=True)
    rest = jnp.where(lane_f == i1, -jnp.inf, logits)
    m2 = jnp.max(rest, axis=-1, keepdims=True)
    i2 = jnp.min(jnp.where(rest == m2, lane_f, float(LANES)), axis=-1, keepdims=True)
    e2 = jnp.exp(m2 - m1)
    p1 = 1.0 / (1.0 + e2)
    p2 = e2 / (1.0 + e2)
    rt_ref[...] = (jnp.where(lane == 0, i1, 0.0) + jnp.where(lane == 1, i2, 0.0)
                   + jnp.where(lane == 2, p1, 0.0) + jnp.where(lane == 3, p2, 0.0))


def _router(x2, g, scale, shift, w_router, seq):
    n, d = x2.shape
    tm = min(ROUTE_TM, seq)
    per_b = seq // tm
    vec = pl.BlockSpec((None, 1, d), lambda i: (i // per_b, 0, 0))
    return pl.pallas_call(
        _router_kernel,
        grid=(n // tm,),
        in_specs=[
            pl.BlockSpec((tm, d), lambda i: (i, 0)),
            pl.BlockSpec((1, d), lambda i: (0, 0)),
            vec, vec,
            pl.BlockSpec((d, LANES), lambda i: (0, 0)),
        ],
        out_specs=[pl.BlockSpec((tm, d), lambda i: (i, 0)),
                   pl.BlockSpec((tm, LANES), lambda i: (i, 0))],
        out_shape=[jax.ShapeDtypeStruct((n, d), F32), jax.ShapeDtypeStruct((n, LANES), F32)],
        compiler_params=pltpu.CompilerParams(
            dimension_semantics=("arbitrary",), vmem_limit_bytes=VMEM_LIMIT),
        name="moe_router",
    )(x2, g, scale, shift, w_router)


def _route_tables(rt, n, n_exp, tm):
    rows = 2 * n + n_exp * tm
    n_tiles = rows // tm
    ef = rt[:, :2].astype(jnp.int32).T.reshape(-1)
    onehot = (ef[:, None] == jnp.arange(n_exp, dtype=jnp.int32)[None, :]).astype(jnp.int32)
    csum = jnp.cumsum(onehot, axis=0)
    rank = jnp.take_along_axis(csum, ef[:, None], axis=1)[:, 0] - 1
    cnt = csum[-1]
    padded = ((cnt + tm - 1) // tm) * tm
    ends = jnp.cumsum(padded)
    dest = (ends - padded)[ef] + rank
    total = ends[-1]
    tile_start = jnp.arange(n_tiles, dtype=jnp.int32) * tm
    tile_valid = (tile_start < total).astype(jnp.int32)
    tile_expert = jnp.sum((tile_start[:, None] >= ends[None, :]).astype(jnp.int32), axis=1)
    last_tile = jnp.maximum(total // tm - 1, 0)
    tile_expert = jnp.where(tile_valid == 1, tile_expert, tile_expert[last_tile])
    tile_rows = jnp.where(tile_valid == 1, jnp.arange(n_tiles, dtype=jnp.int32), last_tile)
    spare_lo = jnp.concatenate([ends - padded + cnt, total[None]]).astype(jnp.int32)
    spare_hi = jnp.concatenate([ends, jnp.full((1,), rows, ends.dtype)]).astype(jnp.int32)
    return dict(tile_expert=tile_expert, tile_valid=tile_valid, tile_rows=tile_rows, dest=dest,
                spare_lo=spare_lo, spare_hi=spare_hi, rows=rows)


def _row_copies(copy, count):
    def body(r, carry):
        copy(r).start()
        return carry
    lax.fori_loop(0, count, body, 0, unroll=8)


def _dispatch_kernel(dest_ref, lo_ref, hi_ref, hn_ref, xs_hbm, zero_ref, sem, zsem, *, n, n_spare):
    i = pl.program_id(0)
    tm = hn_ref.shape[0]

    @pl.when(i == 0)
    def _():
        zero_ref[...] = jnp.zeros(zero_ref.shape, F32)
        for e in range(n_spare):
            def zero_row(r):
                return pltpu.make_async_copy(zero_ref.at[pl.ds(0, 1)], xs_hbm.at[pl.ds(r, 1)], zsem)

            def start(r, carry):
                zero_row(r).start()
                return carry

            def drain(r, carry):
                zero_row(r).wait()
                return carry
            lax.fori_loop(lo_ref[e], hi_ref[e], start, 0)
            lax.fori_loop(lo_ref[e], hi_ref[e], drain, 0)

    for k in range(2):
        _row_copies(lambda r, k=k: pltpu.make_async_copy(
            hn_ref.at[pl.ds(r, 1)], xs_hbm.at[pl.ds(dest_ref[k * n + i * tm + r], 1)], sem), tm)
    for k in range(2):
        pltpu.make_async_copy(hn_ref, xs_hbm.at[pl.ds(0, tm)], sem).wait()


def _dispatch(hn, tables):
    n, d = hn.shape
    tm = MOE_TM
    kern = functools.partial(_dispatch_kernel, n=n, n_spare=tables["spare_lo"].shape[0])
    grid_spec = pltpu.PrefetchScalarGridSpec(
        num_scalar_prefetch=3,
        grid=(n // tm,),
        in_specs=[pl.BlockSpec((tm, d), lambda i, dest, lo, hi: (i, 0))],
        out_specs=pl.BlockSpec(memory_space=pl.ANY),
        scratch_shapes=[pltpu.VMEM((8, d), F32), pltpu.SemaphoreType.DMA(()),
                        pltpu.SemaphoreType.DMA(())],
    )
    return pl.pallas_call(
        kern,
        grid_spec=grid_spec,
        out_shape=jax.ShapeDtypeStruct((tables["rows"], d), F32),
        compiler_params=pltpu.CompilerParams(
            dimension_semantics=("arbitrary",), vmem_limit_bytes=VMEM_LIMIT),
        name="moe_dispatch",
    )(tables["dest"], tables["spare_lo"], tables["spare_hi"], hn)


def _moe_grouped_kernel(te_ref, tv_ref, tr_ref, x_ref, w1_ref, w3_ref, w2_ref, y_ref, xb_ref):
    t = pl.program_id(0)
    j = pl.program_id(1)
    valid = tv_ref[t] == 1

    @pl.when(j == 0)
    def _():
        xb_ref[...] = x_ref[...].astype(BF16)
        y_ref[...] = jnp.zeros(y_ref.shape, F32)

    @pl.when(valid)
    def _():
        act, w2 = _swiglu_step(xb_ref[...], w1_ref[...], w3_ref[...], w2_ref[...])
        y_ref[...] += jnp.dot(act.astype(BF16), w2, preferred_element_type=F32)


def _moe_grouped(xs, tables, w1, w3, w2):
    rows, d = xs.shape
    dff = w1.shape[2]
    tm, tf = MOE_TM, FFN_TF
    nj = dff // tf

    def jj(t, j, tv):
        return j * tv[t] + (nj - 1) * (1 - tv[t])

    grid_spec = pltpu.PrefetchScalarGridSpec(
        num_scalar_prefetch=3,
        grid=(rows // tm, nj),
        in_specs=[
            pl.BlockSpec((tm, d), lambda t, j, te, tv, tr: (tr[t], 0)),
            pl.BlockSpec((None, d, tf), lambda t, j, te, tv, tr: (te[t], 0, jj(t, j, tv))),
            pl.BlockSpec((None, d, tf), lambda t, j, te, tv, tr: (te[t], 0, jj(t, j, tv))),
            pl.BlockSpec((None, tf, d), lambda t, j, te, tv, tr: (te[t], jj(t, j, tv), 0)),
        ],
        out_specs=pl.BlockSpec((tm, d), lambda t, j, te, tv, tr: (t, 0)),
        scratch_shapes=[pltpu.VMEM((tm, d), BF16)],
    )
    return pl.pallas_call(
        _moe_grouped_kernel,
        grid_spec=grid_spec,
        out_shape=jax.ShapeDtypeStruct((rows, d), F32),
        compiler_params=pltpu.CompilerParams(
            dimension_semantics=("arbitrary", "arbitrary"), vmem_limit_bytes=VMEM_LIMIT),
        name="moe_grouped",
    )(tables["tile_expert"], tables["tile_valid"], tables["tile_rows"], xs, w1, w3, w2)


def _moe_combine_kernel(dest_ref, x_ref, rt_ref, gate_ref, y_hbm, o_ref, y0_ref, y1_ref, sem, *, n):
    i = pl.program_id(0)
    tm = x_ref.shape[0]
    for k, yk_ref in enumerate((y0_ref, y1_ref)):
        _row_copies(lambda r, k=k, yk_ref=yk_ref: pltpu.make_async_copy(
            y_hbm.at[pl.ds(dest_ref[k * n + i * tm + r], 1)], yk_ref.at[pl.ds(r, 1)], sem), tm)
    for yk_ref in (y0_ref, y1_ref):
        pltpu.make_async_copy(y_hbm.at[pl.ds(0, tm)], yk_ref, sem).wait()
    p1 = rt_ref[:, 2:3]
    p2 = rt_ref[:, 3:4]
    o_ref[...] = x_ref[...] + gate_ref[...] * (p1 * y0_ref[...] + p2 * y1_ref[...])


def _moe_combine(x2, y, rt, gate, dest, seq):
    n, d = x2.shape
    tm = min(FFN_TM, seq)
    per_b = seq // tm
    kern = functools.partial(_moe_combine_kernel, n=n)
    grid_spec = pltpu.PrefetchScalarGridSpec(
        num_scalar_prefetch=1,
        grid=(n // tm,),
        in_specs=[
            pl.BlockSpec((tm, d), lambda i, dest: (i, 0)),
            pl.BlockSpec((tm, LANES), lambda i, dest: (i, 0)),
            pl.BlockSpec((None, 1, d), lambda i, dest: (i // per_b, 0, 0)),
            pl.BlockSpec(memory_space=pl.ANY),
        ],
        out_specs=pl.BlockSpec((tm, d), lambda i, dest: (i, 0)),
        scratch_shapes=[pltpu.VMEM((tm, d), F32), pltpu.VMEM((tm, d), F32),
                        pltpu.SemaphoreType.DMA(())],
    )
    return pl.pallas_call(
        kern,
        grid_spec=grid_spec,
        out_shape=jax.ShapeDtypeStruct((n, d), F32),
        compiler_params=pltpu.CompilerParams(
            dimension_semantics=("arbitrary",), vmem_limit_bytes=VMEM_LIMIT),
        name="moe_combine",
    )(dest, x2, rt, gate, y)


def _moe(x2, g, scale, shift, w_router, w1, w3, w2, gate, seq):
    n = x2.shape[0]
    hn, rt = _router(x2, g, scale, shift, w_router, seq)
    tables = _route_tables(rt, n, w1.shape[0], MOE_TM)
    xs = _dispatch(hn, tables)
    y = _moe_grouped(xs, tables, w1, w3, w2)
    return _moe_combine(x2, y, rt, gate, tables["dest"], seq)


def _prep_in_proj(w_in, q_norm, k_norm, idx_k_norm, d):
    sizes = dict(aq=ATT_HEADS * HEAD_DIM, ak=ATT_KV_HEADS * HEAD_DIM, av=ATT_KV_HEADS * HEAD_DIM,
                 iq=IDX_HEADS * HEAD_DIM, ik=HEAD_DIM, iw=IDX_HEADS,
                 gq=d // 2, gk=d // 2, gv=d, glr=GLA_GATE_RANK, gr=d, m_att=d, m_gla=d)
    src, acc = {}, 0
    for name in ("aq", "ak", "av", "iq", "ik", "iw", "gq", "gk", "gv", "glr", "gr", "m_att", "m_gla"):
        src[name] = (acc, sizes[name])
        acc += sizes[name]
    order = ("aq", "iq", "gv", "gr", "m_att", "m_gla", "ak", "av", "gq", "gk")
    cols, parts, off = {}, [], 0
    for name in order:
        s, width = src[name]
        cols[name] = off
        parts.append(w_in[:, s:s + width])
        off += width
    w_main = jnp.concatenate(parts, axis=1).astype(BF16)

    def seg(name):
        s, width = src[name]
        return w_in[:, s:s + width]

    pad = jnp.zeros((d, LANES - IDX_HEADS - GLA_GATE_RANK), w_in.dtype)
    w_small = jnp.concatenate([seg("ik"), seg("iw"), seg("glr"), pad], axis=1).astype(BF16)

    att_scale = HEAD_DIM ** -0.5 * LOG2_E
    idx_scale = (HEAD_DIM ** -0.5) * (IDX_HEADS ** -0.5)
    cg = jnp.ones((off,), F32)
    cg = cg.at[cols["aq"]:cols["aq"] + sizes["aq"]].set(jnp.tile(q_norm * att_scale, ATT_HEADS))
    cg = cg.at[cols["ak"]:cols["ak"] + sizes["ak"]].set(jnp.tile(k_norm, ATT_KV_HEADS))
    cg2 = jnp.concatenate([idx_k_norm, jnp.full((IDX_HEADS,), idx_scale, F32),
                           jnp.ones((LANES - IDX_HEADS,), F32)])
    norm_tiles = tuple(range(cols["aq"] // PROJ_SEG, (cols["aq"] + sizes["aq"]) // PROJ_SEG)) + \
        tuple(range(cols["ak"] // PROJ_SEG, (cols["ak"] + sizes["ak"]) // PROJ_SEG))
    sig_tiles = (cols["m_att"] // PROJ_SEG, (cols["m_gla"] + sizes["m_gla"]) // PROJ_SEG)
    return w_main, w_small, cg.reshape(1, -1), cg2.reshape(1, -1), cols, norm_tiles, sig_tiles


def kernel(x, c, ada_w, ada_b, norm_mix, norm_ffn, w_in, q_norm, k_norm, idx_k_norm, w_gla_gate,
           b_gla_gate, gla_out_norm, w_out_attn, w_out_gla, w_out, ffn_w1, ffn_w3, ffn_w2,
           moe_router, moe_w1, moe_w3, moe_w2):
    batch, seq, d = x.shape
    depth = ada_w.shape[0]
    n = batch * seq
    x2 = x.reshape(n, d)

    c8 = jnp.zeros((8, d), F32).at[:batch].set(c)
    mod_all = _ada(c8, ada_w, ada_b)

    for layer in range(depth):
        mod = mod_all[layer, :batch].reshape(batch, 6, 1, d)
        shift_m, scale_m, gate_m = mod[:, 0], mod[:, 1], mod[:, 2]
        shift_f, scale_f, gate_f = mod[:, 3], mod[:, 4], mod[:, 5]

        w_main, w_small, cg, cg2, cols, norm_tiles, sig_tiles = _prep_in_proj(
            w_in[layer], q_norm[layer], k_norm[layer], idx_k_norm[layer], d)
        proj, ik, sm = _modproj(x2, norm_mix[layer].reshape(1, d), scale_m, shift_m,
                                w_main, cg, w_small, cg2, seq, norm_tiles, sig_tiles)
        y_att = _attention(proj, ik, sm, batch, seq, cols)
        wg = jnp.zeros((LANES, w_gla_gate.shape[2]), F32).at[
            GLA_GATE_RANK:2 * GLA_GATE_RANK].set(w_gla_gate[layer])
        y_gla = _gla(proj, sm, wg, b_gla_gate[layer].reshape(1, -1),
                     gla_out_norm[layer].reshape(1, -1), batch, seq, cols)
        merged = _merge(y_att, y_gla, w_out_attn[layer].astype(BF16),
                        w_out_gla[layer].astype(BF16), proj, cols)
        x2 = _resproj(merged, w_out[layer].astype(BF16), x2, gate_m, seq)

        g_f = norm_ffn[layer].reshape(1, d)
        i = layer // 2
        if layer % 2 == 0:
            x2 = _ffn(x2, g_f, scale_f, shift_f, ffn_w1[i].astype(BF16), ffn_w3[i].astype(BF16),
                      ffn_w2[i].astype(BF16), gate_f, seq)
        else:
            w_r = jnp.zeros((d, LANES), F32).at[:, :N_EXPERTS].set(moe_router[i])
            x2 = _moe(x2, g_f, scale_f, shift_f, w_r, moe_w1[i].astype(BF16),
                      moe_w3[i].astype(BF16), moe_w2[i].astype(BF16), gate_f, seq)
    return x2.reshape(batch, seq, d)
```

```python
import functools

import jax
import jax.numpy as jnp
from jax import lax
from jax.experimental import pallas as pl
from jax.experimental.pallas import tpu as pltpu

F32 = jnp.float32
BF16 = jnp.bfloat16

ATT_HEADS = 16
ATT_KV_HEADS = 4
HEAD_DIM = 128
IDX_HEADS = 16
TOPK_MAX = 256
GLA_HEADS = 4
GLA_GATE_RANK = 16
GLA_GATE_TAU = 16.0
GLA_CHUNK = 64
N_EXPERTS = 8
EPS = 1e-6
LANES = 128
NEG_BIG = -1e30
INT_MIN = -(2 ** 31)
LOG2_E = 1.4426950408889634

PROJ_SEG = 512
PROJ_TN = 1024
VMEM_LIMIT = 56 * 1024 * 1024


def _nt_dot(a, b):
    return lax.dot_general(a, b, (((1,), (1,)), ((), ())), preferred_element_type=F32)


def _rms(a):
    return a * lax.rsqrt(jnp.mean(a * a, axis=-1, keepdims=True) + EPS)


def _sigmoid(a):
    return 1.0 / (1.0 + jnp.exp(-a))


def _modulate(x, g, scale, shift):
    return _rms(x) * g * (1.0 + scale) + shift


def _ada_kernel(ct_ref, w_ref, b_ref, o_ref, *, batch):
    ct = ct_ref[...]
    cond = ct * _sigmoid(ct)
    w = w_ref[...]
    o_ref[...] = jnp.zeros(o_ref.shape, F32)
    for b in range(batch):
        o_ref[b:b + 1, :] = jnp.sum(w * cond[:, b:b + 1], axis=0, keepdims=True) + b_ref[...]


def _ada(c, ada_w, ada_b):
    depth, d, n = ada_w.shape
    batch = c.shape[0]
    assert batch <= 8
    tn = 1024
    ct = jnp.zeros((d, LANES), F32).at[:, :batch].set(c.T)
    return pl.pallas_call(
        functools.partial(_ada_kernel, batch=batch),
        grid=(depth, n // tn),
        in_specs=[
            pl.BlockSpec((d, LANES), lambda l, j: (0, 0)),
            pl.BlockSpec((None, d, tn), lambda l, j: (l, 0, j)),
            pl.BlockSpec((None, 1, tn), lambda l, j: (l, 0, j)),
        ],
        out_specs=pl.BlockSpec((None, 8, tn), lambda l, j: (l, 0, j)),
        out_shape=jax.ShapeDtypeStruct((depth, 8, n), F32),
        name="ada_mod",
    )(ct, ada_w, ada_b.reshape(depth, 1, n))


def _modproj_kernel(x_ref, g_ref, sc_ref, sh_ref, w_ref, cg_ref, w2_ref, cg2_ref,
                    o_ref, ik_ref, sm_ref, hn_ref, *, norm_tiles):
    j = pl.program_id(1)

    @pl.when(j == 0)
    def _():
        h = _modulate(x_ref[...], g_ref[...], sc_ref[...], sh_ref[...]).astype(BF16)
        hn_ref[...] = h
        small = jnp.dot(h, w2_ref[...], preferred_element_type=F32)
        ik_ref[...] = (_rms(small[:, :LANES]) * cg2_ref[:, :LANES]).astype(BF16)
        sm_ref[...] = small[:, LANES:] * cg2_ref[:, LANES:]

    acc = jnp.dot(hn_ref[...], w_ref[...], preferred_element_type=F32)

    for part in range(PROJ_TN // PROJ_SEG):
        seg = j * (PROJ_TN // PROJ_SEG) + part
        cols = slice(part * PROJ_SEG, (part + 1) * PROJ_SEG)
        is_norm = functools.reduce(jnp.logical_or, [seg == t for t in norm_tiles])

        @pl.when(is_norm)
        def _(cols=cols):
            for c in range(cols.start, cols.stop, LANES):
                sl = slice(c, c + LANES)
                o_ref[:, sl] = (_rms(acc[:, sl]) * cg_ref[:, sl]).astype(BF16)

        @pl.when(jnp.logical_not(is_norm))
        def _(cols=cols):
            o_ref[:, cols] = acc[:, cols].astype(BF16)


def _modproj(x2, g, scale, shift, w, cg, w2, cg2, seq, norm_tiles):
    n, d = x2.shape
    ncols = w.shape[1]
    tm = min(1024, seq)
    per_b = seq // tm
    kern = functools.partial(_modproj_kernel, norm_tiles=norm_tiles)
    return pl.pallas_call(
        kern,
        grid=(n // tm, ncols // PROJ_TN),
        in_specs=[
            pl.BlockSpec((tm, d), lambda i, j: (i, 0)),
            pl.BlockSpec((1, d), lambda i, j: (0, 0)),
            pl.BlockSpec((None, 1, d), lambda i, j: (i // per_b, 0, 0)),
            pl.BlockSpec((None, 1, d), lambda i, j: (i // per_b, 0, 0)),
            pl.BlockSpec((d, PROJ_TN), lambda i, j: (0, j)),
            pl.BlockSpec((1, PROJ_TN), lambda i, j: (0, j)),
            pl.BlockSpec((d, 2 * LANES), lambda i, j: (0, 0)),
            pl.BlockSpec((1, 2 * LANES), lambda i, j: (0, 0)),
        ],
        out_specs=[
            pl.BlockSpec((tm, PROJ_TN), lambda i, j: (i, j)),
            pl.BlockSpec((tm, LANES), lambda i, j: (i, 0)),
            pl.BlockSpec((tm, LANES), lambda i, j: (i, 0)),
        ],
        out_shape=[
            jax.ShapeDtypeStruct((n, ncols), BF16),
            jax.ShapeDtypeStruct((n, LANES), BF16),
            jax.ShapeDtypeStruct((n, LANES), F32),
        ],
        scratch_shapes=[pltpu.VMEM((tm, d), BF16)],
        compiler_params=pltpu.CompilerParams(
            dimension_semantics=("arbitrary", "arbitrary"), vmem_limit_bytes=VMEM_LIMIT),
        name="modproj",
    )(x2, g, scale, shift, w, cg, w2, cg2)


ATT_TQ = 128
ATT_TK = 256
BISECT_UNCHECKED_BITS = 19
ATT_SPLIT = 4


def _sortable(a):
    a = jnp.where(a == 0.0, 0.0, a)
    bits = pltpu.bitcast(a, jnp.int32)
    return bits ^ ((bits >> 31) & 0x7FFFFFFF)


def _attn_kernel(aq_ref, iq_ref, ak_ref, av_ref, ik_ref, iw_ref, o_ref,
                 sc_ref, sct_ref, qs_ref, wb_ref, m_ref, acc_ref, *, topk, seq):
    tq, tk = ATT_TQ, ATT_TK
    group = ATT_HEADS // ATT_KV_HEADS
    qi = pl.program_id(1)
    n_kt = (qi * tq + tq + tk - 1) // tk
    row = qi * tq + lax.broadcasted_iota(jnp.int32, (tq, 1), 0)
    lane_col = lax.broadcasted_iota(jnp.int32, (1, tk), 1)
    for h in range(IDX_HEADS):
        wb_ref[h] = jnp.broadcast_to(iw_ref[:, h:h + 1], (tq, LANES))
    for h in range(ATT_HEADS):
        qs_ref[h * tq:(h + 1) * tq, :] = aq_ref[:, h * HEAD_DIM:(h + 1) * HEAD_DIM]

    n_pair = (n_kt + 1) // 2

    def score_pair(i, carry):
        kblk = ik_ref[pl.ds(pl.multiple_of(i * 2 * tk, 2 * tk), 2 * tk), :]
        acc = jnp.zeros((tq, 2 * tk), F32)
        for h in range(IDX_HEADS):
            z = _nt_dot(iq_ref[:, h * HEAD_DIM:(h + 1) * HEAD_DIM], kblk)
            wb = wb_ref[h]
            acc = acc + jnp.concatenate([wb] * (2 * tk // LANES), axis=1) * jnp.maximum(z, 0.0)
        keys = []
        for u in range(2):
            kt = 2 * i + u
            causal = (kt * tk + lane_col) <= row
            keys.append(jnp.where(causal, _sortable(acc[:, u * tk:(u + 1) * tk]), INT_MIN))
            sc_ref[kt] = keys[u]
        sct_ref[i] = jnp.concatenate(keys, axis=1).T
        return carry

    lax.fori_loop(0, n_pair, score_pair, 0)

    def count_where(pred):
        def body(i, c):
            for u in range(2):
                kt = 2 * i + u
                hit = jnp.where(pred(sc_ref[kt], kt), 1.0, 0.0)
                for cc in range(tk // LANES):
                    c = c + hit[:, cc * LANES:(cc + 1) * LANES]
            return c
        c = lax.fori_loop(0, n_pair, body, jnp.zeros((tq, LANES), F32))
        return jnp.sum(c, axis=-1, keepdims=True)

    kf = float(topk)

    def count_ge_t(t):
        def body(i, c):
            hit = jnp.where(sct_ref[i] >= t, 1.0, 0.0)
            return c + jnp.sum(hit.reshape(2 * tk // 64, 8, 8, tq), axis=0)
        c = lax.fori_loop(0, n_pair, body, jnp.zeros((8, 8, tq), F32))
        return jnp.sum(jnp.sum(c, axis=0), axis=0, keepdims=True)

    c0 = count_ge_t(jnp.zeros((1, tq), jnp.int32))
    t0 = jnp.where(c0 >= kf, 0, INT_MIN).astype(jnp.int32)
    n_all = jnp.full((1, tq), float(seq), F32)

    def bit_step(i, state):
        t, ct, open_rows = state
        cand = t | jnp.left_shift(jnp.int32(1), 30 - i)
        c = count_ge_t(cand)
        take = jnp.logical_and(c >= kf, open_rows > 0.0)
        t = jnp.where(take, cand, t)
        ct = jnp.where(take, c, ct)
        open_rows = jnp.where(c == kf, 0.0, open_rows)
        return t, ct, open_rows

    def any_open(open_rows):
        return (jnp.max(open_rows) > 0.0).astype(jnp.int32)

    def tail_cond(state):
        i, _, go = state
        return jnp.logical_and(i < 31, go == 1)

    def tail_step(state):
        i, inner, _ = state
        inner = bit_step(i + 1, bit_step(i, inner))
        return i + 2, inner, any_open(inner[2])

    state = (t0, jnp.where(c0 >= kf, c0, n_all), jnp.where(c0 == kf, 0.0, 1.0))
    state = lax.fori_loop(0, BISECT_UNCHECKED_BITS, bit_step, state)
    _, (thr_t, cnt_t, open_t), _ = lax.while_loop(
        tail_cond, tail_step, (jnp.int32(BISECT_UNCHECKED_BITS), state, any_open(state[2])))

    eye = lax.broadcasted_iota(jnp.int32, (tq, tq), 0) == lax.broadcasted_iota(jnp.int32, (tq, tq), 1)

    def to_col(v):
        return jnp.sum(jnp.where(eye, v, 0.0), axis=1, keepdims=True)

    thr = ((to_col((thr_t >> 16).astype(F32)).astype(jnp.int32) << 16)
           | to_col((thr_t & 0xFFFF).astype(F32)).astype(jnp.int32))
    cnt_thr = to_col(cnt_t)
    open_rows = to_col(open_t)

    excess = jnp.logical_and(jnp.logical_and(open_rows > 0.0, cnt_thr > kf), thr > INT_MIN)

    @pl.when(jnp.max(jnp.where(excess, 1.0, 0.0)) > 0.0)
    def _():
        need = kf - count_where(lambda key, kt: key > thr)

        def ties_below(j):
            return count_where(
                lambda key, kt: jnp.logical_and(key == thr, (kt * tk + lane_col) < j))

        nbits = max(seq - 1, 1).bit_length()

        def idx_step(i, j0):
            cand = j0 | jnp.left_shift(jnp.int32(1), nbits - 1 - i)
            return jnp.where(ties_below(cand) < need, cand, j0)

        j0 = lax.fori_loop(0, nbits, idx_step, jnp.zeros((tq, 1), jnp.int32))

        def demote(kt, carry):
            key = sc_ref[kt]
            late_tie = jnp.logical_and(key == thr, (kt * tk + lane_col) > j0)
            sc_ref[kt] = jnp.where(jnp.logical_and(excess, late_tie), INT_MIN, key)
            return carry

        lax.fori_loop(0, n_kt, demote, 0)

    thr = jnp.maximum(thr, INT_MIN + 1)

    m_ref[...] = jnp.full(m_ref.shape, NEG_BIG, F32)
    acc_ref[...] = jnp.zeros(acc_ref.shape, F32)
    tk2 = 2 * tk
    ones = jnp.ones((tk2, HEAD_DIM), BF16)

    def kv_pair(i, carry):
        r0 = pl.multiple_of(i * tk2, tk2)
        keep = jnp.concatenate([sc_ref[2 * i] >= thr, sc_ref[2 * i + 1] >= thr], axis=1)[None]
        for g in range(ATT_KV_HEADS):
            kblk = ak_ref[pl.ds(r0, tk2), g * HEAD_DIM:(g + 1) * HEAD_DIM]
            vblk = av_ref[pl.ds(r0, tk2), g * HEAD_DIM:(g + 1) * HEAD_DIM]
            v1 = jnp.concatenate([vblk, ones], axis=1)
            for part in range(ATT_SPLIT):
                nh = group // ATT_SPLIT
                rows = slice(part * nh * tq, (part + 1) * nh * tq)
                q4 = qs_ref[(g * group + part * nh) * tq:(g * group + (part + 1) * nh) * tq, :]
                s = _nt_dot(q4, kblk).reshape(nh, tq, tk2)
                s = jnp.where(keep, s, NEG_BIG).reshape(nh * tq, tk2)
                m_old = m_ref[g, rows, :]
                m_new = jnp.maximum(m_old, jnp.max(s, axis=-1, keepdims=True))
                alpha = jnp.exp2(m_old - m_new)
                p = jnp.exp2(s - jnp.concatenate([m_new] * (tk2 // LANES), axis=1))
                pv = jnp.dot(p.astype(BF16), v1, preferred_element_type=F32)
                acc_ref[g, rows, :] = jnp.concatenate([alpha, alpha], axis=1) * acc_ref[g, rows, :] + pv
                m_ref[g, rows, :] = m_new
        return carry

    lax.fori_loop(0, n_pair, kv_pair, 0)

    for g in range(ATT_KV_HEADS):
        og = acc_ref[g]
        og = og[:, :HEAD_DIM] / og[:, HEAD_DIM:]
        for r in range(group):
            h = g * group + r
            o_ref[:, h * HEAD_DIM:(h + 1) * HEAD_DIM] = og[r * tq:(r + 1) * tq, :].astype(BF16)


def _attention(proj, ik, sm, batch, seq, cols):
    n = proj.shape[0]
    tq, tk = ATT_TQ, ATT_TK
    nq = seq // tq
    width = ATT_HEADS * HEAD_DIM
    kvw = ATT_KV_HEADS * HEAD_DIM
    topk = min(TOPK_MAX, seq // 4)
    group = ATT_HEADS // ATT_KV_HEADS
    kern = functools.partial(_attn_kernel, topk=topk, seq=seq)
    once = pl.Buffered(1)
    return pl.pallas_call(
        kern,
        grid=(batch, nq),
        in_specs=[
            pl.BlockSpec((tq, width), lambda b, q: (b * nq + q, cols["aq"] // width)),
            pl.BlockSpec((tq, width), lambda b, q: (b * nq + q, cols["iq"] // width)),
            pl.BlockSpec((seq, kvw), lambda b, q: (b, cols["ak"] // kvw), pipeline_mode=once),
            pl.BlockSpec((seq, kvw), lambda b, q: (b, cols["av"] // kvw), pipeline_mode=once),
            pl.BlockSpec((seq, LANES), lambda b, q: (b, 0), pipeline_mode=once),
            pl.BlockSpec((tq, LANES), lambda b, q: (b * nq + q, 0)),
        ],
        out_specs=pl.BlockSpec((tq, width), lambda b, q: (b * nq + q, 0)),
        out_shape=jax.ShapeDtypeStruct((n, width), BF16),
        scratch_shapes=[
            pltpu.VMEM((seq // tk, tq, tk), jnp.int32),
            pltpu.VMEM((seq // (2 * tk), 2 * tk, tq), jnp.int32),
            pltpu.VMEM((ATT_HEADS * tq, HEAD_DIM), BF16),
            pltpu.VMEM((IDX_HEADS, tq, LANES), F32),
            pltpu.VMEM((ATT_KV_HEADS, group * tq, LANES), F32),
            pltpu.VMEM((ATT_KV_HEADS, group * tq, 2 * HEAD_DIM), F32),
        ],
        compiler_params=pltpu.CompilerParams(
            dimension_semantics=("arbitrary", "arbitrary"), vmem_limit_bytes=VMEM_LIMIT),
        name="dsa_attention",
    )(proj, proj, proj, proj, ik, sm)


GLA_T = 256


def _gla_kernel(q_ref, k_ref, v_ref, r_ref, lr_ref, wg_ref, bg_ref, gn_ref, o_ref, st_ref, la_ref,
                *, dk, dv):
    c_len = GLA_CHUNK
    hi = lax.Precision.HIGHEST

    @pl.when(pl.program_id(1) == 0)
    def _():
        st_ref[...] = jnp.zeros(st_ref.shape, F32)

    ri = lax.broadcasted_iota(jnp.int32, (c_len, c_len), 0)
    ci = lax.broadcasted_iota(jnp.int32, (c_len, c_len), 1)
    lower = ri >= ci
    tril = jnp.where(lower, 1.0, 0.0).astype(F32)
    qscale = dk ** -0.5

    gl = jnp.dot(lr_ref[...], wg_ref[...], preferred_element_type=F32, precision=hi) + bg_ref[...]
    la_ref[...] = (jnp.minimum(gl, 0.0) - jnp.log(1.0 + jnp.exp(-jnp.abs(gl)))) / GLA_GATE_TAU

    for c in range(GLA_T // c_len):
        rows = slice(c * c_len, (c + 1) * c_len)
        b = jnp.dot(tril, la_ref[rows, :], preferred_element_type=F32, precision=hi)
        b_last = b[c_len - 1:c_len, :]
        k = k_ref[rows, :].astype(F32)
        q_dec = (q_ref[rows, :].astype(F32) * qscale * jnp.exp(b)).astype(BF16)
        k_inv = (k * jnp.exp(-b)).astype(BF16)
        k_tail = k * jnp.exp(b_last - b)
        for h in range(GLA_HEADS):
            ks = slice(h * dk, (h + 1) * dk)
            vs = slice(h * dv, (h + 1) * dv)
            v = v_ref[rows, vs]
            a = jnp.where(lower, _nt_dot(q_dec[:, ks], k_inv[:, ks]), 0.0)
            state = st_ref[h]
            o = (jnp.dot(a.astype(BF16), v, preferred_element_type=F32)
                 + jnp.dot(q_dec[:, ks], state.astype(BF16), preferred_element_type=F32))
            dec = jnp.exp(b[:, ks].T[:, c_len - 1:c_len])
            upd = jnp.dot(k_tail[:, ks].T.astype(BF16), v, preferred_element_type=F32)
            for cc in range(dv // LANES):
                sl = slice(cc * LANES, (cc + 1) * LANES)
                st_ref[h, :, sl] = dec * state[:, sl] + upd[:, sl]
            r = r_ref[rows, vs].astype(F32)
            o_ref[rows, vs] = (_rms(o) * gn_ref[...] * (r * _sigmoid(r))).astype(BF16)


def _gla(proj, sm, wg, bg, gn, batch, seq, cols):
    n = proj.shape[0]
    dk = wg.shape[1] // GLA_HEADS
    dv = gn.shape[1]
    t = min(GLA_T, seq)
    assert t == GLA_T
    ns = seq // t
    kw, vw = GLA_HEADS * dk, GLA_HEADS * dv
    kern = functools.partial(_gla_kernel, dk=dk, dv=dv)
    return pl.pallas_call(
        kern,
        grid=(batch, ns),
        in_specs=[
            pl.BlockSpec((t, kw), lambda b, s: (b * ns + s, cols["gq"] // kw)),
            pl.BlockSpec((t, kw), lambda b, s: (b * ns + s, cols["gk"] // kw)),
            pl.BlockSpec((t, vw), lambda b, s: (b * ns + s, cols["gv"] // vw)),
            pl.BlockSpec((t, vw), lambda b, s: (b * ns + s, cols["gr"] // vw)),
            pl.BlockSpec((t, LANES), lambda b, s: (b * ns + s, 0)),
            pl.BlockSpec((LANES, kw), lambda b, s: (0, 0)),
            pl.BlockSpec((1, kw), lambda b, s: (0, 0)),
            pl.BlockSpec((1, dv), lambda b, s: (0, 0)),
        ],
        out_specs=pl.BlockSpec((t, vw), lambda b, s: (b * ns + s, 0)),
        out_shape=jax.ShapeDtypeStruct((n, vw), BF16),
        scratch_shapes=[pltpu.VMEM((GLA_HEADS, dk, dv), F32), pltpu.VMEM((t, kw), F32)],
        compiler_params=pltpu.CompilerParams(
            dimension_semantics=("arbitrary", "arbitrary"),
            vmem_limit_bytes=VMEM_LIMIT),
        name="gla",
    )(proj, proj, proj, proj, sm, wg, bg, gn)


def _merge_kernel(ya_ref, yg_ref, wa_ref, wg_ref, ma_ref, mg_ref, o_ref):
    a = jnp.dot(ya_ref[...], wa_ref[...], preferred_element_type=F32)
    g = jnp.dot(yg_ref[...], wg_ref[...], preferred_element_type=F32)
    gate_a = _sigmoid(ma_ref[...].astype(F32))
    gate_g = _sigmoid(mg_ref[...].astype(F32))
    o_ref[...] = (gate_a * a + gate_g * g).astype(BF16)


def _merge(y_att, y_gla, w_att, w_gla, proj, cols):
    n, d_att = y_att.shape
    d_gla = y_gla.shape[1]
    d = w_att.shape[1]
    tm, tn = 1024, PROJ_TN
    return pl.pallas_call(
        _merge_kernel,
        grid=(n // tm, d // tn),
        in_specs=[
            pl.BlockSpec((tm, d_att), lambda i, j: (i, 0)),
            pl.BlockSpec((tm, d_gla), lambda i, j: (i, 0)),
            pl.BlockSpec((d_att, tn), lambda i, j: (0, j)),
            pl.BlockSpec((d_gla, tn), lambda i, j: (0, j)),
            pl.BlockSpec((tm, tn), lambda i, j: (i, cols["m_att"] // tn + j)),
            pl.BlockSpec((tm, tn), lambda i, j: (i, cols["m_gla"] // tn + j)),
        ],
        out_specs=pl.BlockSpec((tm, tn), lambda i, j: (i, j)),
        out_shape=jax.ShapeDtypeStruct((n, d), BF16),
        compiler_params=pltpu.CompilerParams(
            dimension_semantics=("arbitrary", "arbitrary"), vmem_limit_bytes=VMEM_LIMIT),
        name="merge",
    )(y_att, y_gla, w_att, w_gla, proj, proj)


def _resproj_kernel(m_ref, w_ref, x_ref, gate_ref, o_ref):
    y = jnp.dot(m_ref[...], w_ref[...], preferred_element_type=F32)
    o_ref[...] = x_ref[...] + gate_ref[...] * y


def _resproj(merged, w, x2, gate, seq):
    n, d = x2.shape
    tm, tn = 1024, PROJ_TN
    per_b = seq // tm
    return pl.pallas_call(
        _resproj_kernel,
        grid=(n // tm, d // tn),
        in_specs=[
            pl.BlockSpec((tm, merged.shape[1]), lambda i, j: (i, 0)),
            pl.BlockSpec((merged.shape[1], tn), lambda i, j: (0, j)),
            pl.BlockSpec((tm, tn), lambda i, j: (i, j)),
            pl.BlockSpec((None, 1, tn), lambda i, j: (i // per_b, 0, j)),
        ],
        out_specs=pl.BlockSpec((tm, tn), lambda i, j: (i, j)),
        out_shape=jax.ShapeDtypeStruct((n, d), F32),
        compiler_params=pltpu.CompilerParams(
            dimension_semantics=("arbitrary", "arbitrary"), vmem_limit_bytes=VMEM_LIMIT),
        name="resproj",
    )(merged, w, x2, gate)


FFN_TM = 512
FFN_TF = 1024


def _swiglu_step(hn, w1, w3, w2):
    a = jnp.dot(hn, w1, preferred_element_type=F32)
    b = jnp.dot(hn, w3, preferred_element_type=F32)
    return a * _sigmoid(a) * b, w2


def _ffn_kernel(x_ref, g_ref, sc_ref, sh_ref, w1_ref, w3_ref, w2_ref, gate_ref, o_ref,
                hn_ref, acc_ref):
    j = pl.program_id(1)

    @pl.when(j == 0)
    def _():
        hn_ref[...] = _modulate(x_ref[...], g_ref[...], sc_ref[...], sh_ref[...]).astype(BF16)
        acc_ref[...] = jnp.zeros(acc_ref.shape, F32)

    act, w2 = _swiglu_step(hn_ref[...], w1_ref[...], w3_ref[...], w2_ref[...])
    acc_ref[...] += jnp.dot(act.astype(BF16), w2, preferred_element_type=F32)

    @pl.when(j == pl.num_programs(1) - 1)
    def _():
        o_ref[...] = x_ref[...] + gate_ref[...] * acc_ref[...]


def _ffn(x2, g, scale, shift, w1, w3, w2, gate, seq):
    n, d = x2.shape
    dff = w1.shape[1]
    tm, tf = min(FFN_TM, seq), FFN_TF
    per_b = seq // tm
    vec = pl.BlockSpec((None, 1, d), lambda i, j: (i // per_b, 0, 0))
    return pl.pallas_call(
        _ffn_kernel,
        grid=(n // tm, dff // tf),
        in_specs=[
            pl.BlockSpec((tm, d), lambda i, j: (i, 0)),
            pl.BlockSpec((1, d), lambda i, j: (0, 0)),
            vec, vec,
            pl.BlockSpec((d, tf), lambda i, j: (0, j)),
            pl.BlockSpec((d, tf), lambda i, j: (0, j)),
            pl.BlockSpec((tf, d), lambda i, j: (j, 0)),
            vec,
        ],
        out_specs=pl.BlockSpec((tm, d), lambda i, j: (i, 0)),
        out_shape=jax.ShapeDtypeStruct((n, d), F32),
        scratch_shapes=[pltpu.VMEM((tm, d), BF16), pltpu.VMEM((tm, d), F32)],
        compiler_params=pltpu.CompilerParams(
            dimension_semantics=("arbitrary", "arbitrary"), vmem_limit_bytes=VMEM_LIMIT),
        name="ffn_swiglu",
    )(x2, g, scale, shift, w1, w3, w2, gate)


MOE_TM = 512
ROUTE_TM = 1024


def _router_kernel(x_ref, g_ref, sc_ref, sh_ref, wr_ref, hn_ref, rt_ref):
    lane = lax.broadcasted_iota(jnp.int32, (1, LANES), 1)
    lane_f = lane.astype(F32)
    h = _modulate(x_ref[...], g_ref[...], sc_ref[...], sh_ref[...])
    hn_ref[...] = h
    logits = jnp.dot(h, wr_ref[...], preferred_element_type=F32, precision=lax.Precision.HIGHEST)
    logits = jnp.where(lane < N_EXPERTS, logits, -jnp.inf)
    m1 = jnp.max(logits, axis=-1, keepdims=True)
    i1 = jnp.min(jnp.where(logits == m1, lane_f, float(LANES)), axis=-1, keepdims=True)
    rest = jnp.where(lane_f == i1, -jnp.inf, logits)
    m2 = jnp.max(rest, axis=-1, keepdims=True)
    i2 = jnp.min(jnp.where(rest == m2, lane_f, float(LANES)), axis=-1, keepdims=True)
    e2 = jnp.exp(m2 - m1)
    p1 = 1.0 / (1.0 + e2)
    p2 = e2 / (1.0 + e2)
    rt_ref[...] = (jnp.where(lane == 0, i1, 0.0) + jnp.where(lane == 1, i2, 0.0)
                   + jnp.where(lane == 2, p1, 0.0) + jnp.where(lane == 3, p2, 0.0))


def _router(x2, g, scale, shift, w_router, seq):
    n, d = x2.shape
    tm = min(ROUTE_TM, seq)
    per_b = seq // tm
    vec = pl.BlockSpec((None, 1, d), lambda i: (i // per_b, 0, 0))
    return pl.pallas_call(
        _router_kernel,
        grid=(n // tm,),
        in_specs=[
            pl.BlockSpec((tm, d), lambda i: (i, 0)),
            pl.BlockSpec((1, d), lambda i: (0, 0)),
            vec, vec,
            pl.BlockSpec((d, LANES), lambda i: (0, 0)),
        ],
        out_specs=[pl.BlockSpec((tm, d), lambda i: (i, 0)),
                   pl.BlockSpec((tm, LANES), lambda i: (i, 0))],
        out_shape=[jax.ShapeDtypeStruct((n, d), F32), jax.ShapeDtypeStruct((n, LANES), F32)],
        compiler_params=pltpu.CompilerParams(
            dimension_semantics=("arbitrary",), vmem_limit_bytes=VMEM_LIMIT),
        name="moe_router",
    )(x2, g, scale, shift, w_router)


def _route_tables(rt, n, n_exp, tm):
    rows = 2 * n + n_exp * tm
    n_tiles = rows // tm
    ef = rt[:, :2].astype(jnp.int32).T.reshape(-1)
    onehot = (ef[:, None] == jnp.arange(n_exp, dtype=jnp.int32)[None, :]).astype(jnp.int32)
    csum = jnp.cumsum(onehot, axis=0)
    rank = jnp.take_along_axis(csum, ef[:, None], axis=1)[:, 0] - 1
    cnt = csum[-1]
    padded = ((cnt + tm - 1) // tm) * tm
    ends = jnp.cumsum(padded)
    dest = (ends - padded)[ef] + rank
    total = ends[-1]
    tile_start = jnp.arange(n_tiles, dtype=jnp.int32) * tm
    tile_valid = (tile_start < total).astype(jnp.int32)
    tile_expert = jnp.sum((tile_start[:, None] >= ends[None, :]).astype(jnp.int32), axis=1)
    last_tile = jnp.maximum(total // tm - 1, 0)
    tile_expert = jnp.where(tile_valid == 1, tile_expert, tile_expert[last_tile])
    tile_rows = jnp.where(tile_valid == 1, jnp.arange(n_tiles, dtype=jnp.int32), last_tile)
    spare_lo = jnp.concatenate([ends - padded + cnt, total[None]]).astype(jnp.int32)
    spare_hi = jnp.concatenate([ends, jnp.full((1,), rows, ends.dtype)]).astype(jnp.int32)
    return dict(tile_expert=tile_expert, tile_valid=tile_valid, tile_rows=tile_rows, dest=dest,
                spare_lo=spare_lo, spare_hi=spare_hi, rows=rows)


def _row_copies(copy, count):
    def body(r, carry):
        copy(r).start()
        return carry
    lax.fori_loop(0, count, body, 0, unroll=8)


def _dispatch_kernel(dest_ref, lo_ref, hi_ref, hn_ref, xs_hbm, zero_ref, sem, zsem, *, n, n_spare):
    i = pl.program_id(0)
    tm = hn_ref.shape[0]

    @pl.when(i == 0)
    def _():
        zero_ref[...] = jnp.zeros(zero_ref.shape, F32)
        for e in range(n_spare):
            def zero_row(r):
                return pltpu.make_async_copy(zero_ref.at[pl.ds(0, 1)], xs_hbm.at[pl.ds(r, 1)], zsem)

            def start(r, carry):
                zero_row(r).start()
                return carry

            def drain(r, carry):
                zero_row(r).wait()
                return carry
            lax.fori_loop(lo_ref[e], hi_ref[e], start, 0)
            lax.fori_loop(lo_ref[e], hi_ref[e], drain, 0)

    for k in range(2):
        _row_copies(lambda r, k=k: pltpu.make_async_copy(
            hn_ref.at[pl.ds(r, 1)], xs_hbm.at[pl.ds(dest_ref[k * n + i * tm + r], 1)], sem), tm)
    for k in range(2):
        pltpu.make_async_copy(hn_ref, xs_hbm.at[pl.ds(0, tm)], sem).wait()


def _dispatch(hn, tables):
    n, d = hn.shape
    tm = MOE_TM
    kern = functools.partial(_dispatch_kernel, n=n, n_spare=tables["spare_lo"].shape[0])
    grid_spec = pltpu.PrefetchScalarGridSpec(
        num_scalar_prefetch=3,
        grid=(n // tm,),
        in_specs=[pl.BlockSpec((tm, d), lambda i, dest, lo, hi: (i, 0))],
        out_specs=pl.BlockSpec(memory_space=pl.ANY),
        scratch_shapes=[pltpu.VMEM((8, d), F32), pltpu.SemaphoreType.DMA(()),
                        pltpu.SemaphoreType.DMA(())],
    )
    return pl.pallas_call(
        kern,
        grid_spec=grid_spec,
        out_shape=jax.ShapeDtypeStruct((tables["rows"], d), F32),
        compiler_params=pltpu.CompilerParams(
            dimension_semantics=("arbitrary",), vmem_limit_bytes=VMEM_LIMIT),
        name="moe_dispatch",
    )(tables["dest"], tables["spare_lo"], tables["spare_hi"], hn)


def _moe_grouped_kernel(te_ref, tv_ref, tr_ref, x_ref, w1_ref, w3_ref, w2_ref, y_ref, xb_ref):
    t = pl.program_id(0)
    j = pl.program_id(1)
    valid = tv_ref[t] == 1

    @pl.when(j == 0)
    def _():
        xb_ref[...] = x_ref[...].astype(BF16)
        y_ref[...] = jnp.zeros(y_ref.shape, F32)

    @pl.when(valid)
    def _():
        act, w2 = _swiglu_step(xb_ref[...], w1_ref[...], w3_ref[...], w2_ref[...])
        y_ref[...] += jnp.dot(act.astype(BF16), w2, preferred_element_type=F32)


def _moe_grouped(xs, tables, w1, w3, w2):
    rows, d = xs.shape
    dff = w1.shape[2]
    tm, tf = MOE_TM, FFN_TF
    nj = dff // tf

    def jj(t, j, tv):
        return j * tv[t] + (nj - 1) * (1 - tv[t])

    grid_spec = pltpu.PrefetchScalarGridSpec(
        num_scalar_prefetch=3,
        grid=(rows // tm, nj),
        in_specs=[
            pl.BlockSpec((tm, d), lambda t, j, te, tv, tr: (tr[t], 0)),
            pl.BlockSpec((None, d, tf), lambda t, j, te, tv, tr: (te[t], 0, jj(t, j, tv))),
            pl.BlockSpec((None, d, tf), lambda t, j, te, tv, tr: (te[t], 0, jj(t, j, tv))),
            pl.BlockSpec((None, tf, d), lambda t, j, te, tv, tr: (te[t], jj(t, j, tv), 0)),
        ],
        out_specs=pl.BlockSpec((tm, d), lambda t, j, te, tv, tr: (t, 0)),
        scratch_shapes=[pltpu.VMEM((tm, d), BF16)],
    )
    return pl.pallas_call(
        _moe_grouped_kernel,
        grid_spec=grid_spec,
        out_shape=jax.ShapeDtypeStruct((rows, d), F32),
        compiler_params=pltpu.CompilerParams(
            dimension_semantics=("arbitrary", "arbitrary"), vmem_limit_bytes=VMEM_LIMIT),
        name="moe_grouped",
    )(tables["tile_expert"], tables["tile_valid"], tables["tile_rows"], xs, w1, w3, w2)


def _moe_combine_kernel(dest_ref, x_ref, rt_ref, gate_ref, y_hbm, o_ref, y0_ref, y1_ref, sem, *, n):
    i = pl.program_id(0)
    tm = x_ref.shape[0]
    for k, yk_ref in enumerate((y0_ref, y1_ref)):
        _row_copies(lambda r, k=k, yk_ref=yk_ref: pltpu.make_async_copy(
            y_hbm.at[pl.ds(dest_ref[k * n + i * tm + r], 1)], yk_ref.at[pl.ds(r, 1)], sem), tm)
    for yk_ref in (y0_ref, y1_ref):
        pltpu.make_async_copy(y_hbm.at[pl.ds(0, tm)], yk_ref, sem).wait()
    p1 = rt_ref[:, 2:3]
    p2 = rt_ref[:, 3:4]
    o_ref[...] = x_ref[...] + gate_ref[...] * (p1 * y0_ref[...] + p2 * y1_ref[...])


def _moe_combine(x2, y, rt, gate, dest, seq):
    n, d = x2.shape
    tm = min(FFN_TM, seq)
    per_b = seq // tm
    kern = functools.partial(_moe_combine_kernel, n=n)
    grid_spec = pltpu.PrefetchScalarGridSpec(
        num_scalar_prefetch=1,
        grid=(n // tm,),
        in_specs=[
            pl.BlockSpec((tm, d), lambda i, dest: (i, 0)),
            pl.BlockSpec((tm, LANES), lambda i, dest: (i, 0)),
            pl.BlockSpec((None, 1, d), lambda i, dest: (i // per_b, 0, 0)),
            pl.BlockSpec(memory_space=pl.ANY),
        ],
        out_specs=pl.BlockSpec((tm, d), lambda i, dest: (i, 0)),
        scratch_shapes=[pltpu.VMEM((tm, d), F32), pltpu.VMEM((tm, d), F32),
                        pltpu.SemaphoreType.DMA(())],
    )
    return pl.pallas_call(
        kern,
        grid_spec=grid_spec,
        out_shape=jax.ShapeDtypeStruct((n, d), F32),
        compiler_params=pltpu.CompilerParams(
            dimension_semantics=("arbitrary",), vmem_limit_bytes=VMEM_LIMIT),
        name="moe_combine",
    )(dest, x2, rt, gate, y)


def _moe(x2, g, scale, shift, w_router, w1, w3, w2, gate, seq):
    n = x2.shape[0]
    hn, rt = _router(x2, g, scale, shift, w_router, seq)
    tables = _route_tables(rt, n, w1.shape[0], MOE_TM)
    xs = _dispatch(hn, tables)
    y = _moe_grouped(xs, tables, w1, w3, w2)
    return _moe_combine(x2, y, rt, gate, tables["dest"], seq)


def _prep_in_proj(w_in, q_norm, k_norm, idx_k_norm, d):
    sizes = dict(aq=ATT_HEADS * HEAD_DIM, ak=ATT_KV_HEADS * HEAD_DIM, av=ATT_KV_HEADS * HEAD_DIM,
                 iq=IDX_HEADS * HEAD_DIM, ik=HEAD_DIM, iw=IDX_HEADS,
                 gq=d // 2, gk=d // 2, gv=d, glr=GLA_GATE_RANK, gr=d, m_att=d, m_gla=d)
    src, acc = {}, 0
    for name in ("aq", "ak", "av", "iq", "ik", "iw", "gq", "gk", "gv", "glr", "gr", "m_att", "m_gla"):
        src[name] = (acc, sizes[name])
        acc += sizes[name]
    order = ("aq", "iq", "gv", "gr", "m_att", "m_gla", "ak", "av", "gq", "gk")
    cols, parts, off = {}, [], 0
    for name in order:
        s, width = src[name]
        cols[name] = off
        parts.append(w_in[:, s:s + width])
        off += width
    w_main = jnp.concatenate(parts, axis=1).astype(BF16)

    def seg(name):
        s, width = src[name]
        return w_in[:, s:s + width]

    pad = jnp.zeros((d, LANES - IDX_HEADS - GLA_GATE_RANK), w_in.dtype)
    w_small = jnp.concatenate([seg("ik"), seg("iw"), seg("glr"), pad], axis=1).astype(BF16)

    att_scale = HEAD_DIM ** -0.5 * LOG2_E
    idx_scale = (HEAD_DIM ** -0.5) * (IDX_HEADS ** -0.5)
    cg = jnp.ones((off,), F32)
    cg = cg.at[cols["aq"]:cols["aq"] + sizes["aq"]].set(jnp.tile(q_norm * att_scale, ATT_HEADS))
    cg = cg.at[cols["ak"]:cols["ak"] + sizes["ak"]].set(jnp.tile(k_norm, ATT_KV_HEADS))
    cg2 = jnp.concatenate([idx_k_norm, jnp.full((IDX_HEADS,), idx_scale, F32),
                           jnp.ones((LANES - IDX_HEADS,), F32)])
    norm_tiles = tuple(range(cols["aq"] // PROJ_SEG, (cols["aq"] + sizes["aq"]) // PROJ_SEG)) + \
        tuple(range(cols["ak"] // PROJ_SEG, (cols["ak"] + sizes["ak"]) // PROJ_SEG))
    return w_main, w_small, cg.reshape(1, -1), cg2.reshape(1, -1), cols, norm_tiles


def kernel(x, c, ada_w, ada_b, norm_mix, norm_ffn, w_in, q_norm, k_norm, idx_k_norm, w_gla_gate,
           b_gla_gate, gla_out_norm, w_out_attn, w_out_gla, w_out, ffn_w1, ffn_w3, ffn_w2,
           moe_router, moe_w1, moe_w3, moe_w2):
    batch, seq, d = x.shape
    depth = ada_w.shape[0]
    n = batch * seq
    x2 = x.reshape(n, d)

    mod_all = _ada(c, ada_w, ada_b)

    for layer in range(depth):
        mod = mod_all[layer, :batch].reshape(batch, 6, 1, d)
        shift_m, scale_m, gate_m = mod[:, 0], mod[:, 1], mod[:, 2]
        shift_f, scale_f, gate_f = mod[:, 3], mod[:, 4], mod[:, 5]

        w_main, w_small, cg, cg2, cols, norm_tiles = _prep_in_proj(
            w_in[layer], q_norm[layer], k_norm[layer], idx_k_norm[layer], d)
        proj, ik, sm = _modproj(x2, norm_mix[layer].reshape(1, d), scale_m, shift_m,
                                w_main, cg, w_small, cg2, seq, norm_tiles)
        y_att = _attention(proj, ik, sm, batch, seq, cols)
        wg = jnp.zeros((LANES, w_gla_gate.shape[2]), F32).at[
            GLA_GATE_RANK:2 * GLA_GATE_RANK].set(w_gla_gate[layer])
        y_gla = _gla(proj, sm, wg, b_gla_gate[layer].reshape(1, -1),
                     gla_out_norm[layer].reshape(1, -1), batch, seq, cols)
        merged = _merge(y_att, y_gla, w_out_attn[layer].astype(BF16),
                        w_out_gla[layer].astype(BF16), proj, cols)
        x2 = _resproj(merged, w_out[layer].astype(BF16), x2, gate_m, seq)

        g_f = norm_ffn[layer].reshape(1, d)
        i = layer // 2
        if layer % 2 == 0:
            x2 = _ffn(x2, g_f, scale_f, shift_f, ffn_w1[i].astype(BF16), ffn_w3[i].astype(BF16),
                      ffn_w2[i].astype(BF16), gate_f, seq)
        else:
            w_r = jnp.zeros((d, LANES), F32).at[:, :N_EXPERTS].set(moe_router[i])
            x2 = _moe(x2, g_f, scale_f, shift_f, w_r, moe_w1[i].astype(BF16),
                      moe_w3[i].astype(BF16), moe_w2[i].astype(BF16), gate_f, seq)
    return x2.reshape(batch, seq, d)
```

```python
import functools

import jax
import jax.numpy as jnp
from jax import lax
from jax.experimental import pallas as pl
from jax.experimental.pallas import tpu as pltpu

F32 = jnp.float32
BF16 = jnp.bfloat16

ATT_HEADS = 16
ATT_KV_HEADS = 4
HEAD_DIM = 128
IDX_HEADS = 16
TOPK_MAX = 256
GLA_HEADS = 4
GLA_GATE_RANK = 16
GLA_GATE_TAU = 16.0
GLA_CHUNK = 64
N_EXPERTS = 8
EPS = 1e-6
LANES = 128
NEG_BIG = -1e30
INT_MIN = -(2 ** 31)
LOG2_E = 1.4426950408889634

PROJ_SEG = 512
PROJ_TN = 1024
VMEM_LIMIT = 56 * 1024 * 1024


def _nt_dot(a, b):
    return lax.dot_general(a, b, (((1,), (1,)), ((), ())), preferred_element_type=F32)


def _rms(a):
    return a * lax.rsqrt(jnp.mean(a * a, axis=-1, keepdims=True) + EPS)


def _sigmoid(a):
    return 1.0 / (1.0 + jnp.exp(-a))


def _modulate(x, g, scale, shift):
    return _rms(x) * g * (1.0 + scale) + shift


ADA_TK = 256


def _ada_kernel(ct_ref, w_ref, b_ref, o_ref, *, batch):
    k = pl.program_id(1)

    @pl.when(k == 0)
    def _():
        o_ref[...] = jnp.zeros(o_ref.shape, F32)
        for b in range(batch):
            o_ref[b:b + 1, :] = b_ref[...]

    ct = ct_ref[...]
    cond = ct * _sigmoid(ct)
    w = w_ref[...]
    for b in range(batch):
        o_ref[b:b + 1, :] += jnp.sum(w * cond[:, b:b + 1], axis=0, keepdims=True)


def _ada(c, ada_w, ada_b):
    depth, d, n = ada_w.shape
    batch = c.shape[0]
    assert batch <= 8
    ct = jnp.zeros((d, LANES), F32).at[:, :batch].set(c.T)
    return pl.pallas_call(
        functools.partial(_ada_kernel, batch=batch),
        grid=(depth, d // ADA_TK),
        in_specs=[
            pl.BlockSpec((ADA_TK, LANES), lambda l, k: (k, 0)),
            pl.BlockSpec((None, ADA_TK, n), lambda l, k: (l, k, 0)),
            pl.BlockSpec((None, 1, n), lambda l, k: (l, 0, 0)),
        ],
        out_specs=pl.BlockSpec((None, 8, n), lambda l, k: (l, 0, 0)),
        out_shape=jax.ShapeDtypeStruct((depth, 8, n), F32),
        compiler_params=pltpu.CompilerParams(
            dimension_semantics=("arbitrary", "arbitrary"), vmem_limit_bytes=VMEM_LIMIT),
        name="ada_mod",
    )(ct, ada_w, ada_b.reshape(depth, 1, n))


def _modproj_kernel(x_ref, g_ref, sc_ref, sh_ref, w_ref, cg_ref, w2_ref, cg2_ref,
                    o_ref, ik_ref, sm_ref, hn_ref, *, norm_tiles):
    j = pl.program_id(1)

    @pl.when(j == 0)
    def _():
        h = _modulate(x_ref[...], g_ref[...], sc_ref[...], sh_ref[...]).astype(BF16)
        hn_ref[...] = h
        small = jnp.dot(h, w2_ref[...], preferred_element_type=F32)
        ik_ref[...] = (_rms(small[:, :LANES]) * cg2_ref[:, :LANES]).astype(BF16)
        sm_ref[...] = small[:, LANES:] * cg2_ref[:, LANES:]

    acc = jnp.dot(hn_ref[...], w_ref[...], preferred_element_type=F32)

    for part in range(PROJ_TN // PROJ_SEG):
        seg = j * (PROJ_TN // PROJ_SEG) + part
        cols = slice(part * PROJ_SEG, (part + 1) * PROJ_SEG)
        is_norm = functools.reduce(jnp.logical_or, [seg == t for t in norm_tiles])

        @pl.when(is_norm)
        def _(cols=cols):
            for c in range(cols.start, cols.stop, LANES):
                sl = slice(c, c + LANES)
                o_ref[:, sl] = (_rms(acc[:, sl]) * cg_ref[:, sl]).astype(BF16)

        @pl.when(jnp.logical_not(is_norm))
        def _(cols=cols):
            o_ref[:, cols] = acc[:, cols].astype(BF16)


def _modproj(x2, g, scale, shift, w, cg, w2, cg2, seq, norm_tiles):
    n, d = x2.shape
    ncols = w.shape[1]
    tm = min(1024, seq)
    per_b = seq // tm
    kern = functools.partial(_modproj_kernel, norm_tiles=norm_tiles)
    return pl.pallas_call(
        kern,
        grid=(n // tm, ncols // PROJ_TN),
        in_specs=[
            pl.BlockSpec((tm, d), lambda i, j: (i, 0)),
            pl.BlockSpec((1, d), lambda i, j: (0, 0)),
            pl.BlockSpec((None, 1, d), lambda i, j: (i // per_b, 0, 0)),
            pl.BlockSpec((None, 1, d), lambda i, j: (i // per_b, 0, 0)),
            pl.BlockSpec((d, PROJ_TN), lambda i, j: (0, j)),
            pl.BlockSpec((1, PROJ_TN), lambda i, j: (0, j)),
            pl.BlockSpec((d, 2 * LANES), lambda i, j: (0, 0)),
            pl.BlockSpec((1, 2 * LANES), lambda i, j: (0, 0)),
        ],
        out_specs=[
            pl.BlockSpec((tm, PROJ_TN), lambda i, j: (i, j)),
            pl.BlockSpec((tm, LANES), lambda i, j: (i, 0)),
            pl.BlockSpec((tm, LANES), lambda i, j: (i, 0)),
        ],
        out_shape=[
            jax.ShapeDtypeStruct((n, ncols), BF16),
            jax.ShapeDtypeStruct((n, LANES), BF16),
            jax.ShapeDtypeStruct((n, LANES), F32),
        ],
        scratch_shapes=[pltpu.VMEM((tm, d), BF16)],
        compiler_params=pltpu.CompilerParams(
            dimension_semantics=("arbitrary", "arbitrary"), vmem_limit_bytes=VMEM_LIMIT),
        name="modproj",
    )(x2, g, scale, shift, w, cg, w2, cg2)


ATT_TQ = 128
ATT_TK = 256
BISECT_UNCHECKED_BITS = 19
ATT_SPLIT = 4


def _sortable(a):
    a = jnp.where(a == 0.0, 0.0, a)
    bits = pltpu.bitcast(a, jnp.int32)
    return bits ^ ((bits >> 31) & 0x7FFFFFFF)


def _attn_kernel(aq_ref, iq_ref, ak_ref, av_ref, ik_ref, iw_ref, o_ref,
                 sc_ref, sct_ref, qs_ref, wb_ref, m_ref, acc_ref, *, topk, seq):
    tq, tk = ATT_TQ, ATT_TK
    group = ATT_HEADS // ATT_KV_HEADS
    qi = pl.program_id(1)
    n_kt = (qi * tq + tq + tk - 1) // tk
    row = qi * tq + lax.broadcasted_iota(jnp.int32, (tq, 1), 0)
    lane_col = lax.broadcasted_iota(jnp.int32, (1, tk), 1)
    for h in range(IDX_HEADS):
        wb_ref[h] = jnp.broadcast_to(iw_ref[:, h:h + 1], (tq, LANES))
    for h in range(ATT_HEADS):
        qs_ref[h * tq:(h + 1) * tq, :] = aq_ref[:, h * HEAD_DIM:(h + 1) * HEAD_DIM]

    n_pair = (n_kt + 1) // 2

    def score_pair(i, carry):
        kblk = ik_ref[pl.ds(pl.multiple_of(i * 2 * tk, 2 * tk), 2 * tk), :]
        acc = jnp.zeros((tq, 2 * tk), F32)
        for h in range(IDX_HEADS):
            z = _nt_dot(iq_ref[:, h * HEAD_DIM:(h + 1) * HEAD_DIM], kblk)
            wb = wb_ref[h]
            acc = acc + jnp.concatenate([wb] * (2 * tk // LANES), axis=1) * jnp.maximum(z, 0.0)
        keys = []
        for u in range(2):
            kt = 2 * i + u
            causal = (kt * tk + lane_col) <= row
            keys.append(jnp.where(causal, _sortable(acc[:, u * tk:(u + 1) * tk]), INT_MIN))
            sc_ref[kt] = keys[u]
        sct_ref[i] = jnp.concatenate(keys, axis=1).T
        return carry

    lax.fori_loop(0, n_pair, score_pair, 0)

    def count_where(pred):
        def body(i, c):
            for u in range(2):
                kt = 2 * i + u
                hit = jnp.where(pred(sc_ref[kt], kt), 1.0, 0.0)
                for cc in range(tk // LANES):
                    c = c + hit[:, cc * LANES:(cc + 1) * LANES]
            return c
        c = lax.fori_loop(0, n_pair, body, jnp.zeros((tq, LANES), F32))
        return jnp.sum(c, axis=-1, keepdims=True)

    kf = float(topk)

    def count_ge_t(t):
        def body(i, c):
            hit = jnp.where(sct_ref[i] >= t, 1.0, 0.0)
            return c + jnp.sum(hit.reshape(2 * tk // 64, 8, 8, tq), axis=0)
        c = lax.fori_loop(0, n_pair, body, jnp.zeros((8, 8, tq), F32))
        return jnp.sum(jnp.sum(c, axis=0), axis=0, keepdims=True)

    c0 = count_ge_t(jnp.zeros((1, tq), jnp.int32))
    t0 = jnp.where(c0 >= kf, 0, INT_MIN).astype(jnp.int32)
    n_all = jnp.full((1, tq), float(seq), F32)

    def bit_step(i, state):
        t, ct, open_rows = state
        cand = t | jnp.left_shift(jnp.int32(1), 30 - i)
        c = count_ge_t(cand)
        take = jnp.logical_and(c >= kf, open_rows > 0.0)
        t = jnp.where(take, cand, t)
        ct = jnp.where(take, c, ct)
        open_rows = jnp.where(c == kf, 0.0, open_rows)
        return t, ct, open_rows

    def any_open(open_rows):
        return (jnp.max(open_rows) > 0.0).astype(jnp.int32)

    def tail_cond(state):
        i, _, go = state
        return jnp.logical_and(i < 31, go == 1)

    def tail_step(state):
        i, inner, _ = state
        inner = bit_step(i + 1, bit_step(i, inner))
        return i + 2, inner, any_open(inner[2])

    state = (t0, jnp.where(c0 >= kf, c0, n_all), jnp.where(c0 == kf, 0.0, 1.0))
    state = lax.fori_loop(0, BISECT_UNCHECKED_BITS, bit_step, state)
    _, (thr_t, cnt_t, open_t), _ = lax.while_loop(
        tail_cond, tail_step, (jnp.int32(BISECT_UNCHECKED_BITS), state, any_open(state[2])))

    eye = lax.broadcasted_iota(jnp.int32, (tq, tq), 0) == lax.broadcasted_iota(jnp.int32, (tq, tq), 1)

    def to_col(v):
        return jnp.sum(jnp.where(eye, v, 0.0), axis=1, keepdims=True)

    thr = ((to_col((thr_t >> 16).astype(F32)).astype(jnp.int32) << 16)
           | to_col((thr_t & 0xFFFF).astype(F32)).astype(jnp.int32))
    cnt_thr = to_col(cnt_t)
    open_rows = to_col(open_t)

    excess = jnp.logical_and(jnp.logical_and(open_rows > 0.0, cnt_thr > kf), thr > INT_MIN)

    @pl.when(jnp.max(jnp.where(excess, 1.0, 0.0)) > 0.0)
    def _():
        need = kf - count_where(lambda key, kt: key > thr)

        def ties_below(j):
            return count_where(
                lambda key, kt: jnp.logical_and(key == thr, (kt * tk + lane_col) < j))

        nbits = max(seq - 1, 1).bit_length()

        def idx_step(i, j0):
            cand = j0 | jnp.left_shift(jnp.int32(1), nbits - 1 - i)
            return jnp.where(ties_below(cand) < need, cand, j0)

        j0 = lax.fori_loop(0, nbits, idx_step, jnp.zeros((tq, 1), jnp.int32))

        def demote(kt, carry):
            key = sc_ref[kt]
            late_tie = jnp.logical_and(key == thr, (kt * tk + lane_col) > j0)
            sc_ref[kt] = jnp.where(jnp.logical_and(excess, late_tie), INT_MIN, key)
            return carry

        lax.fori_loop(0, n_kt, demote, 0)

    thr = jnp.maximum(thr, INT_MIN + 1)

    m_ref[...] = jnp.full(m_ref.shape, NEG_BIG, F32)
    acc_ref[...] = jnp.zeros(acc_ref.shape, F32)
    tk2 = 2 * tk
    ones = jnp.ones((tk2, HEAD_DIM), BF16)

    def kv_pair(i, carry):
        r0 = pl.multiple_of(i * tk2, tk2)
        keep = jnp.concatenate([sc_ref[2 * i] >= thr, sc_ref[2 * i + 1] >= thr], axis=1)[None]
        for g in range(ATT_KV_HEADS):
            kblk = ak_ref[pl.ds(r0, tk2), g * HEAD_DIM:(g + 1) * HEAD_DIM]
            vblk = av_ref[pl.ds(r0, tk2), g * HEAD_DIM:(g + 1) * HEAD_DIM]
            v1 = jnp.concatenate([vblk, ones], axis=1)
            for part in range(ATT_SPLIT):
                nh = group // ATT_SPLIT
                rows = slice(part * nh * tq, (part + 1) * nh * tq)
                q4 = qs_ref[(g * group + part * nh) * tq:(g * group + (part + 1) * nh) * tq, :]
                s = _nt_dot(q4, kblk).reshape(nh, tq, tk2)
                s = jnp.where(keep, s, NEG_BIG).reshape(nh * tq, tk2)
                m_old = m_ref[g, rows, :]
                m_new = jnp.maximum(m_old, jnp.max(s, axis=-1, keepdims=True))
                alpha = jnp.exp2(m_old - m_new)
                p = jnp.exp2(s - jnp.concatenate([m_new] * (tk2 // LANES), axis=1))
                pv = jnp.dot(p.astype(BF16), v1, preferred_element_type=F32)
                acc_ref[g, rows, :] = jnp.concatenate([alpha, alpha], axis=1) * acc_ref[g, rows, :] + pv
                m_ref[g, rows, :] = m_new
        return carry

    lax.fori_loop(0, n_pair, kv_pair, 0)

    for g in range(ATT_KV_HEADS):
        og = acc_ref[g]
        og = og[:, :HEAD_DIM] / og[:, HEAD_DIM:]
        for r in range(group):
            h = g * group + r
            o_ref[:, h * HEAD_DIM:(h + 1) * HEAD_DIM] = og[r * tq:(r + 1) * tq, :].astype(BF16)


def _attention(proj, ik, sm, batch, seq, cols):
    n = proj.shape[0]
    tq, tk = ATT_TQ, ATT_TK
    nq = seq // tq
    width = ATT_HEADS * HEAD_DIM
    kvw = ATT_KV_HEADS * HEAD_DIM
    topk = min(TOPK_MAX, seq // 4)
    group = ATT_HEADS // ATT_KV_HEADS
    kern = functools.partial(_attn_kernel, topk=topk, seq=seq)
    once = pl.Buffered(1)
    return pl.pallas_call(
        kern,
        grid=(batch, nq),
        in_specs=[
            pl.BlockSpec((tq, width), lambda b, q: (b * nq + q, cols["aq"] // width)),
            pl.BlockSpec((tq, width), lambda b, q: (b * nq + q, cols["iq"] // width)),
            pl.BlockSpec((seq, kvw), lambda b, q: (b, cols["ak"] // kvw), pipeline_mode=once),
            pl.BlockSpec((seq, kvw), lambda b, q: (b, cols["av"] // kvw), pipeline_mode=once),
            pl.BlockSpec((seq, LANES), lambda b, q: (b, 0), pipeline_mode=once),
            pl.BlockSpec((tq, LANES), lambda b, q: (b * nq + q, 0)),
        ],
        out_specs=pl.BlockSpec((tq, width), lambda b, q: (b * nq + q, 0)),
        out_shape=jax.ShapeDtypeStruct((n, width), BF16),
        scratch_shapes=[
            pltpu.VMEM((seq // tk, tq, tk), jnp.int32),
            pltpu.VMEM((seq // (2 * tk), 2 * tk, tq), jnp.int32),
            pltpu.VMEM((ATT_HEADS * tq, HEAD_DIM), BF16),
            pltpu.VMEM((IDX_HEADS, tq, LANES), F32),
            pltpu.VMEM((ATT_KV_HEADS, group * tq, LANES), F32),
            pltpu.VMEM((ATT_KV_HEADS, group * tq, 2 * HEAD_DIM), F32),
        ],
        compiler_params=pltpu.CompilerParams(
            dimension_semantics=("arbitrary", "arbitrary"), vmem_limit_bytes=VMEM_LIMIT),
        name="dsa_attention",
    )(proj, proj, proj, proj, ik, sm)


GLA_T = 256


def _gla_kernel(q_ref, k_ref, v_ref, r_ref, lr_ref, wg_ref, bg_ref, gn_ref, o_ref, st_ref, la_ref,
                *, dk, dv):
    c_len = GLA_CHUNK
    hi = lax.Precision.HIGHEST

    @pl.when(pl.program_id(1) == 0)
    def _():
        st_ref[...] = jnp.zeros(st_ref.shape, F32)

    ri = lax.broadcasted_iota(jnp.int32, (c_len, c_len), 0)
    ci = lax.broadcasted_iota(jnp.int32, (c_len, c_len), 1)
    lower = ri >= ci
    tril = jnp.where(lower, 1.0, 0.0).astype(F32)
    qscale = dk ** -0.5

    gl = jnp.dot(lr_ref[...], wg_ref[...], preferred_element_type=F32, precision=hi) + bg_ref[...]
    la_ref[...] = (jnp.minimum(gl, 0.0) - jnp.log(1.0 + jnp.exp(-jnp.abs(gl)))) / GLA_GATE_TAU

    for c in range(GLA_T // c_len):
        rows = slice(c * c_len, (c + 1) * c_len)
        b = jnp.dot(tril, la_ref[rows, :], preferred_element_type=F32, precision=hi)
        b_last = b[c_len - 1:c_len, :]
        k = k_ref[rows, :].astype(F32)
        q_dec = (q_ref[rows, :].astype(F32) * qscale * jnp.exp(b)).astype(BF16)
        k_inv = (k * jnp.exp(-b)).astype(BF16)
        k_tail = k * jnp.exp(b_last - b)
        for h in range(GLA_HEADS):
            ks = slice(h * dk, (h + 1) * dk)
            vs = slice(h * dv, (h + 1) * dv)
            v = v_ref[rows, vs]
            a = jnp.where(lower, _nt_dot(q_dec[:, ks], k_inv[:, ks]), 0.0)
            state = st_ref[h]
            o = (jnp.dot(a.astype(BF16), v, preferred_element_type=F32)
                 + jnp.dot(q_dec[:, ks], state.astype(BF16), preferred_element_type=F32))
            dec = jnp.exp(b[:, ks].T[:, c_len - 1:c_len])
            upd = jnp.dot(k_tail[:, ks].T.astype(BF16), v, preferred_element_type=F32)
            for cc in range(dv // LANES):
                sl = slice(cc * LANES, (cc + 1) * LANES)
                st_ref[h, :, sl] = dec * state[:, sl] + upd[:, sl]
            r = r_ref[rows, vs].astype(F32)
            o_ref[rows, vs] = (_rms(o) * gn_ref[...] * (r * _sigmoid(r))).astype(BF16)


def _gla(proj, sm, wg, bg, gn, batch, seq, cols):
    n = proj.shape[0]
    dk = wg.shape[1] // GLA_HEADS
    dv = gn.shape[1]
    t = min(GLA_T, seq)
    assert t == GLA_T
    ns = seq // t
    kw, vw = GLA_HEADS * dk, GLA_HEADS * dv
    kern = functools.partial(_gla_kernel, dk=dk, dv=dv)
    return pl.pallas_call(
        kern,
        grid=(batch, ns),
        in_specs=[
            pl.BlockSpec((t, kw), lambda b, s: (b * ns + s, cols["gq"] // kw)),
            pl.BlockSpec((t, kw), lambda b, s: (b * ns + s, cols["gk"] // kw)),
            pl.BlockSpec((t, vw), lambda b, s: (b * ns + s, cols["gv"] // vw)),
            pl.BlockSpec((t, vw), lambda b, s: (b * ns + s, cols["gr"] // vw)),
            pl.BlockSpec((t, LANES), lambda b, s: (b * ns + s, 0)),
            pl.BlockSpec((LANES, kw), lambda b, s: (0, 0)),
            pl.BlockSpec((1, kw), lambda b, s: (0, 0)),
            pl.BlockSpec((1, dv), lambda b, s: (0, 0)),
        ],
        out_specs=pl.BlockSpec((t, vw), lambda b, s: (b * ns + s, 0)),
        out_shape=jax.ShapeDtypeStruct((n, vw), BF16),
        scratch_shapes=[pltpu.VMEM((GLA_HEADS, dk, dv), F32), pltpu.VMEM((t, kw), F32)],
        compiler_params=pltpu.CompilerParams(
            dimension_semantics=("arbitrary", "arbitrary"),
            vmem_limit_bytes=VMEM_LIMIT),
        name="gla",
    )(proj, proj, proj, proj, sm, wg, bg, gn)


def _merge_kernel(ya_ref, yg_ref, wa_ref, wg_ref, ma_ref, mg_ref, o_ref):
    a = jnp.dot(ya_ref[...], wa_ref[...], preferred_element_type=F32)
    g = jnp.dot(yg_ref[...], wg_ref[...], preferred_element_type=F32)
    gate_a = _sigmoid(ma_ref[...].astype(F32))
    gate_g = _sigmoid(mg_ref[...].astype(F32))
    o_ref[...] = (gate_a * a + gate_g * g).astype(BF16)


def _merge(y_att, y_gla, w_att, w_gla, proj, cols):
    n, d_att = y_att.shape
    d_gla = y_gla.shape[1]
    d = w_att.shape[1]
    tm, tn = 1024, PROJ_TN
    return pl.pallas_call(
        _merge_kernel,
        grid=(n // tm, d // tn),
        in_specs=[
            pl.BlockSpec((tm, d_att), lambda i, j: (i, 0)),
            pl.BlockSpec((tm, d_gla), lambda i, j: (i, 0)),
            pl.BlockSpec((d_att, tn), lambda i, j: (0, j)),
            pl.BlockSpec((d_gla, tn), lambda i, j: (0, j)),
            pl.BlockSpec((tm, tn), lambda i, j: (i, cols["m_att"] // tn + j)),
            pl.BlockSpec((tm, tn), lambda i, j: (i, cols["m_gla"] // tn + j)),
        ],
        out_specs=pl.BlockSpec((tm, tn), lambda i, j: (i, j)),
        out_shape=jax.ShapeDtypeStruct((n, d), BF16),
        compiler_params=pltpu.CompilerParams(
            dimension_semantics=("arbitrary", "arbitrary"), vmem_limit_bytes=VMEM_LIMIT),
        name="merge",
    )(y_att, y_gla, w_att, w_gla, proj, proj)


def _resproj_kernel(m_ref, w_ref, x_ref, gate_ref, o_ref):
    y = jnp.dot(m_ref[...], w_ref[...], preferred_element_type=F32)
    o_ref[...] = x_ref[...] + gate_ref[...] * y


def _resproj(merged, w, x2, gate, seq):
    n, d = x2.shape
    tm, tn = 1024, PROJ_TN
    per_b = seq // tm
    return pl.pallas_call(
        _resproj_kernel,
        grid=(n // tm, d // tn),
        in_specs=[
            pl.BlockSpec((tm, merged.shape[1]), lambda i, j: (i, 0)),
            pl.BlockSpec((merged.shape[1], tn), lambda i, j: (0, j)),
            pl.BlockSpec((tm, tn), lambda i, j: (i, j)),
            pl.BlockSpec((None, 1, tn), lambda i, j: (i // per_b, 0, j)),
        ],
        out_specs=pl.BlockSpec((tm, tn), lambda i, j: (i, j)),
        out_shape=jax.ShapeDtypeStruct((n, d), F32),
        compiler_params=pltpu.CompilerParams(
            dimension_semantics=("arbitrary", "arbitrary"), vmem_limit_bytes=VMEM_LIMIT),
        name="resproj",
    )(merged, w, x2, gate)


FFN_TM = 512
FFN_TF = 1024


def _swiglu_step(hn, w1, w3, w2):
    a = jnp.dot(hn, w1, preferred_element_type=F32)
    b = jnp.dot(hn, w3, preferred_element_type=F32)
    return a * _sigmoid(a) * b, w2


def _ffn_kernel(x_ref, g_ref, sc_ref, sh_ref, w1_ref, w3_ref, w2_ref, gate_ref, o_ref,
                hn_ref, acc_ref):
    j = pl.program_id(1)

    @pl.when(j == 0)
    def _():
        hn_ref[...] = _modulate(x_ref[...], g_ref[...], sc_ref[...], sh_ref[...]).astype(BF16)
        acc_ref[...] = jnp.zeros(acc_ref.shape, F32)

    act, w2 = _swiglu_step(hn_ref[...], w1_ref[...], w3_ref[...], w2_ref[...])
    acc_ref[...] += jnp.dot(act.astype(BF16), w2, preferred_element_type=F32)

    @pl.when(j == pl.num_programs(1) - 1)
    def _():
        o_ref[...] = x_ref[...] + gate_ref[...] * acc_ref[...]


def _ffn(x2, g, scale, shift, w1, w3, w2, gate, seq):
    n, d = x2.shape
    dff = w1.shape[1]
    tm, tf = min(FFN_TM, seq), FFN_TF
    per_b = seq // tm
    vec = pl.BlockSpec((None, 1, d), lambda i, j: (i // per_b, 0, 0))
    return pl.pallas_call(
        _ffn_kernel,
        grid=(n // tm, dff // tf),
        in_specs=[
            pl.BlockSpec((tm, d), lambda i, j: (i, 0)),
            pl.BlockSpec((1, d), lambda i, j: (0, 0)),
            vec, vec,
            pl.BlockSpec((d, tf), lambda i, j: (0, j)),
            pl.BlockSpec((d, tf), lambda i, j: (0, j)),
            pl.BlockSpec((tf, d), lambda i, j: (j, 0)),
            vec,
        ],
        out_specs=pl.BlockSpec((tm, d), lambda i, j: (i, 0)),
        out_shape=jax.ShapeDtypeStruct((n, d), F32),
        scratch_shapes=[pltpu.VMEM((tm, d), BF16), pltpu.VMEM((tm, d), F32)],
        compiler_params=pltpu.CompilerParams(
            dimension_semantics=("arbitrary", "arbitrary"), vmem_limit_bytes=VMEM_LIMIT),
        name="ffn_swiglu",
    )(x2, g, scale, shift, w1, w3, w2, gate)


MOE_TM = 512
ROUTE_TM = 1024


def _router_kernel(x_ref, g_ref, sc_ref, sh_ref, wr_ref, hn_ref, rt_ref):
    lane = lax.broadcasted_iota(jnp.int32, (1, LANES), 1)
    lane_f = lane.astype(F32)
    h = _modulate(x_ref[...], g_ref[...], sc_ref[...], sh_ref[...])
    hn_ref[...] = h
    logits = jnp.dot(h, wr_ref[...], preferred_element_type=F32, precision=lax.Precision.HIGHEST)
    logits = jnp.where(lane < N_EXPERTS, logits, -jnp.inf)
    m1 = jnp.max(logits, axis=-1, keepdims=True)
    i1 = jnp.min(jnp.where(logits == m1, lane_f, float(LANES)), axis=-1, keepdims=True)
    rest = jnp.where(lane_f == i1, -jnp.inf, logits)
    m2 = jnp.max(rest, axis=-1, keepdims=True)
    i2 = jnp.min(jnp.where(rest == m2, lane_f, float(LANES)), axis=-1, keepdims=True)
    e2 = jnp.exp(m2 - m1)
    p1 = 1.0 / (1.0 + e2)
    p2 = e2 / (1.0 + e2)
    rt_ref[...] = (jnp.where(lane == 0, i1, 0.0) + jnp.where(lane == 1, i2, 0.0)
                   + jnp.where(lane == 2, p1, 0.0) + jnp.where(lane == 3, p2, 0.0))


def _router(x2, g, scale, shift, w_router, seq):
    n, d = x2.shape
    tm = min(ROUTE_TM, seq)
    per_b = seq // tm
    vec = pl.BlockSpec((None, 1, d), lambda i: (i // per_b, 0, 0))
    return pl.pallas_call(
        _router_kernel,
        grid=(n // tm,),
        in_specs=[
            pl.BlockSpec((tm, d), lambda i: (i, 0)),
            pl.BlockSpec((1, d), lambda i: (0, 0)),
            vec, vec,
            pl.BlockSpec((d, LANES), lambda i: (0, 0)),
        ],
        out_specs=[pl.BlockSpec((tm, d), lambda i: (i, 0)),
                   pl.BlockSpec((tm, LANES), lambda i: (i, 0))],
        out_shape=[jax.ShapeDtypeStruct((n, d), F32), jax.ShapeDtypeStruct((n, LANES), F32)],
        compiler_params=pltpu.CompilerParams(
            dimension_semantics=("arbitrary",), vmem_limit_bytes=VMEM_LIMIT),
        name="moe_router",
    )(x2, g, scale, shift, w_router)


def _route_tables(rt, n, n_exp, tm):
    rows = 2 * n + n_exp * tm
    n_tiles = rows // tm
    ef = rt[:, :2].astype(jnp.int32).T.reshape(-1)
    onehot = ef[:, None] == jnp.arange(n_exp, dtype=jnp.int32)[None, :]
    blocks = onehot.reshape(-1, LANES, n_exp).astype(BF16)
    tri = jnp.tril(jnp.ones((LANES, LANES), BF16))
    within = jnp.einsum("ij,bje->bie", tri, blocks, preferred_element_type=F32).astype(jnp.int32)
    block_tot = within[:, -1, :]
    csum = (within + (jnp.cumsum(block_tot, axis=0) - block_tot)[:, None, :]).reshape(-1, n_exp)
    rank = jnp.take_along_axis(csum, ef[:, None], axis=1)[:, 0] - 1
    cnt = csum[-1]
    padded = ((cnt + tm - 1) // tm) * tm
    ends = jnp.cumsum(padded)
    dest = (ends - padded)[ef] + rank
    total = ends[-1]
    tile_start = jnp.arange(n_tiles, dtype=jnp.int32) * tm
    tile_valid = (tile_start < total).astype(jnp.int32)
    tile_expert = jnp.sum((tile_start[:, None] >= ends[None, :]).astype(jnp.int32), axis=1)
    last_tile = jnp.maximum(total // tm - 1, 0)
    tile_expert = jnp.where(tile_valid == 1, tile_expert, tile_expert[last_tile])
    tile_rows = jnp.where(tile_valid == 1, jnp.arange(n_tiles, dtype=jnp.int32), last_tile)
    spare_lo = jnp.concatenate([ends - padded + cnt, total[None]]).astype(jnp.int32)
    spare_hi = jnp.concatenate([ends, jnp.full((1,), rows, ends.dtype)]).astype(jnp.int32)
    return dict(tile_expert=tile_expert, tile_valid=tile_valid, tile_rows=tile_rows, dest=dest,
                spare_lo=spare_lo, spare_hi=spare_hi, rows=rows)


def _row_copies(copy, count):
    def body(r, carry):
        copy(r).start()
        return carry
    lax.fori_loop(0, count, body, 0, unroll=8)


def _dispatch_kernel(dest_ref, lo_ref, hi_ref, hn_ref, xs_hbm, zero_ref, sem, zsem, *, n, n_spare):
    i = pl.program_id(0)
    tm = hn_ref.shape[0]

    @pl.when(i == 0)
    def _():
        zero_ref[...] = jnp.zeros(zero_ref.shape, F32)
        for e in range(n_spare):
            def zero_row(r):
                return pltpu.make_async_copy(zero_ref.at[pl.ds(0, 1)], xs_hbm.at[pl.ds(r, 1)], zsem)

            def start(r, carry):
                zero_row(r).start()
                return carry

            def drain(r, carry):
                zero_row(r).wait()
                return carry
            lax.fori_loop(lo_ref[e], hi_ref[e], start, 0)
            lax.fori_loop(lo_ref[e], hi_ref[e], drain, 0)

    for k in range(2):
        _row_copies(lambda r, k=k: pltpu.make_async_copy(
            hn_ref.at[pl.ds(r, 1)], xs_hbm.at[pl.ds(dest_ref[k * n + i * tm + r], 1)], sem), tm)
    for k in range(2):
        pltpu.make_async_copy(hn_ref, xs_hbm.at[pl.ds(0, tm)], sem).wait()


def _dispatch(hn, tables):
    n, d = hn.shape
    tm = MOE_TM
    kern = functools.partial(_dispatch_kernel, n=n, n_spare=tables["spare_lo"].shape[0])
    grid_spec = pltpu.PrefetchScalarGridSpec(
        num_scalar_prefetch=3,
        grid=(n // tm,),
        in_specs=[pl.BlockSpec((tm, d), lambda i, dest, lo, hi: (i, 0))],
        out_specs=pl.BlockSpec(memory_space=pl.ANY),
        scratch_shapes=[pltpu.VMEM((8, d), F32), pltpu.SemaphoreType.DMA(()),
                        pltpu.SemaphoreType.DMA(())],
    )
    return pl.pallas_call(
        kern,
        grid_spec=grid_spec,
        out_shape=jax.ShapeDtypeStruct((tables["rows"], d), F32),
        compiler_params=pltpu.CompilerParams(
            dimension_semantics=("arbitrary",), vmem_limit_bytes=VMEM_LIMIT),
        name="moe_dispatch",
    )(tables["dest"], tables["spare_lo"], tables["spare_hi"], hn)


def _moe_grouped_kernel(te_ref, tv_ref, tr_ref, x_ref, w1_ref, w3_ref, w2_ref, y_ref, xb_ref):
    t = pl.program_id(0)
    j = pl.program_id(1)
    valid = tv_ref[t] == 1

    @pl.when(j == 0)
    def _():
        xb_ref[...] = x_ref[...].astype(BF16)
        y_ref[...] = jnp.zeros(y_ref.shape, F32)

    @pl.when(valid)
    def _():
        act, w2 = _swiglu_step(xb_ref[...], w1_ref[...], w3_ref[...], w2_ref[...])
        y_ref[...] += jnp.dot(act.astype(BF16), w2, preferred_element_type=F32)


def _moe_grouped(xs, tables, w1, w3, w2):
    rows, d = xs.shape
    dff = w1.shape[2]
    tm, tf = MOE_TM, FFN_TF
    nj = dff // tf

    def jj(t, j, tv):
        return j * tv[t] + (nj - 1) * (1 - tv[t])

    grid_spec = pltpu.PrefetchScalarGridSpec(
        num_scalar_prefetch=3,
        grid=(rows // tm, nj),
        in_specs=[
            pl.BlockSpec((tm, d), lambda t, j, te, tv, tr: (tr[t], 0)),
            pl.BlockSpec((None, d, tf), lambda t, j, te, tv, tr: (te[t], 0, jj(t, j, tv))),
            pl.BlockSpec((None, d, tf), lambda t, j, te, tv, tr: (te[t], 0, jj(t, j, tv))),
            pl.BlockSpec((None, tf, d), lambda t, j, te, tv, tr: (te[t], jj(t, j, tv), 0)),
        ],
        out_specs=pl.BlockSpec((tm, d), lambda t, j, te, tv, tr: (t, 0)),
        scratch_shapes=[pltpu.VMEM((tm, d), BF16)],
    )
    return pl.pallas_call(
        _moe_grouped_kernel,
        grid_spec=grid_spec,
        out_shape=jax.ShapeDtypeStruct((rows, d), F32),
        compiler_params=pltpu.CompilerParams(
            dimension_semantics=("arbitrary", "arbitrary"), vmem_limit_bytes=VMEM_LIMIT),
        name="moe_grouped",
    )(tables["tile_expert"], tables["tile_valid"], tables["tile_rows"], xs, w1, w3, w2)


def _moe_combine_kernel(dest_ref, x_ref, rt_ref, gate_ref, y_hbm, o_ref, y0_ref, y1_ref, sem, *, n):
    i = pl.program_id(0)
    tm = x_ref.shape[0]
    for k, yk_ref in enumerate((y0_ref, y1_ref)):
        _row_copies(lambda r, k=k, yk_ref=yk_ref: pltpu.make_async_copy(
            y_hbm.at[pl.ds(dest_ref[k * n + i * tm + r], 1)], yk_ref.at[pl.ds(r, 1)], sem), tm)
    for yk_ref in (y0_ref, y1_ref):
        pltpu.make_async_copy(y_hbm.at[pl.ds(0, tm)], yk_ref, sem).wait()
    p1 = rt_ref[:, 2:3]
    p2 = rt_ref[:, 3:4]
    o_ref[...] = x_ref[...] + gate_ref[...] * (p1 * y0_ref[...] + p2 * y1_ref[...])


def _moe_combine(x2, y, rt, gate, dest, seq):
    n, d = x2.shape
    tm = min(FFN_TM, seq)
    per_b = seq // tm
    kern = functools.partial(_moe_combine_kernel, n=n)
    grid_spec = pltpu.PrefetchScalarGridSpec(
        num_scalar_prefetch=1,
        grid=(n // tm,),
        in_specs=[
            pl.BlockSpec((tm, d), lambda i, dest: (i, 0)),
            pl.BlockSpec((tm, LANES), lambda i, dest: (i, 0)),
            pl.BlockSpec((None, 1, d), lambda i, dest: (i // per_b, 0, 0)),
            pl.BlockSpec(memory_space=pl.ANY),
        ],
        out_specs=pl.BlockSpec((tm, d), lambda i, dest: (i, 0)),
        scratch_shapes=[pltpu.VMEM((tm, d), F32), pltpu.VMEM((tm, d), F32),
                        pltpu.SemaphoreType.DMA(())],
    )
    return pl.pallas_call(
        kern,
        grid_spec=grid_spec,
        out_shape=jax.ShapeDtypeStruct((n, d), F32),
        compiler_params=pltpu.CompilerParams(
            dimension_semantics=("arbitrary",), vmem_limit_bytes=VMEM_LIMIT),
        name="moe_combine",
    )(dest, x2, rt, gate, y)


def _moe(x2, g, scale, shift, w_router, w1, w3, w2, gate, seq):
    n = x2.shape[0]
    hn, rt = _router(x2, g, scale, shift, w_router, seq)
    tables = _route_tables(rt, n, w1.shape[0], MOE_TM)
    xs = _dispatch(hn, tables)
    y = _moe_grouped(xs, tables, w1, w3, w2)
    return _moe_combine(x2, y, rt, gate, tables["dest"], seq)


def _prep_in_proj(w_in, q_norm, k_norm, idx_k_norm, d):
    sizes = dict(aq=ATT_HEADS * HEAD_DIM, ak=ATT_KV_HEADS * HEAD_DIM, av=ATT_KV_HEADS * HEAD_DIM,
                 iq=IDX_HEADS * HEAD_DIM, ik=HEAD_DIM, iw=IDX_HEADS,
                 gq=d // 2, gk=d // 2, gv=d, glr=GLA_GATE_RANK, gr=d, m_att=d, m_gla=d)
    src, acc = {}, 0
    for name in ("aq", "ak", "av", "iq", "ik", "iw", "gq", "gk", "gv", "glr", "gr", "m_att", "m_gla"):
        src[name] = (acc, sizes[name])
        acc += sizes[name]
    order = ("aq", "iq", "gv", "gr", "m_att", "m_gla", "ak", "av", "gq", "gk")
    cols, parts, off = {}, [], 0
    for name in order:
        s, width = src[name]
        cols[name] = off
        parts.append(w_in[:, s:s + width])
        off += width
    w_main = jnp.concatenate(parts, axis=1).astype(BF16)

    def seg(name):
        s, width = src[name]
        return w_in[:, s:s + width]

    pad = jnp.zeros((d, LANES - IDX_HEADS - GLA_GATE_RANK), w_in.dtype)
    w_small = jnp.concatenate([seg("ik"), seg("iw"), seg("glr"), pad], axis=1).astype(BF16)

    att_scale = HEAD_DIM ** -0.5 * LOG2_E
    idx_scale = (HEAD_DIM ** -0.5) * (IDX_HEADS ** -0.5)
    cg = jnp.ones((off,), F32)
    cg = cg.at[cols["aq"]:cols["aq"] + sizes["aq"]].set(jnp.tile(q_norm * att_scale, ATT_HEADS))
    cg = cg.at[cols["ak"]:cols["ak"] + sizes["ak"]].set(jnp.tile(k_norm, ATT_KV_HEADS))
    cg2 = jnp.concatenate([idx_k_norm, jnp.full((IDX_HEADS,), idx_scale, F32),
                           jnp.ones((LANES - IDX_HEADS,), F32)])
    norm_tiles = tuple(range(cols["aq"] // PROJ_SEG, (cols["aq"] + sizes["aq"]) // PROJ_SEG)) + \
        tuple(range(cols["ak"] // PROJ_SEG, (cols["ak"] + sizes["ak"]) // PROJ_SEG))
    return w_main, w_small, cg.reshape(1, -1), cg2.reshape(1, -1), cols, norm_tiles


def kernel(x, c, ada_w, ada_b, norm_mix, norm_ffn, w_in, q_norm, k_norm, idx_k_norm, w_gla_gate,
           b_gla_gate, gla_out_norm, w_out_attn, w_out_gla, w_out, ffn_w1, ffn_w3, ffn_w2,
           moe_router, moe_w1, moe_w3, moe_w2):
    batch, seq, d = x.shape
    depth = ada_w.shape[0]
    n = batch * seq
    x2 = x.reshape(n, d)

    mod_all = _ada(c, ada_w, ada_b)

    for layer in range(depth):
        mod = mod_all[layer, :batch].reshape(batch, 6, 1, d)
        shift_m, scale_m, gate_m = mod[:, 0], mod[:, 1], mod[:, 2]
        shift_f, scale_f, gate_f = mod[:, 3], mod[:, 4], mod[:, 5]

        w_main, w_small, cg, cg2, cols, norm_tiles = _prep_in_proj(
            w_in[layer], q_norm[layer], k_norm[layer], idx_k_norm[layer], d)
        proj, ik, sm = _modproj(x2, norm_mix[layer].reshape(1, d), scale_m, shift_m,
                                w_main, cg, w_small, cg2, seq, norm_tiles)
        y_att = _attention(proj, ik, sm, batch, seq, cols)
        wg = jnp.zeros((LANES, w_gla_gate.shape[2]), F32).at[
            GLA_GATE_RANK:2 * GLA_GATE_RANK].set(w_gla_gate[layer])
        y_gla = _gla(proj, sm, wg, b_gla_gate[layer].reshape(1, -1),
                     gla_out_norm[layer].reshape(1, -1), batch, seq, cols)
        merged = _merge(y_att, y_gla, w_out_attn[layer].astype(BF16),
                        w_out_gla[layer].astype(BF16), proj, cols)
        x2 = _resproj(merged, w_out[layer].astype(BF16), x2, gate_m, seq)

        g_f = norm_ffn[layer].reshape(1, d)
        i = layer // 2
        if layer % 2 == 0:
            x2 = _ffn(x2, g_f, scale_f, shift_f, ffn_w1[i].astype(BF16), ffn_w3[i].astype(BF16),
                      ffn_w2[i].astype(BF16), gate_f, seq)
        else:
            w_r = jnp.zeros((d, LANES), F32).at[:, :N_EXPERTS].set(moe_router[i])
            x2 = _moe(x2, g_f, scale_f, shift_f, w_r, moe_w1[i].astype(BF16),
                      moe_w3[i].astype(BF16), moe_w2[i].astype(BF16), gate_f, seq)
    return x2.reshape(batch, seq, d)
```

```python
import functools

import jax
import jax.numpy as jnp
from jax import lax
from jax.experimental import pallas as pl
from jax.experimental.pallas import tpu as pltpu

F32 = jnp.float32
BF16 = jnp.bfloat16

ATT_HEADS = 16
ATT_KV_HEADS = 4
HEAD_DIM = 128
IDX_HEADS = 16
TOPK_MAX = 256
GLA_HEADS = 4
GLA_GATE_RANK = 16
GLA_GATE_TAU = 16.0
GLA_CHUNK = 64
N_EXPERTS = 8
EPS = 1e-6
LANES = 128
NEG_BIG = -1e30
INT_MIN = -(2 ** 31)
LOG2_E = 1.4426950408889634

PROJ_SEG = 512
PROJ_TN = 1024
PROJ_TM = 1024
VMEM_LIMIT = 56 * 1024 * 1024


def _nt_dot(a, b):
    return lax.dot_general(a, b, (((1,), (1,)), ((), ())), preferred_element_type=F32)


def _rms(a):
    return a * lax.rsqrt(jnp.mean(a * a, axis=-1, keepdims=True) + EPS)


def _sigmoid(a):
    return 1.0 / (1.0 + jnp.exp(-a))


def _modulate(x, g, scale, shift):
    return _rms(x) * g * (1.0 + scale) + shift


ADA_TN = 1024
ADA_STREAMS = 3


def _ada_kernel(ct_ref, *refs, batch):
    w_refs, b_ref, o_ref = refs[:ADA_STREAMS], refs[ADA_STREAMS], refs[ADA_STREAMS + 1]
    ct = ct_ref[...]
    cond = ct * _sigmoid(ct)
    o_ref[...] = jnp.zeros(o_ref.shape, F32)
    for s, w_ref in enumerate(w_refs):
        cols = slice(s * ADA_TN, (s + 1) * ADA_TN)
        w = w_ref[...]
        for b in range(batch):
            o_ref[b:b + 1, cols] = (jnp.sum(w * cond[:, b:b + 1], axis=0, keepdims=True)
                                    + b_ref[:, cols])


def _ada(c, ada_w, ada_b):
    depth, d, n = ada_w.shape
    batch = c.shape[0]
    step = ADA_TN * ADA_STREAMS
    assert batch <= 8 and n % step == 0
    ct = jnp.zeros((d, LANES), F32).at[:, :batch].set(c.T)
    slab = [pl.BlockSpec((None, d, ADA_TN), lambda l, j, s=s: (l, 0, ADA_STREAMS * j + s))
            for s in range(ADA_STREAMS)]
    return pl.pallas_call(
        functools.partial(_ada_kernel, batch=batch),
        grid=(depth, n // step),
        in_specs=[pl.BlockSpec((d, LANES), lambda l, j: (0, 0))] + slab
        + [pl.BlockSpec((None, 1, step), lambda l, j: (l, 0, j))],
        out_specs=pl.BlockSpec((None, 8, step), lambda l, j: (l, 0, j)),
        out_shape=jax.ShapeDtypeStruct((depth, 8, n), F32),
        compiler_params=pltpu.CompilerParams(
            dimension_semantics=("arbitrary", "arbitrary"), vmem_limit_bytes=VMEM_LIMIT),
        name="ada_mod",
    )(ct, *([ada_w] * ADA_STREAMS), ada_b.reshape(depth, 1, n))


def _modproj_kernel(x_ref, g_ref, sc_ref, sh_ref, w_ref, cg_ref, w2_ref, cg2_ref,
                    o_ref, ik_ref, sm_ref, hn_ref, *, norm_tiles):
    j = pl.program_id(1)

    @pl.when(j == 0)
    def _():
        h = _modulate(x_ref[...], g_ref[...], sc_ref[...], sh_ref[...]).astype(BF16)
        hn_ref[...] = h
        small = jnp.dot(h, w2_ref[...], preferred_element_type=F32)
        ik_ref[...] = (_rms(small[:, :LANES]) * cg2_ref[:, :LANES]).astype(BF16)
        sm_ref[...] = small[:, LANES:] * cg2_ref[:, LANES:]

    acc = jnp.dot(hn_ref[...], w_ref[...], preferred_element_type=F32)

    for part in range(PROJ_TN // PROJ_SEG):
        seg = j * (PROJ_TN // PROJ_SEG) + part
        cols = slice(part * PROJ_SEG, (part + 1) * PROJ_SEG)
        is_norm = functools.reduce(jnp.logical_or, [seg == t for t in norm_tiles])

        @pl.when(is_norm)
        def _(cols=cols):
            for c in range(cols.start, cols.stop, LANES):
                sl = slice(c, c + LANES)
                o_ref[:, sl] = (_rms(acc[:, sl]) * cg_ref[:, sl]).astype(BF16)

        @pl.when(jnp.logical_not(is_norm))
        def _(cols=cols):
            o_ref[:, cols] = acc[:, cols].astype(BF16)


def _modproj(x2, g, scale, shift, w, cg, w2, cg2, seq, norm_tiles):
    n, d = x2.shape
    ncols = w.shape[1]
    tm = PROJ_TM
    per_b = seq // tm
    kern = functools.partial(_modproj_kernel, norm_tiles=norm_tiles)
    return pl.pallas_call(
        kern,
        grid=(n // tm, ncols // PROJ_TN),
        in_specs=[
            pl.BlockSpec((tm, d), lambda i, j: (i, 0)),
            pl.BlockSpec((1, d), lambda i, j: (0, 0)),
            pl.BlockSpec((None, 1, d), lambda i, j: (i // per_b, 0, 0)),
            pl.BlockSpec((None, 1, d), lambda i, j: (i // per_b, 0, 0)),
            pl.BlockSpec((d, PROJ_TN), lambda i, j: (0, j)),
            pl.BlockSpec((1, PROJ_TN), lambda i, j: (0, j)),
            pl.BlockSpec((d, 2 * LANES), lambda i, j: (0, 0)),
            pl.BlockSpec((1, 2 * LANES), lambda i, j: (0, 0)),
        ],
        out_specs=[
            pl.BlockSpec((tm, PROJ_TN), lambda i, j: (i, j)),
            pl.BlockSpec((tm, LANES), lambda i, j: (i, 0)),
            pl.BlockSpec((tm, LANES), lambda i, j: (i, 0)),
        ],
        out_shape=[
            jax.ShapeDtypeStruct((n, ncols), BF16),
            jax.ShapeDtypeStruct((n, LANES), BF16),
            jax.ShapeDtypeStruct((n, LANES), F32),
        ],
        scratch_shapes=[pltpu.VMEM((tm, d), BF16)],
        compiler_params=pltpu.CompilerParams(
            dimension_semantics=("arbitrary", "arbitrary"), vmem_limit_bytes=VMEM_LIMIT),
        name="modproj",
    )(x2, g, scale, shift, w, cg, w2, cg2)


ATT_TQ = 128
ATT_TK = 256
BISECT_UNCHECKED_BITS = 19
ATT_SPLIT = 4


def _sortable(a):
    a = jnp.where(a == 0.0, 0.0, a)
    bits = pltpu.bitcast(a, jnp.int32)
    return bits ^ ((bits >> 31) & 0x7FFFFFFF)


def _attn_kernel(aq_ref, iq_ref, ak_ref, av_ref, ik_ref, iw_ref, o_ref,
                 sc_ref, sct_ref, qs_ref, wb_ref, m_ref, acc_ref, *, topk, seq):
    tq, tk = ATT_TQ, ATT_TK
    group = ATT_HEADS // ATT_KV_HEADS
    qi = pl.program_id(1)
    n_kt = (qi * tq + tq + tk - 1) // tk
    row = qi * tq + lax.broadcasted_iota(jnp.int32, (tq, 1), 0)
    lane_col = lax.broadcasted_iota(jnp.int32, (1, tk), 1)
    for h in range(IDX_HEADS):
        wb_ref[h] = jnp.broadcast_to(iw_ref[:, h:h + 1], (tq, LANES))
    for h in range(ATT_HEADS):
        qs_ref[h * tq:(h + 1) * tq, :] = aq_ref[:, h * HEAD_DIM:(h + 1) * HEAD_DIM]

    n_pair = (n_kt + 1) // 2

    def score_pair(i, carry):
        kblk = ik_ref[pl.ds(pl.multiple_of(i * 2 * tk, 2 * tk), 2 * tk), :]
        acc = jnp.zeros((tq, 2 * tk), F32)
        for h in range(IDX_HEADS):
            z = _nt_dot(iq_ref[:, h * HEAD_DIM:(h + 1) * HEAD_DIM], kblk)
            wb = wb_ref[h]
            acc = acc + jnp.concatenate([wb] * (2 * tk // LANES), axis=1) * jnp.maximum(z, 0.0)
        keys = []
        for u in range(2):
            kt = 2 * i + u
            causal = (kt * tk + lane_col) <= row
            keys.append(jnp.where(causal, _sortable(acc[:, u * tk:(u + 1) * tk]), INT_MIN))
            sc_ref[kt] = keys[u]
        sct_ref[i] = jnp.concatenate(keys, axis=1).T
        return carry

    lax.fori_loop(0, n_pair, score_pair, 0)

    def count_where(pred):
        def body(i, c):
            for u in range(2):
                kt = 2 * i + u
                hit = jnp.where(pred(sc_ref[kt], kt), 1.0, 0.0)
                for cc in range(tk // LANES):
                    c = c + hit[:, cc * LANES:(cc + 1) * LANES]
            return c
        c = lax.fori_loop(0, n_pair, body, jnp.zeros((tq, LANES), F32))
        return jnp.sum(c, axis=-1, keepdims=True)

    kf = float(topk)

    def count_ge_t(t):
        def body(i, c):
            hit = jnp.where(sct_ref[i] >= t, 1.0, 0.0)
            return c + jnp.sum(hit.reshape(2 * tk // 64, 8, 8, tq), axis=0)
        c = lax.fori_loop(0, n_pair, body, jnp.zeros((8, 8, tq), F32))
        return jnp.sum(jnp.sum(c, axis=0), axis=0, keepdims=True)

    c0 = count_ge_t(jnp.zeros((1, tq), jnp.int32))
    t0 = jnp.where(c0 >= kf, 0, INT_MIN).astype(jnp.int32)
    n_all = jnp.full((1, tq), float(seq), F32)

    def bit_step(i, state):
        t, ct, open_rows = state
        cand = t | jnp.left_shift(jnp.int32(1), 30 - i)
        c = count_ge_t(cand)
        take = jnp.logical_and(c >= kf, open_rows > 0.0)
        t = jnp.where(take, cand, t)
        ct = jnp.where(take, c, ct)
        open_rows = jnp.where(c == kf, 0.0, open_rows)
        return t, ct, open_rows

    def any_open(open_rows):
        return (jnp.max(open_rows) > 0.0).astype(jnp.int32)

    def tail_cond(state):
        i, _, go = state
        return jnp.logical_and(i < 31, go == 1)

    def tail_step(state):
        i, inner, _ = state
        inner = bit_step(i + 1, bit_step(i, inner))
        return i + 2, inner, any_open(inner[2])

    state = (t0, jnp.where(c0 >= kf, c0, n_all), jnp.where(c0 == kf, 0.0, 1.0))
    state = lax.fori_loop(0, BISECT_UNCHECKED_BITS, bit_step, state)
    _, (thr_t, cnt_t, open_t), _ = lax.while_loop(
        tail_cond, tail_step, (jnp.int32(BISECT_UNCHECKED_BITS), state, any_open(state[2])))

    eye = lax.broadcasted_iota(jnp.int32, (tq, tq), 0) == lax.broadcasted_iota(jnp.int32, (tq, tq), 1)

    def to_col(v):
        return jnp.sum(jnp.where(eye, v, 0.0), axis=1, keepdims=True)

    thr = ((to_col((thr_t >> 16).astype(F32)).astype(jnp.int32) << 16)
           | to_col((thr_t & 0xFFFF).astype(F32)).astype(jnp.int32))
    cnt_thr = to_col(cnt_t)
    open_rows = to_col(open_t)

    excess = jnp.logical_and(jnp.logical_and(open_rows > 0.0, cnt_thr > kf), thr > INT_MIN)

    @pl.when(jnp.max(jnp.where(excess, 1.0, 0.0)) > 0.0)
    def _():
        need = kf - count_where(lambda key, kt: key > thr)

        def ties_below(j):
            return count_where(
                lambda key, kt: jnp.logical_and(key == thr, (kt * tk + lane_col) < j))

        nbits = max(seq - 1, 1).bit_length()

        def idx_step(i, j0):
            cand = j0 | jnp.left_shift(jnp.int32(1), nbits - 1 - i)
            return jnp.where(ties_below(cand) < need, cand, j0)

        j0 = lax.fori_loop(0, nbits, idx_step, jnp.zeros((tq, 1), jnp.int32))

        def demote(kt, carry):
            key = sc_ref[kt]
            late_tie = jnp.logical_and(key == thr, (kt * tk + lane_col) > j0)
            sc_ref[kt] = jnp.where(jnp.logical_and(excess, late_tie), INT_MIN, key)
            return carry

        lax.fori_loop(0, n_kt, demote, 0)

    thr = jnp.maximum(thr, INT_MIN + 1)

    m_ref[...] = jnp.full(m_ref.shape, NEG_BIG, F32)
    acc_ref[...] = jnp.zeros(acc_ref.shape, F32)
    tk2 = 2 * tk
    ones = jnp.ones((tk2, HEAD_DIM), BF16)

    def kv_pair(i, carry):
        r0 = pl.multiple_of(i * tk2, tk2)
        keep = jnp.concatenate([sc_ref[2 * i] >= thr, sc_ref[2 * i + 1] >= thr], axis=1)[None]
        for g in range(ATT_KV_HEADS):
            kblk = ak_ref[pl.ds(r0, tk2), g * HEAD_DIM:(g + 1) * HEAD_DIM]
            vblk = av_ref[pl.ds(r0, tk2), g * HEAD_DIM:(g + 1) * HEAD_DIM]
            v1 = jnp.concatenate([vblk, ones], axis=1)
            for part in range(ATT_SPLIT):
                nh = group // ATT_SPLIT
                rows = slice(part * nh * tq, (part + 1) * nh * tq)
                q4 = qs_ref[(g * group + part * nh) * tq:(g * group + (part + 1) * nh) * tq, :]
                s = _nt_dot(q4, kblk).reshape(nh, tq, tk2)
                s = jnp.where(keep, s, NEG_BIG).reshape(nh * tq, tk2)
                m_old = m_ref[g, rows, :]
                m_new = jnp.maximum(m_old, jnp.max(s, axis=-1, keepdims=True))
                alpha = jnp.exp2(m_old - m_new)
                p = jnp.exp2(s - jnp.concatenate([m_new] * (tk2 // LANES), axis=1))
                pv = jnp.dot(p.astype(BF16), v1, preferred_element_type=F32)
                acc_ref[g, rows, :] = jnp.concatenate([alpha, alpha], axis=1) * acc_ref[g, rows, :] + pv
                m_ref[g, rows, :] = m_new
        return carry

    lax.fori_loop(0, n_pair, kv_pair, 0)

    for g in range(ATT_KV_HEADS):
        og = acc_ref[g]
        og = og[:, :HEAD_DIM] / og[:, HEAD_DIM:]
        for r in range(group):
            h = g * group + r
            o_ref[:, h * HEAD_DIM:(h + 1) * HEAD_DIM] = og[r * tq:(r + 1) * tq, :].astype(BF16)


def _attention(proj, ik, sm, batch, seq, cols):
    n = proj.shape[0]
    tq, tk = ATT_TQ, ATT_TK
    nq = seq // tq
    width = ATT_HEADS * HEAD_DIM
    kvw = ATT_KV_HEADS * HEAD_DIM
    topk = min(TOPK_MAX, seq // 4)
    group = ATT_HEADS // ATT_KV_HEADS
    kern = functools.partial(_attn_kernel, topk=topk, seq=seq)
    once = pl.Buffered(1)
    return pl.pallas_call(
        kern,
        grid=(batch, nq),
        in_specs=[
            pl.BlockSpec((tq, width), lambda b, q: (b * nq + q, cols["aq"] // width)),
            pl.BlockSpec((tq, width), lambda b, q: (b * nq + q, cols["iq"] // width)),
            pl.BlockSpec((seq, kvw), lambda b, q: (b, cols["ak"] // kvw), pipeline_mode=once),
            pl.BlockSpec((seq, kvw), lambda b, q: (b, cols["av"] // kvw), pipeline_mode=once),
            pl.BlockSpec((seq, LANES), lambda b, q: (b, 0), pipeline_mode=once),
            pl.BlockSpec((tq, LANES), lambda b, q: (b * nq + q, 0)),
        ],
        out_specs=pl.BlockSpec((tq, width), lambda b, q: (b * nq + q, 0)),
        out_shape=jax.ShapeDtypeStruct((n, width), BF16),
        scratch_shapes=[
            pltpu.VMEM((seq // tk, tq, tk), jnp.int32),
            pltpu.VMEM((seq // (2 * tk), 2 * tk, tq), jnp.int32),
            pltpu.VMEM((ATT_HEADS * tq, HEAD_DIM), BF16),
            pltpu.VMEM((IDX_HEADS, tq, LANES), F32),
            pltpu.VMEM((ATT_KV_HEADS, group * tq, LANES), F32),
            pltpu.VMEM((ATT_KV_HEADS, group * tq, 2 * HEAD_DIM), F32),
        ],
        compiler_params=pltpu.CompilerParams(
            dimension_semantics=("arbitrary", "arbitrary"), vmem_limit_bytes=VMEM_LIMIT),
        name="dsa_attention",
    )(proj, proj, proj, proj, ik, sm)


GLA_T = 256


def _gla_kernel(q_ref, k_ref, v_ref, r_ref, lr_ref, wg_ref, bg_ref, gn_ref, o_ref, st_ref, la_ref,
                *, dk, dv):
    c_len = GLA_CHUNK
    hi = lax.Precision.HIGHEST

    @pl.when(pl.program_id(1) == 0)
    def _():
        st_ref[...] = jnp.zeros(st_ref.shape, F32)

    ri = lax.broadcasted_iota(jnp.int32, (c_len, c_len), 0)
    ci = lax.broadcasted_iota(jnp.int32, (c_len, c_len), 1)
    lower = ri >= ci
    tril = jnp.where(lower, 1.0, 0.0).astype(F32)
    qscale = dk ** -0.5

    gl = jnp.dot(lr_ref[...], wg_ref[...], preferred_element_type=F32, precision=hi) + bg_ref[...]
    la_ref[...] = (jnp.minimum(gl, 0.0) - jnp.log(1.0 + jnp.exp(-jnp.abs(gl)))) / GLA_GATE_TAU

    for c in range(GLA_T // c_len):
        rows = slice(c * c_len, (c + 1) * c_len)
        b = jnp.dot(tril, la_ref[rows, :], preferred_element_type=F32, precision=hi)
        b_last = b[c_len - 1:c_len, :]
        k = k_ref[rows, :].astype(F32)
        q_dec = (q_ref[rows, :].astype(F32) * qscale * jnp.exp(b)).astype(BF16)
        k_inv = (k * jnp.exp(-b)).astype(BF16)
        k_tail = k * jnp.exp(b_last - b)
        for h in range(GLA_HEADS):
            ks = slice(h * dk, (h + 1) * dk)
            vs = slice(h * dv, (h + 1) * dv)
            v = v_ref[rows, vs]
            a = jnp.where(lower, _nt_dot(q_dec[:, ks], k_inv[:, ks]), 0.0)
            state = st_ref[h]
            o = (jnp.dot(a.astype(BF16), v, preferred_element_type=F32)
                 + jnp.dot(q_dec[:, ks], state.astype(BF16), preferred_element_type=F32))
            dec = jnp.exp(b[:, ks].T[:, c_len - 1:c_len])
            upd = jnp.dot(k_tail[:, ks].T.astype(BF16), v, preferred_element_type=F32)
            for cc in range(dv // LANES):
                sl = slice(cc * LANES, (cc + 1) * LANES)
                st_ref[h, :, sl] = dec * state[:, sl] + upd[:, sl]
            r = r_ref[rows, vs].astype(F32)
            o_ref[rows, vs] = (_rms(o) * gn_ref[...] * (r * _sigmoid(r))).astype(BF16)


def _gla(proj, sm, wg, bg, gn, batch, seq, cols):
    n = proj.shape[0]
    dk = wg.shape[1] // GLA_HEADS
    dv = gn.shape[1]
    t = min(GLA_T, seq)
    assert t == GLA_T
    ns = seq // t
    kw, vw = GLA_HEADS * dk, GLA_HEADS * dv
    kern = functools.partial(_gla_kernel, dk=dk, dv=dv)
    return pl.pallas_call(
        kern,
        grid=(batch, ns),
        in_specs=[
            pl.BlockSpec((t, kw), lambda b, s: (b * ns + s, cols["gq"] // kw)),
            pl.BlockSpec((t, kw), lambda b, s: (b * ns + s, cols["gk"] // kw)),
            pl.BlockSpec((t, vw), lambda b, s: (b * ns + s, cols["gv"] // vw)),
            pl.BlockSpec((t, vw), lambda b, s: (b * ns + s, cols["gr"] // vw)),
            pl.BlockSpec((t, LANES), lambda b, s: (b * ns + s, 0)),
            pl.BlockSpec((LANES, kw), lambda b, s: (0, 0)),
            pl.BlockSpec((1, kw), lambda b, s: (0, 0)),
            pl.BlockSpec((1, dv), lambda b, s: (0, 0)),
        ],
        out_specs=pl.BlockSpec((t, vw), lambda b, s: (b * ns + s, 0)),
        out_shape=jax.ShapeDtypeStruct((n, vw), BF16),
        scratch_shapes=[pltpu.VMEM((GLA_HEADS, dk, dv), F32), pltpu.VMEM((t, kw), F32)],
        compiler_params=pltpu.CompilerParams(
            dimension_semantics=("arbitrary", "arbitrary"),
            vmem_limit_bytes=VMEM_LIMIT),
        name="gla",
    )(proj, proj, proj, proj, sm, wg, bg, gn)


def _merge_kernel(ya_ref, yg_ref, wa_ref, wg_ref, ma_ref, mg_ref, o_ref):
    a = jnp.dot(ya_ref[...], wa_ref[...], preferred_element_type=F32)
    g = jnp.dot(yg_ref[...], wg_ref[...], preferred_element_type=F32)
    gate_a = _sigmoid(ma_ref[...].astype(F32))
    gate_g = _sigmoid(mg_ref[...].astype(F32))
    o_ref[...] = (gate_a * a + gate_g * g).astype(BF16)


def _merge(y_att, y_gla, w_att, w_gla, proj, cols):
    n, d_att = y_att.shape
    d_gla = y_gla.shape[1]
    d = w_att.shape[1]
    tm, tn = PROJ_TM, PROJ_TN
    return pl.pallas_call(
        _merge_kernel,
        grid=(n // tm, d // tn),
        in_specs=[
            pl.BlockSpec((tm, d_att), lambda i, j: (i, 0)),
            pl.BlockSpec((tm, d_gla), lambda i, j: (i, 0)),
            pl.BlockSpec((d_att, tn), lambda i, j: (0, j)),
            pl.BlockSpec((d_gla, tn), lambda i, j: (0, j)),
            pl.BlockSpec((tm, tn), lambda i, j: (i, cols["m_att"] // tn + j)),
            pl.BlockSpec((tm, tn), lambda i, j: (i, cols["m_gla"] // tn + j)),
        ],
        out_specs=pl.BlockSpec((tm, tn), lambda i, j: (i, j)),
        out_shape=jax.ShapeDtypeStruct((n, d), BF16),
        compiler_params=pltpu.CompilerParams(
            dimension_semantics=("arbitrary", "arbitrary"), vmem_limit_bytes=VMEM_LIMIT),
        name="merge",
    )(y_att, y_gla, w_att, w_gla, proj, proj)


def _resproj_kernel(m_ref, w_ref, x_ref, gate_ref, o_ref):
    y = jnp.dot(m_ref[...], w_ref[...], preferred_element_type=F32)
    o_ref[...] = x_ref[...] + gate_ref[...] * y


def _resproj(merged, w, x2, gate, seq):
    n, d = x2.shape
    tm, tn = PROJ_TM, PROJ_TN
    per_b = seq // tm
    return pl.pallas_call(
        _resproj_kernel,
        grid=(n // tm, d // tn),
        in_specs=[
            pl.BlockSpec((tm, merged.shape[1]), lambda i, j: (i, 0)),
            pl.BlockSpec((merged.shape[1], tn), lambda i, j: (0, j)),
            pl.BlockSpec((tm, tn), lambda i, j: (i, j)),
            pl.BlockSpec((None, 1, tn), lambda i, j: (i // per_b, 0, j)),
        ],
        out_specs=pl.BlockSpec((tm, tn), lambda i, j: (i, j)),
        out_shape=jax.ShapeDtypeStruct((n, d), F32),
        compiler_params=pltpu.CompilerParams(
            dimension_semantics=("arbitrary", "arbitrary"), vmem_limit_bytes=VMEM_LIMIT),
        name="resproj",
    )(merged, w, x2, gate)


FFN_TM = 512
FFN_TF = 1024


def _swiglu_step(hn, w1, w3, w2):
    a = jnp.dot(hn, w1, preferred_element_type=F32)
    b = jnp.dot(hn, w3, preferred_element_type=F32)
    return a * _sigmoid(a) * b, w2


def _ffn_kernel(x_ref, g_ref, sc_ref, sh_ref, w1_ref, w3_ref, w2_ref, gate_ref, o_ref,
                hn_ref, acc_ref):
    j = pl.program_id(1)

    @pl.when(j == 0)
    def _():
        hn_ref[...] = _modulate(x_ref[...], g_ref[...], sc_ref[...], sh_ref[...]).astype(BF16)
        acc_ref[...] = jnp.zeros(acc_ref.shape, F32)

    act, w2 = _swiglu_step(hn_ref[...], w1_ref[...], w3_ref[...], w2_ref[...])
    acc_ref[...] += jnp.dot(act.astype(BF16), w2, preferred_element_type=F32)

    @pl.when(j == pl.num_programs(1) - 1)
    def _():
        o_ref[...] = x_ref[...] + gate_ref[...] * acc_ref[...]


def _ffn(x2, g, scale, shift, w1, w3, w2, gate, seq):
    n, d = x2.shape
    dff = w1.shape[1]
    tm, tf = min(FFN_TM, seq), FFN_TF
    per_b = seq // tm
    vec = pl.BlockSpec((None, 1, d), lambda i, j: (i // per_b, 0, 0))
    return pl.pallas_call(
        _ffn_kernel,
        grid=(n // tm, dff // tf),
        in_specs=[
            pl.BlockSpec((tm, d), lambda i, j: (i, 0)),
            pl.BlockSpec((1, d), lambda i, j: (0, 0)),
            vec, vec,
            pl.BlockSpec((d, tf), lambda i, j: (0, j)),
            pl.BlockSpec((d, tf), lambda i, j: (0, j)),
            pl.BlockSpec((tf, d), lambda i, j: (j, 0)),
            vec,
        ],
        out_specs=pl.BlockSpec((tm, d), lambda i, j: (i, 0)),
        out_shape=jax.ShapeDtypeStruct((n, d), F32),
        scratch_shapes=[pltpu.VMEM((tm, d), BF16), pltpu.VMEM((tm, d), F32)],
        compiler_params=pltpu.CompilerParams(
            dimension_semantics=("arbitrary", "arbitrary"), vmem_limit_bytes=VMEM_LIMIT),
        name="ffn_swiglu",
    )(x2, g, scale, shift, w1, w3, w2, gate)


MOE_TM = 512
ROUTE_TM = 1024


def _router_kernel(x_ref, g_ref, sc_ref, sh_ref, wr_ref, hn_ref, rt_ref):
    lane = lax.broadcasted_iota(jnp.int32, (1, LANES), 1)
    lane_f = lane.astype(F32)
    h = _modulate(x_ref[...], g_ref[...], sc_ref[...], sh_ref[...])
    hn_ref[...] = h
    logits = jnp.dot(h, wr_ref[...], preferred_element_type=F32, precision=lax.Precision.HIGHEST)
    logits = jnp.where(lane < N_EXPERTS, logits, -jnp.inf)
    m1 = jnp.max(logits, axis=-1, keepdims=True)
    i1 = jnp.min(jnp.where(logits == m1, lane_f, float(LANES)), axis=-1, keepdims=True)
    rest = jnp.where(lane_f == i1, -jnp.inf, logits)
    m2 = jnp.max(rest, axis=-1, keepdims=True)
    i2 = jnp.min(jnp.where(rest == m2, lane_f, float(LANES)), axis=-1, keepdims=True)
    e2 = jnp.exp(m2 - m1)
    p1 = 1.0 / (1.0 + e2)
    p2 = e2 / (1.0 + e2)
    rt_ref[...] = (jnp.where(lane == 0, i1, 0.0) + jnp.where(lane == 1, i2, 0.0)
                   + jnp.where(lane == 2, p1, 0.0) + jnp.where(lane == 3, p2, 0.0))


def _router(x2, g, scale, shift, w_router, seq):
    n, d = x2.shape
    tm = min(ROUTE_TM, seq)
    per_b = seq // tm
    vec = pl.BlockSpec((None, 1, d), lambda i: (i // per_b, 0, 0))
    return pl.pallas_call(
        _router_kernel,
        grid=(n // tm,),
        in_specs=[
            pl.BlockSpec((tm, d), lambda i: (i, 0)),
            pl.BlockSpec((1, d), lambda i: (0, 0)),
            vec, vec,
            pl.BlockSpec((d, LANES), lambda i: (0, 0)),
        ],
        out_specs=[pl.BlockSpec((tm, d), lambda i: (i, 0)),
                   pl.BlockSpec((tm, LANES), lambda i: (i, 0))],
        out_shape=[jax.ShapeDtypeStruct((n, d), F32), jax.ShapeDtypeStruct((n, LANES), F32)],
        compiler_params=pltpu.CompilerParams(
            dimension_semantics=("arbitrary",), vmem_limit_bytes=VMEM_LIMIT),
        name="moe_router",
    )(x2, g, scale, shift, w_router)


def _route_tables(rt, n, n_exp, tm):
    rows = 2 * n + n_exp * tm
    n_tiles = rows // tm
    ef = rt[:, :2].astype(jnp.int32).T.reshape(-1)
    onehot = (ef[:, None] == jnp.arange(n_exp, dtype=jnp.int32)[None, :]).astype(jnp.int32)
    csum = jnp.cumsum(onehot, axis=0)
    rank = jnp.take_along_axis(csum, ef[:, None], axis=1)[:, 0] - 1
    cnt = csum[-1]
    padded = ((cnt + tm - 1) // tm) * tm
    ends = jnp.cumsum(padded)
    dest = (ends - padded)[ef] + rank
    total = ends[-1]
    tile_start = jnp.arange(n_tiles, dtype=jnp.int32) * tm
    tile_valid = (tile_start < total).astype(jnp.int32)
    tile_expert = jnp.sum((tile_start[:, None] >= ends[None, :]).astype(jnp.int32), axis=1)
    last_tile = jnp.maximum(total // tm - 1, 0)
    tile_expert = jnp.where(tile_valid == 1, tile_expert, tile_expert[last_tile])
    tile_rows = jnp.where(tile_valid == 1, jnp.arange(n_tiles, dtype=jnp.int32), last_tile)
    spare_lo = jnp.concatenate([ends - padded + cnt, total[None]]).astype(jnp.int32)
    spare_hi = jnp.concatenate([ends, jnp.full((1,), rows, ends.dtype)]).astype(jnp.int32)
    return dict(tile_expert=tile_expert, tile_valid=tile_valid, tile_rows=tile_rows, dest=dest,
                spare_lo=spare_lo, spare_hi=spare_hi, rows=rows)


def _row_copies(copy, count):
    def body(r, carry):
        copy(r).start()
        return carry
    lax.fori_loop(0, count, body, 0, unroll=8)


def _dispatch_kernel(dest_ref, lo_ref, hi_ref, hn_ref, xs_hbm, zero_ref, sem, zsem, *, n, n_spare):
    i = pl.program_id(0)
    tm = hn_ref.shape[0]

    @pl.when(i == 0)
    def _():
        zero_ref[...] = jnp.zeros(zero_ref.shape, F32)
        for e in range(n_spare):
            def zero_row(r):
                return pltpu.make_async_copy(zero_ref.at[pl.ds(0, 1)], xs_hbm.at[pl.ds(r, 1)], zsem)

            def start(r, carry):
                zero_row(r).start()
                return carry

            def drain(r, carry):
                zero_row(r).wait()
                return carry
            lax.fori_loop(lo_ref[e], hi_ref[e], start, 0)
            lax.fori_loop(lo_ref[e], hi_ref[e], drain, 0)

    for k in range(2):
        _row_copies(lambda r, k=k: pltpu.make_async_copy(
            hn_ref.at[pl.ds(r, 1)], xs_hbm.at[pl.ds(dest_ref[k * n + i * tm + r], 1)], sem), tm)
    for k in range(2):
        pltpu.make_async_copy(hn_ref, xs_hbm.at[pl.ds(0, tm)], sem).wait()


def _dispatch(hn, tables):
    n, d = hn.shape
    tm = MOE_TM
    kern = functools.partial(_dispatch_kernel, n=n, n_spare=tables["spare_lo"].shape[0])
    grid_spec = pltpu.PrefetchScalarGridSpec(
        num_scalar_prefetch=3,
        grid=(n // tm,),
        in_specs=[pl.BlockSpec((tm, d), lambda i, dest, lo, hi: (i, 0))],
        out_specs=pl.BlockSpec(memory_space=pl.ANY),
        scratch_shapes=[pltpu.VMEM((8, d), F32), pltpu.SemaphoreType.DMA(()),
                        pltpu.SemaphoreType.DMA(())],
    )
    return pl.pallas_call(
        kern,
        grid_spec=grid_spec,
        out_shape=jax.ShapeDtypeStruct((tables["rows"], d), F32),
        compiler_params=pltpu.CompilerParams(
            dimension_semantics=("arbitrary",), vmem_limit_bytes=VMEM_LIMIT),
        name="moe_dispatch",
    )(tables["dest"], tables["spare_lo"], tables["spare_hi"], hn)


def _moe_grouped_kernel(te_ref, tv_ref, tr_ref, x_ref, w1_ref, w3_ref, w2_ref, y_ref, xb_ref):
    t = pl.program_id(0)
    j = pl.program_id(1)
    valid = tv_ref[t] == 1

    @pl.when(j == 0)
    def _():
        xb_ref[...] = x_ref[...].astype(BF16)
        y_ref[...] = jnp.zeros(y_ref.shape, F32)

    @pl.when(valid)
    def _():
        act, w2 = _swiglu_step(xb_ref[...], w1_ref[...], w3_ref[...], w2_ref[...])
        y_ref[...] += jnp.dot(act.astype(BF16), w2, preferred_element_type=F32)


def _moe_grouped(xs, tables, w1, w3, w2):
    rows, d = xs.shape
    dff = w1.shape[2]
    tm, tf = MOE_TM, FFN_TF
    nj = dff // tf

    def jj(t, j, tv):
        return j * tv[t] + (nj - 1) * (1 - tv[t])

    grid_spec = pltpu.PrefetchScalarGridSpec(
        num_scalar_prefetch=3,
        grid=(rows // tm, nj),
        in_specs=[
            pl.BlockSpec((tm, d), lambda t, j, te, tv, tr: (tr[t], 0)),
            pl.BlockSpec((None, d, tf), lambda t, j, te, tv, tr: (te[t], 0, jj(t, j, tv))),
            pl.BlockSpec((None, d, tf), lambda t, j, te, tv, tr: (te[t], 0, jj(t, j, tv))),
            pl.BlockSpec((None, tf, d), lambda t, j, te, tv, tr: (te[t], jj(t, j, tv), 0)),
        ],
        out_specs=pl.BlockSpec((tm, d), lambda t, j, te, tv, tr: (t, 0)),
        scratch_shapes=[pltpu.VMEM((tm, d), BF16)],
    )
    return pl.pallas_call(
        _moe_grouped_kernel,
        grid_spec=grid_spec,
        out_shape=jax.ShapeDtypeStruct((rows, d), F32),
        compiler_params=pltpu.CompilerParams(
            dimension_semantics=("arbitrary", "arbitrary"), vmem_limit_bytes=VMEM_LIMIT),
        name="moe_grouped",
    )(tables["tile_expert"], tables["tile_valid"], tables["tile_rows"], xs, w1, w3, w2)


def _moe_combine_kernel(dest_ref, x_ref, rt_ref, gate_ref, y_hbm, o_ref, y0_ref, y1_ref, sem, *, n):
    i = pl.program_id(0)
    tm = x_ref.shape[0]
    for k, yk_ref in enumerate((y0_ref, y1_ref)):
        _row_copies(lambda r, k=k, yk_ref=yk_ref: pltpu.make_async_copy(
            y_hbm.at[pl.ds(dest_ref[k * n + i * tm + r], 1)], yk_ref.at[pl.ds(r, 1)], sem), tm)
    for yk_ref in (y0_ref, y1_ref):
        pltpu.make_async_copy(y_hbm.at[pl.ds(0, tm)], yk_ref, sem).wait()
    p1 = rt_ref[:, 2:3]
    p2 = rt_ref[:, 3:4]
    o_ref[...] = x_ref[...] + gate_ref[...] * (p1 * y0_ref[...] + p2 * y1_ref[...])


def _moe_combine(x2, y, rt, gate, dest, seq):
    n, d = x2.shape
    tm = min(FFN_TM, seq)
    per_b = seq // tm
    kern = functools.partial(_moe_combine_kernel, n=n)
    grid_spec = pltpu.PrefetchScalarGridSpec(
        num_scalar_prefetch=1,
        grid=(n // tm,),
        in_specs=[
            pl.BlockSpec((tm, d), lambda i, dest: (i, 0)),
            pl.BlockSpec((tm, LANES), lambda i, dest: (i, 0)),
            pl.BlockSpec((None, 1, d), lambda i, dest: (i // per_b, 0, 0)),
            pl.BlockSpec(memory_space=pl.ANY),
        ],
        out_specs=pl.BlockSpec((tm, d), lambda i, dest: (i, 0)),
        scratch_shapes=[pltpu.VMEM((tm, d), F32), pltpu.VMEM((tm, d), F32),
                        pltpu.SemaphoreType.DMA(())],
    )
    return pl.pallas_call(
        kern,
        grid_spec=grid_spec,
        out_shape=jax.ShapeDtypeStruct((n, d), F32),
        compiler_params=pltpu.CompilerParams(
            dimension_semantics=("arbitrary",), vmem_limit_bytes=VMEM_LIMIT),
        name="moe_combine",
    )(dest, x2, rt, gate, y)


def _moe(x2, g, scale, shift, w_router, w1, w3, w2, gate, seq):
    n = x2.shape[0]
    hn, rt = _router(x2, g, scale, shift, w_router, seq)
    tables = _route_tables(rt, n, w1.shape[0], MOE_TM)
    xs = _dispatch(hn, tables)
    y = _moe_grouped(xs, tables, w1, w3, w2)
    return _moe_combine(x2, y, rt, gate, tables["dest"], seq)


def _prep_in_proj(w_in, q_norm, k_norm, idx_k_norm, d):
    sizes = dict(aq=ATT_HEADS * HEAD_DIM, ak=ATT_KV_HEADS * HEAD_DIM, av=ATT_KV_HEADS * HEAD_DIM,
                 iq=IDX_HEADS * HEAD_DIM, ik=HEAD_DIM, iw=IDX_HEADS,
                 gq=d // 2, gk=d // 2, gv=d, glr=GLA_GATE_RANK, gr=d, m_att=d, m_gla=d)
    src, acc = {}, 0
    for name in ("aq", "ak", "av", "iq", "ik", "iw", "gq", "gk", "gv", "glr", "gr", "m_att", "m_gla"):
        src[name] = (acc, sizes[name])
        acc += sizes[name]
    order = ("aq", "iq", "gv", "gr", "m_att", "m_gla", "ak", "av", "gq", "gk")
    cols, parts, off = {}, [], 0
    for name in order:
        s, width = src[name]
        cols[name] = off
        parts.append(w_in[:, s:s + width])
        off += width
    w_main = jnp.concatenate(parts, axis=1).astype(BF16)

    def seg(name):
        s, width = src[name]
        return w_in[:, s:s + width]

    pad = jnp.zeros((d, LANES - IDX_HEADS - GLA_GATE_RANK), w_in.dtype)
    w_small = jnp.concatenate([seg("ik"), seg("iw"), seg("glr"), pad], axis=1).astype(BF16)

    att_scale = HEAD_DIM ** -0.5 * LOG2_E
    idx_scale = (HEAD_DIM ** -0.5) * (IDX_HEADS ** -0.5)
    cg = jnp.ones((off,), F32)
    cg = cg.at[cols["aq"]:cols["aq"] + sizes["aq"]].set(jnp.tile(q_norm * att_scale, ATT_HEADS))
    cg = cg.at[cols["ak"]:cols["ak"] + sizes["ak"]].set(jnp.tile(k_norm, ATT_KV_HEADS))
    cg2 = jnp.concatenate([idx_k_norm, jnp.full((IDX_HEADS,), idx_scale, F32),
                           jnp.ones((LANES - IDX_HEADS,), F32)])
    norm_tiles = tuple(range(cols["aq"] // PROJ_SEG, (cols["aq"] + sizes["aq"]) // PROJ_SEG)) + \
        tuple(range(cols["ak"] // PROJ_SEG, (cols["ak"] + sizes["ak"]) // PROJ_SEG))
    return w_main, w_small, cg.reshape(1, -1), cg2.reshape(1, -1), cols, norm_tiles


def kernel(x, c, ada_w, ada_b, norm_mix, norm_ffn, w_in, q_norm, k_norm, idx_k_norm, w_gla_gate,
           b_gla_gate, gla_out_norm, w_out_attn, w_out_gla, w_out, ffn_w1, ffn_w3, ffn_w2,
           moe_router, moe_w1, moe_w3, moe_w2):
    batch, seq, d = x.shape
    depth = ada_w.shape[0]
    n = batch * seq
    for tile in (PROJ_TM, 2 * ATT_TK, GLA_T, ROUTE_TM, FFN_TM, MOE_TM):
        assert seq % tile == 0, (seq, tile)
    assert d % PROJ_TN == 0
    x2 = x.reshape(n, d)

    mod_all = _ada(c, ada_w, ada_b)

    for layer in range(depth):
        mod = mod_all[layer, :batch].reshape(batch, 6, 1, d)
        shift_m, scale_m, gate_m = mod[:, 0], mod[:, 1], mod[:, 2]
        shift_f, scale_f, gate_f = mod[:, 3], mod[:, 4], mod[:, 5]

        w_main, w_small, cg, cg2, cols, norm_tiles = _prep_in_proj(
            w_in[layer], q_norm[layer], k_norm[layer], idx_k_norm[layer], d)
        proj, ik, sm = _modproj(x2, norm_mix[layer].reshape(1, d), scale_m, shift_m,
                                w_main, cg, w_small, cg2, seq, norm_tiles)
        y_att = _attention(proj, ik, sm, batch, seq, cols)
        wg = jnp.zeros((LANES, w_gla_gate.shape[2]), F32).at[
            GLA_GATE_RANK:2 * GLA_GATE_RANK].set(w_gla_gate[layer])
        y_gla = _gla(proj, sm, wg, b_gla_gate[layer].reshape(1, -1),
                     gla_out_norm[layer].reshape(1, -1), batch, seq, cols)
        merged = _merge(y_att, y_gla, w_out_attn[layer].astype(BF16),
                        w_out_gla[layer].astype(BF16), proj, cols)
        x2 = _resproj(merged, w_out[layer].astype(BF16), x2, gate_m, seq)

        g_f = norm_ffn[layer].reshape(1, d)
        i = layer // 2
        if layer % 2 == 0:
            x2 = _ffn(x2, g_f, scale_f, shift_f, ffn_w1[i].astype(BF16), ffn_w3[i].astype(BF16),
                      ffn_w2[i].astype(BF16), gate_f, seq)
        else:
            w_r = jnp.zeros((d, LANES), F32).at[:, :N_EXPERTS].set(moe_router[i])
            x2 = _moe(x2, g_f, scale_f, shift_f, w_r, moe_w1[i].astype(BF16),
                      moe_w3[i].astype(BF16), moe_w2[i].astype(BF16), gate_f, seq)
    return x2.reshape(batch, seq, d)
```

```python
import functools

import jax
import jax.numpy as jnp
from jax import lax
from jax.experimental import pallas as pl
from jax.experimental.pallas import tpu as pltpu

F32 = jnp.float32
BF16 = jnp.bfloat16

ATT_HEADS = 16
ATT_KV_HEADS = 4
HEAD_DIM = 128
IDX_HEADS = 16
TOPK_MAX = 256
GLA_HEADS = 4
GLA_GATE_RANK = 16
GLA_GATE_TAU = 16.0
GLA_CHUNK = 64
N_EXPERTS = 8
EPS = 1e-6
LANES = 128
MASKED = -(2.0 ** 100)
INT_MIN = -(2 ** 31)
LOG2_E = 1.4426950408889634

PROJ_SEG = 512
PROJ_TN = 1024
PROJ_TM = 1024
VMEM_LIMIT = 56 * 1024 * 1024


def _nt_dot(a, b):
    return lax.dot_general(a, b, (((1,), (1,)), ((), ())), preferred_element_type=F32)


def _rms(a):
    return a * lax.rsqrt(jnp.mean(a * a, axis=-1, keepdims=True) + EPS)


def _sigmoid(a):
    return 1.0 / (1.0 + jnp.exp(-a))


def _modulate(x, g, scale, shift):
    return _rms(x) * g * (1.0 + scale) + shift


ADA_TN = 1024
ADA_STREAMS = 3


def _ada_kernel(ct_ref, *refs, batch):
    w_refs, b_ref, o_ref = refs[:ADA_STREAMS], refs[ADA_STREAMS], refs[ADA_STREAMS + 1]
    ct = ct_ref[...]
    cond = ct * _sigmoid(ct)
    o_ref[...] = jnp.zeros(o_ref.shape, F32)
    for s, w_ref in enumerate(w_refs):
        cols = slice(s * ADA_TN, (s + 1) * ADA_TN)
        w = w_ref[...]
        for b in range(batch):
            o_ref[b:b + 1, cols] = (jnp.sum(w * cond[:, b:b + 1], axis=0, keepdims=True)
                                    + b_ref[:, cols])


def _ada(c, ada_w, ada_b):
    depth, d, n = ada_w.shape
    batch = c.shape[0]
    step = ADA_TN * ADA_STREAMS
    assert batch <= 8 and n % step == 0
    ct = jnp.zeros((d, LANES), F32).at[:, :batch].set(c.T)
    slab = [pl.BlockSpec((None, d, ADA_TN), lambda l, j, s=s: (l, 0, ADA_STREAMS * j + s))
            for s in range(ADA_STREAMS)]
    return pl.pallas_call(
        functools.partial(_ada_kernel, batch=batch),
        grid=(depth, n // step),
        in_specs=[pl.BlockSpec((d, LANES), lambda l, j: (0, 0))] + slab
        + [pl.BlockSpec((None, 1, step), lambda l, j: (l, 0, j))],
        out_specs=pl.BlockSpec((None, 8, step), lambda l, j: (l, 0, j)),
        out_shape=jax.ShapeDtypeStruct((depth, 8, n), F32),
        compiler_params=pltpu.CompilerParams(
            dimension_semantics=("arbitrary", "arbitrary"), vmem_limit_bytes=VMEM_LIMIT),
        name="ada_mod",
    )(ct, *([ada_w] * ADA_STREAMS), ada_b.reshape(depth, 1, n))


def _modproj_kernel(x_ref, g_ref, sc_ref, sh_ref, w_ref, cg_ref, w2_ref, cg2_ref,
                    o_ref, ik_ref, sm_ref, hn_ref, *, norm_tiles):
    j = pl.program_id(1)

    @pl.when(j == 0)
    def _():
        h = _modulate(x_ref[...], g_ref[...], sc_ref[...], sh_ref[...]).astype(BF16)
        hn_ref[...] = h
        small = jnp.dot(h, w2_ref[...], preferred_element_type=F32)
        ik_ref[...] = (_rms(small[:, :LANES]) * cg2_ref[:, :LANES]).astype(BF16)
        sm_ref[...] = small[:, LANES:] * cg2_ref[:, LANES:]

    acc = jnp.dot(hn_ref[...], w_ref[...], preferred_element_type=F32)

    for part in range(PROJ_TN // PROJ_SEG):
        seg = j * (PROJ_TN // PROJ_SEG) + part
        cols = slice(part * PROJ_SEG, (part + 1) * PROJ_SEG)
        is_norm = functools.reduce(jnp.logical_or, [seg == t for t in norm_tiles])

        @pl.when(is_norm)
        def _(cols=cols):
            for c in range(cols.start, cols.stop, LANES):
                sl = slice(c, c + LANES)
                o_ref[:, sl] = (_rms(acc[:, sl]) * cg_ref[:, sl]).astype(BF16)

        @pl.when(jnp.logical_not(is_norm))
        def _(cols=cols):
            o_ref[:, cols] = acc[:, cols].astype(BF16)


def _modproj(x2, g, scale, shift, w, cg, w2, cg2, seq, norm_tiles):
    n, d = x2.shape
    ncols = w.shape[1]
    tm = PROJ_TM
    per_b = seq // tm
    kern = functools.partial(_modproj_kernel, norm_tiles=norm_tiles)
    return pl.pallas_call(
        kern,
        grid=(n // tm, ncols // PROJ_TN),
        in_specs=[
            pl.BlockSpec((tm, d), lambda i, j: (i, 0)),
            pl.BlockSpec((1, d), lambda i, j: (0, 0)),
            pl.BlockSpec((None, 1, d), lambda i, j: (i // per_b, 0, 0)),
            pl.BlockSpec((None, 1, d), lambda i, j: (i // per_b, 0, 0)),
            pl.BlockSpec((d, PROJ_TN), lambda i, j: (0, j)),
            pl.BlockSpec((1, PROJ_TN), lambda i, j: (0, j)),
            pl.BlockSpec((d, 2 * LANES), lambda i, j: (0, 0)),
            pl.BlockSpec((1, 2 * LANES), lambda i, j: (0, 0)),
        ],
        out_specs=[
            pl.BlockSpec((tm, PROJ_TN), lambda i, j: (i, j)),
            pl.BlockSpec((tm, LANES), lambda i, j: (i, 0)),
            pl.BlockSpec((tm, LANES), lambda i, j: (i, 0)),
        ],
        out_shape=[
            jax.ShapeDtypeStruct((n, ncols), BF16),
            jax.ShapeDtypeStruct((n, LANES), BF16),
            jax.ShapeDtypeStruct((n, LANES), F32),
        ],
        scratch_shapes=[pltpu.VMEM((tm, d), BF16)],
        compiler_params=pltpu.CompilerParams(
            dimension_semantics=("arbitrary", "arbitrary"), vmem_limit_bytes=VMEM_LIMIT),
        name="modproj",
    )(x2, g, scale, shift, w, cg, w2, cg2)


ATT_TQ = 128
ATT_TK = 256
BISECT_UNCHECKED_BITS = 19


def _sortable(a):
    a = jnp.where(a == 0.0, 0.0, a)
    bits = pltpu.bitcast(a, jnp.int32)
    return bits ^ ((bits >> 31) & 0x7FFFFFFF)


def _attn_kernel(aq_ref, iq_ref, ak_ref, av_ref, ik_ref, iw_ref, o_ref,
                 sc_ref, sct_ref, qs_ref, wb_ref, m_ref, acc_ref, *, topk, seq):
    tq, tk = ATT_TQ, ATT_TK
    group = ATT_HEADS // ATT_KV_HEADS
    qi = pl.program_id(1)
    n_kt = (qi * tq + tq + tk - 1) // tk
    row = qi * tq + lax.broadcasted_iota(jnp.int32, (tq, 1), 0)
    lane_col = lax.broadcasted_iota(jnp.int32, (1, tk), 1)
    for h in range(IDX_HEADS):
        wb_ref[h] = jnp.broadcast_to(iw_ref[:, h:h + 1], (tq, LANES))
    for h in range(ATT_HEADS):
        qs_ref[h * tq:(h + 1) * tq, :] = aq_ref[:, h * HEAD_DIM:(h + 1) * HEAD_DIM]

    n_pair = (n_kt + 1) // 2

    def score_pair(i, carry):
        kblk = ik_ref[pl.ds(pl.multiple_of(i * 2 * tk, 2 * tk), 2 * tk), :]
        acc = jnp.zeros((tq, 2 * tk), F32)
        for h in range(IDX_HEADS):
            z = _nt_dot(iq_ref[:, h * HEAD_DIM:(h + 1) * HEAD_DIM], kblk)
            wb = wb_ref[h]
            acc = acc + jnp.concatenate([wb] * (2 * tk // LANES), axis=1) * jnp.maximum(z, 0.0)
        keys = []
        for u in range(2):
            kt = 2 * i + u
            causal = (kt * tk + lane_col) <= row
            keys.append(jnp.where(causal, _sortable(acc[:, u * tk:(u + 1) * tk]), INT_MIN))
            sc_ref[kt] = keys[u]
        sct_ref[i] = jnp.concatenate(keys, axis=1).T
        return carry

    lax.fori_loop(0, n_pair, score_pair, 0)

    def count_where(pred):
        def body(i, c):
            for u in range(2):
                kt = 2 * i + u
                hit = jnp.where(pred(sc_ref[kt], kt), 1.0, 0.0)
                for cc in range(tk // LANES):
                    c = c + hit[:, cc * LANES:(cc + 1) * LANES]
            return c
        c = lax.fori_loop(0, n_pair, body, jnp.zeros((tq, LANES), F32))
        return jnp.sum(c, axis=-1, keepdims=True)

    kf = float(topk)

    def count_ge_t(t):
        def body(i, c):
            hit = jnp.where(sct_ref[i] >= t, 1.0, 0.0)
            return c + jnp.sum(hit.reshape(2 * tk // 64, 8, 8, tq), axis=0)
        c = lax.fori_loop(0, n_pair, body, jnp.zeros((8, 8, tq), F32))
        return jnp.sum(jnp.sum(c, axis=0), axis=0, keepdims=True)

    c0 = count_ge_t(jnp.zeros((1, tq), jnp.int32))
    t0 = jnp.where(c0 >= kf, 0, INT_MIN).astype(jnp.int32)
    n_all = jnp.full((1, tq), float(seq), F32)

    def bit_step(i, state):
        t, ct, open_rows = state
        cand = t | jnp.left_shift(jnp.int32(1), 30 - i)
        c = count_ge_t(cand)
        take = jnp.logical_and(c >= kf, open_rows > 0.0)
        t = jnp.where(take, cand, t)
        ct = jnp.where(take, c, ct)
        open_rows = jnp.where(c == kf, 0.0, open_rows)
        return t, ct, open_rows

    def any_open(open_rows):
        return (jnp.max(open_rows) > 0.0).astype(jnp.int32)

    def tail_cond(state):
        i, _, go = state
        return jnp.logical_and(i < 31, go == 1)

    def tail_step(state):
        i, inner, _ = state
        inner = bit_step(i + 1, bit_step(i, inner))
        return i + 2, inner, any_open(inner[2])

    state = (t0, jnp.where(c0 >= kf, c0, n_all), jnp.where(c0 == kf, 0.0, 1.0))
    state = lax.fori_loop(0, BISECT_UNCHECKED_BITS, bit_step, state)
    _, (thr_t, cnt_t, open_t), _ = lax.while_loop(
        tail_cond, tail_step, (jnp.int32(BISECT_UNCHECKED_BITS), state, any_open(state[2])))

    eye = lax.broadcasted_iota(jnp.int32, (tq, tq), 0) == lax.broadcasted_iota(jnp.int32, (tq, tq), 1)

    def to_col(v):
        return jnp.sum(jnp.where(eye, v, 0.0), axis=1, keepdims=True)

    thr = ((to_col((thr_t >> 16).astype(F32)).astype(jnp.int32) << 16)
           | to_col((thr_t & 0xFFFF).astype(F32)).astype(jnp.int32))
    cnt_thr = to_col(cnt_t)
    open_rows = to_col(open_t)

    excess = jnp.logical_and(jnp.logical_and(open_rows > 0.0, cnt_thr > kf), thr > INT_MIN)

    @pl.when(jnp.max(jnp.where(excess, 1.0, 0.0)) > 0.0)
    def _():
        need = kf - count_where(lambda key, kt: key > thr)

        def ties_below(j):
            return count_where(
                lambda key, kt: jnp.logical_and(key == thr, (kt * tk + lane_col) < j))

        nbits = max(seq - 1, 1).bit_length()

        def idx_step(i, j0):
            cand = j0 | jnp.left_shift(jnp.int32(1), nbits - 1 - i)
            return jnp.where(ties_below(cand) < need, cand, j0)

        j0 = lax.fori_loop(0, nbits, idx_step, jnp.zeros((tq, 1), jnp.int32))

        def demote(kt, carry):
            key = sc_ref[kt]
            late_tie = jnp.logical_and(key == thr, (kt * tk + lane_col) > j0)
            sc_ref[kt] = jnp.where(jnp.logical_and(excess, late_tie), INT_MIN, key)
            return carry

        lax.fori_loop(0, n_kt, demote, 0)

    thr = jnp.maximum(thr, INT_MIN + 1)

    m_ref[...] = jnp.full(m_ref.shape, MASKED, F32)
    acc_ref[...] = jnp.zeros(acc_ref.shape, F32)
    tk2 = 2 * tk
    ones = jnp.ones((tk2, HEAD_DIM), BF16)

    def kv_pair(i, carry):
        r0 = pl.multiple_of(i * tk2, tk2)
        keep = jnp.concatenate([sc_ref[2 * i] >= thr, sc_ref[2 * i + 1] >= thr], axis=1)
        bias = jnp.where(keep, 0.0, MASKED).astype(BF16)
        for g in range(ATT_KV_HEADS):
            kblk = ak_ref[pl.ds(r0, tk2), g * HEAD_DIM:(g + 1) * HEAD_DIM]
            vblk = av_ref[pl.ds(r0, tk2), g * HEAD_DIM:(g + 1) * HEAD_DIM]
            v1 = jnp.concatenate([vblk, ones], axis=1)
            for part in range(group):
                rows = slice(part * tq, (part + 1) * tq)
                q1 = qs_ref[(g * group + part) * tq:(g * group + part + 1) * tq, :]
                s = _nt_dot(q1, kblk).astype(BF16) + bias
                m_old = m_ref[g, rows, :]
                m_new = jnp.maximum(m_old, jnp.max(s, axis=-1, keepdims=True).astype(F32))
                alpha = jnp.exp2(m_old - m_new)
                shift = m_new.astype(BF16)
                p = jnp.exp2(s - jnp.concatenate([shift] * (tk2 // LANES), axis=1))
                pv = jnp.dot(p, v1, preferred_element_type=F32)
                acc_ref[g, rows, :] = jnp.concatenate([alpha, alpha], axis=1) * acc_ref[g, rows, :] + pv
                m_ref[g, rows, :] = m_new
        return carry

    lax.fori_loop(0, n_pair, kv_pair, 0)

    for g in range(ATT_KV_HEADS):
        og = acc_ref[g]
        og = og[:, :HEAD_DIM] / og[:, HEAD_DIM:]
        for r in range(group):
            h = g * group + r
            o_ref[:, h * HEAD_DIM:(h + 1) * HEAD_DIM] = og[r * tq:(r + 1) * tq, :].astype(BF16)


def _attention(proj, ik, sm, batch, seq, cols):
    n = proj.shape[0]
    tq, tk = ATT_TQ, ATT_TK
    nq = seq // tq
    width = ATT_HEADS * HEAD_DIM
    kvw = ATT_KV_HEADS * HEAD_DIM
    topk = min(TOPK_MAX, seq // 4)
    group = ATT_HEADS // ATT_KV_HEADS
    kern = functools.partial(_attn_kernel, topk=topk, seq=seq)
    once = pl.Buffered(1)
    return pl.pallas_call(
        kern,
        grid=(batch, nq),
        in_specs=[
            pl.BlockSpec((tq, width), lambda b, q: (b * nq + q, cols["aq"] // width)),
            pl.BlockSpec((tq, width), lambda b, q: (b * nq + q, cols["iq"] // width)),
            pl.BlockSpec((seq, kvw), lambda b, q: (b, cols["ak"] // kvw), pipeline_mode=once),
            pl.BlockSpec((seq, kvw), lambda b, q: (b, cols["av"] // kvw), pipeline_mode=once),
            pl.BlockSpec((seq, LANES), lambda b, q: (b, 0), pipeline_mode=once),
            pl.BlockSpec((tq, LANES), lambda b, q: (b * nq + q, 0)),
        ],
        out_specs=pl.BlockSpec((tq, width), lambda b, q: (b * nq + q, 0)),
        out_shape=jax.ShapeDtypeStruct((n, width), BF16),
        scratch_shapes=[
            pltpu.VMEM((seq // tk, tq, tk), jnp.int32),
            pltpu.VMEM((seq // (2 * tk), 2 * tk, tq), jnp.int32),
            pltpu.VMEM((ATT_HEADS * tq, HEAD_DIM), BF16),
            pltpu.VMEM((IDX_HEADS, tq, LANES), F32),
            pltpu.VMEM((ATT_KV_HEADS, group * tq, LANES), F32),
            pltpu.VMEM((ATT_KV_HEADS, group * tq, 2 * HEAD_DIM), F32),
        ],
        compiler_params=pltpu.CompilerParams(
            dimension_semantics=("arbitrary", "arbitrary"), vmem_limit_bytes=VMEM_LIMIT),
        name="dsa_attention",
    )(proj, proj, proj, proj, ik, sm)


GLA_T = 256


def _gla_kernel(q_ref, k_ref, v_ref, r_ref, lr_ref, wg_ref, bg_ref, gn_ref, o_ref, st_ref, la_ref,
                *, dk, dv):
    c_len = GLA_CHUNK
    hi = lax.Precision.HIGHEST

    @pl.when(pl.program_id(1) == 0)
    def _():
        st_ref[...] = jnp.zeros(st_ref.shape, F32)

    ri = lax.broadcasted_iota(jnp.int32, (c_len, c_len), 0)
    ci = lax.broadcasted_iota(jnp.int32, (c_len, c_len), 1)
    lower = ri >= ci
    tril = jnp.where(lower, 1.0, 0.0).astype(F32)
    qscale = dk ** -0.5

    gl = jnp.dot(lr_ref[...], wg_ref[...], preferred_element_type=F32, precision=hi) + bg_ref[...]
    la_ref[...] = (jnp.minimum(gl, 0.0) - jnp.log(1.0 + jnp.exp(-jnp.abs(gl)))) / GLA_GATE_TAU

    for c in range(GLA_T // c_len):
        rows = slice(c * c_len, (c + 1) * c_len)
        b = jnp.dot(tril, la_ref[rows, :], preferred_element_type=F32, precision=hi)
        b_last = b[c_len - 1:c_len, :]
        k = k_ref[rows, :].astype(F32)
        q_dec = (q_ref[rows, :].astype(F32) * qscale * jnp.exp(b)).astype(BF16)
        k_inv = (k * jnp.exp(-b)).astype(BF16)
        k_tail = k * jnp.exp(b_last - b)
        for h in range(GLA_HEADS):
            ks = slice(h * dk, (h + 1) * dk)
            vs = slice(h * dv, (h + 1) * dv)
            v = v_ref[rows, vs]
            a = jnp.where(lower, _nt_dot(q_dec[:, ks], k_inv[:, ks]), 0.0)
            state = st_ref[h]
            o = (jnp.dot(a.astype(BF16), v, preferred_element_type=F32)
                 + jnp.dot(q_dec[:, ks], state.astype(BF16), preferred_element_type=F32))
            dec = jnp.exp(b[:, ks].T[:, c_len - 1:c_len])
            upd = jnp.dot(k_tail[:, ks].T.astype(BF16), v, preferred_element_type=F32)
            for cc in range(dv // LANES):
                sl = slice(cc * LANES, (cc + 1) * LANES)
                st_ref[h, :, sl] = dec * state[:, sl] + upd[:, sl]
            r = r_ref[rows, vs].astype(F32)
            o_ref[rows, vs] = (_rms(o) * gn_ref[...] * (r * _sigmoid(r))).astype(BF16)


def _gla(proj, sm, wg, bg, gn, batch, seq, cols):
    n = proj.shape[0]
    dk = wg.shape[1] // GLA_HEADS
    dv = gn.shape[1]
    t = min(GLA_T, seq)
    assert t == GLA_T
    ns = seq // t
    kw, vw = GLA_HEADS * dk, GLA_HEADS * dv
    kern = functools.partial(_gla_kernel, dk=dk, dv=dv)
    return pl.pallas_call(
        kern,
        grid=(batch, ns),
        in_specs=[
            pl.BlockSpec((t, kw), lambda b, s: (b * ns + s, cols["gq"] // kw)),
            pl.BlockSpec((t, kw), lambda b, s: (b * ns + s, cols["gk"] // kw)),
            pl.BlockSpec((t, vw), lambda b, s: (b * ns + s, cols["gv"] // vw)),
            pl.BlockSpec((t, vw), lambda b, s: (b * ns + s, cols["gr"] // vw)),
            pl.BlockSpec((t, LANES), lambda b, s: (b * ns + s, 0)),
            pl.BlockSpec((LANES, kw), lambda b, s: (0, 0)),
            pl.BlockSpec((1, kw), lambda b, s: (0, 0)),
            pl.BlockSpec((1, dv), lambda b, s: (0, 0)),
        ],
        out_specs=pl.BlockSpec((t, vw), lambda b, s: (b * ns + s, 0)),
        out_shape=jax.ShapeDtypeStruct((n, vw), BF16),
        scratch_shapes=[pltpu.VMEM((GLA_HEADS, dk, dv), F32), pltpu.VMEM((t, kw), F32)],
        compiler_params=pltpu.CompilerParams(
            dimension_semantics=("arbitrary", "arbitrary"),
            vmem_limit_bytes=VMEM_LIMIT),
        name="gla",
    )(proj, proj, proj, proj, sm, wg, bg, gn)


def _merge_kernel(ya_ref, yg_ref, wa_ref, wg_ref, ma_ref, mg_ref, o_ref):
    a = jnp.dot(ya_ref[...], wa_ref[...], preferred_element_type=F32)
    g = jnp.dot(yg_ref[...], wg_ref[...], preferred_element_type=F32)
    gate_a = _sigmoid(ma_ref[...].astype(F32))
    gate_g = _sigmoid(mg_ref[...].astype(F32))
    o_ref[...] = (gate_a * a + gate_g * g).astype(BF16)


def _merge(y_att, y_gla, w_att, w_gla, proj, cols):
    n, d_att = y_att.shape
    d_gla = y_gla.shape[1]
    d = w_att.shape[1]
    tm, tn = PROJ_TM, PROJ_TN
    return pl.pallas_call(
        _merge_kernel,
        grid=(n // tm, d // tn),
        in_specs=[
            pl.BlockSpec((tm, d_att), lambda i, j: (i, 0)),
            pl.BlockSpec((tm, d_gla), lambda i, j: (i, 0)),
            pl.BlockSpec((d_att, tn), lambda i, j: (0, j)),
            pl.BlockSpec((d_gla, tn), lambda i, j: (0, j)),
            pl.BlockSpec((tm, tn), lambda i, j: (i, cols["m_att"] // tn + j)),
            pl.BlockSpec((tm, tn), lambda i, j: (i, cols["m_gla"] // tn + j)),
        ],
        out_specs=pl.BlockSpec((tm, tn), lambda i, j: (i, j)),
        out_shape=jax.ShapeDtypeStruct((n, d), BF16),
        compiler_params=pltpu.CompilerParams(
            dimension_semantics=("arbitrary", "arbitrary"), vmem_limit_bytes=VMEM_LIMIT),
        name="merge",
    )(y_att, y_gla, w_att, w_gla, proj, proj)


def _resproj_kernel(m_ref, w_ref, x_ref, gate_ref, o_ref):
    y = jnp.dot(m_ref[...], w_ref[...], preferred_element_type=F32)
    o_ref[...] = x_ref[...] + gate_ref[...] * y


def _resproj(merged, w, x2, gate, seq):
    n, d = x2.shape
    tm, tn = PROJ_TM, PROJ_TN
    per_b = seq // tm
    return pl.pallas_call(
        _resproj_kernel,
        grid=(n // tm, d // tn),
        in_specs=[
            pl.BlockSpec((tm, merged.shape[1]), lambda i, j: (i, 0)),
            pl.BlockSpec((merged.shape[1], tn), lambda i, j: (0, j)),
            pl.BlockSpec((tm, tn), lambda i, j: (i, j)),
            pl.BlockSpec((None, 1, tn), lambda i, j: (i // per_b, 0, j)),
        ],
        out_specs=pl.BlockSpec((tm, tn), lambda i, j: (i, j)),
        out_shape=jax.ShapeDtypeStruct((n, d), F32),
        compiler_params=pltpu.CompilerParams(
            dimension_semantics=("arbitrary", "arbitrary"), vmem_limit_bytes=VMEM_LIMIT),
        name="resproj",
    )(merged, w, x2, gate)


FFN_TM = 512
FFN_TF = 1024


def _swiglu_step(hn, w1, w3, w2):
    a = jnp.dot(hn, w1, preferred_element_type=F32)
    b = jnp.dot(hn, w3, preferred_element_type=F32)
    return a * _sigmoid(a) * b, w2


def _ffn_kernel(x_ref, g_ref, sc_ref, sh_ref, w1_ref, w3_ref, w2_ref, gate_ref, o_ref,
                hn_ref, acc_ref):
    j = pl.program_id(1)

    @pl.when(j == 0)
    def _():
        hn_ref[...] = _modulate(x_ref[...], g_ref[...], sc_ref[...], sh_ref[...]).astype(BF16)
        acc_ref[...] = jnp.zeros(acc_ref.shape, F32)

    act, w2 = _swiglu_step(hn_ref[...], w1_ref[...], w3_ref[...], w2_ref[...])
    acc_ref[...] += jnp.dot(act.astype(BF16), w2, preferred_element_type=F32)

    @pl.when(j == pl.num_programs(1) - 1)
    def _():
        o_ref[...] = x_ref[...] + gate_ref[...] * acc_ref[...]


def _ffn(x2, g, scale, shift, w1, w3, w2, gate, seq):
    n, d = x2.shape
    dff = w1.shape[1]
    tm, tf = min(FFN_TM, seq), FFN_TF
    per_b = seq // tm
    vec = pl.BlockSpec((None, 1, d), lambda i, j: (i // per_b, 0, 0))
    return pl.pallas_call(
        _ffn_kernel,
        grid=(n // tm, dff // tf),
        in_specs=[
            pl.BlockSpec((tm, d), lambda i, j: (i, 0)),
            pl.BlockSpec((1, d), lambda i, j: (0, 0)),
            vec, vec,
            pl.BlockSpec((d, tf), lambda i, j: (0, j)),
            pl.BlockSpec((d, tf), lambda i, j: (0, j)),
            pl.BlockSpec((tf, d), lambda i, j: (j, 0)),
            vec,
        ],
        out_specs=pl.BlockSpec((tm, d), lambda i, j: (i, 0)),
        out_shape=jax.ShapeDtypeStruct((n, d), F32),
        scratch_shapes=[pltpu.VMEM((tm, d), BF16), pltpu.VMEM((tm, d), F32)],
        compiler_params=pltpu.CompilerParams(
            dimension_semantics=("arbitrary", "arbitrary"), vmem_limit_bytes=VMEM_LIMIT),
        name="ffn_swiglu",
    )(x2, g, scale, shift, w1, w3, w2, gate)


MOE_TM = 512
ROUTE_TM = 1024


def _router_kernel(x_ref, g_ref, sc_ref, sh_ref, wr_ref, hn_ref, rt_ref):
    lane = lax.broadcasted_iota(jnp.int32, (1, LANES), 1)
    lane_f = lane.astype(F32)
    h = _modulate(x_ref[...], g_ref[...], sc_ref[...], sh_ref[...])
    hn_ref[...] = h
    logits = jnp.dot(h, wr_ref[...], preferred_element_type=F32, precision=lax.Precision.HIGHEST)
    logits = jnp.where(lane < N_EXPERTS, logits, -jnp.inf)
    m1 = jnp.max(logits, axis=-1, keepdims=True)
    i1 = jnp.min(jnp.where(logits == m1, lane_f, float(LANES)), axis=-1, keepdims=True)
    rest = jnp.where(lane_f == i1, -jnp.inf, logits)
    m2 = jnp.max(rest, axis=-1, keepdims=True)
    i2 = jnp.min(jnp.where(rest == m2, lane_f, float(LANES)), axis=-1, keepdims=True)
    e2 = jnp.exp(m2 - m1)
    p1 = 1.0 / (1.0 + e2)
    p2 = e2 / (1.0 + e2)
    rt_ref[...] = (jnp.where(lane == 0, i1, 0.0) + jnp.where(lane == 1, i2, 0.0)
                   + jnp.where(lane == 2, p1, 0.0) + jnp.where(lane == 3, p2, 0.0))


def _router(x2, g, scale, shift, w_router, seq):
    n, d = x2.shape
    tm = min(ROUTE_TM, seq)
    per_b = seq // tm
    vec = pl.BlockSpec((None, 1, d), lambda i: (i // per_b, 0, 0))
    return pl.pallas_call(
        _router_kernel,
        grid=(n // tm,),
        in_specs=[
            pl.BlockSpec((tm, d), lambda i: (i, 0)),
            pl.BlockSpec((1, d), lambda i: (0, 0)),
            vec, vec,
            pl.BlockSpec((d, LANES), lambda i: (0, 0)),
        ],
        out_specs=[pl.BlockSpec((tm, d), lambda i: (i, 0)),
                   pl.BlockSpec((tm, LANES), lambda i: (i, 0))],
        out_shape=[jax.ShapeDtypeStruct((n, d), F32), jax.ShapeDtypeStruct((n, LANES), F32)],
        compiler_params=pltpu.CompilerParams(
            dimension_semantics=("arbitrary",), vmem_limit_bytes=VMEM_LIMIT),
        name="moe_router",
    )(x2, g, scale, shift, w_router)


def _route_tables(rt, n, n_exp, tm):
    rows = 2 * n + n_exp * tm
    n_tiles = rows // tm
    ef = rt[:, :2].astype(jnp.int32).T.reshape(-1)
    onehot = (ef[:, None] == jnp.arange(n_exp, dtype=jnp.int32)[None, :]).astype(jnp.int32)
    csum = jnp.cumsum(onehot, axis=0)
    rank = jnp.take_along_axis(csum, ef[:, None], axis=1)[:, 0] - 1
    cnt = csum[-1]
    padded = ((cnt + tm - 1) // tm) * tm
    ends = jnp.cumsum(padded)
    dest = (ends - padded)[ef] + rank
    total = ends[-1]
    tile_start = jnp.arange(n_tiles, dtype=jnp.int32) * tm
    tile_valid = (tile_start < total).astype(jnp.int32)
    tile_expert = jnp.sum((tile_start[:, None] >= ends[None, :]).astype(jnp.int32), axis=1)
    last_tile = jnp.maximum(total // tm - 1, 0)
    tile_expert = jnp.where(tile_valid == 1, tile_expert, tile_expert[last_tile])
    tile_rows = jnp.where(tile_valid == 1, jnp.arange(n_tiles, dtype=jnp.int32), last_tile)
    spare_lo = jnp.concatenate([ends - padded + cnt, total[None]]).astype(jnp.int32)
    spare_hi = jnp.concatenate([ends, jnp.full((1,), rows, ends.dtype)]).astype(jnp.int32)
    return dict(tile_expert=tile_expert, tile_valid=tile_valid, tile_rows=tile_rows, dest=dest,
                spare_lo=spare_lo, spare_hi=spare_hi, rows=rows)


def _row_copies(copy, count):
    def body(r, carry):
        copy(r).start()
        return carry
    lax.fori_loop(0, count, body, 0, unroll=8)


def _dispatch_kernel(dest_ref, lo_ref, hi_ref, hn_ref, xs_hbm, zero_ref, sem, zsem, *, n, n_spare):
    i = pl.program_id(0)
    tm = hn_ref.shape[0]

    @pl.when(i == 0)
    def _():
        zero_ref[...] = jnp.zeros(zero_ref.shape, F32)
        for e in range(n_spare):
            def zero_row(r):
                return pltpu.make_async_copy(zero_ref.at[pl.ds(0, 1)], xs_hbm.at[pl.ds(r, 1)], zsem)

            def start(r, carry):
                zero_row(r).start()
                return carry

            def drain(r, carry):
                zero_row(r).wait()
                return carry
            lax.fori_loop(lo_ref[e], hi_ref[e], start, 0)
            lax.fori_loop(lo_ref[e], hi_ref[e], drain, 0)

    for k in range(2):
        _row_copies(lambda r, k=k: pltpu.make_async_copy(
            hn_ref.at[pl.ds(r, 1)], xs_hbm.at[pl.ds(dest_ref[k * n + i * tm + r], 1)], sem), tm)
    for k in range(2):
        pltpu.make_async_copy(hn_ref, xs_hbm.at[pl.ds(0, tm)], sem).wait()


def _dispatch(hn, tables):
    n, d = hn.shape
    tm = MOE_TM
    kern = functools.partial(_dispatch_kernel, n=n, n_spare=tables["spare_lo"].shape[0])
    grid_spec = pltpu.PrefetchScalarGridSpec(
        num_scalar_prefetch=3,
        grid=(n // tm,),
        in_specs=[pl.BlockSpec((tm, d), lambda i, dest, lo, hi: (i, 0))],
        out_specs=pl.BlockSpec(memory_space=pl.ANY),
        scratch_shapes=[pltpu.VMEM((8, d), F32), pltpu.SemaphoreType.DMA(()),
                        pltpu.SemaphoreType.DMA(())],
    )
    return pl.pallas_call(
        kern,
        grid_spec=grid_spec,
        out_shape=jax.ShapeDtypeStruct((tables["rows"], d), F32),
        compiler_params=pltpu.CompilerParams(
            dimension_semantics=("arbitrary",), vmem_limit_bytes=VMEM_LIMIT),
        name="moe_dispatch",
    )(tables["dest"], tables["spare_lo"], tables["spare_hi"], hn)


def _moe_grouped_kernel(te_ref, tv_ref, tr_ref, x_ref, w1_ref, w3_ref, w2_ref, y_ref, xb_ref):
    t = pl.program_id(0)
    j = pl.program_id(1)
    valid = tv_ref[t] == 1

    @pl.when(j == 0)
    def _():
        xb_ref[...] = x_ref[...].astype(BF16)
        y_ref[...] = jnp.zeros(y_ref.shape, F32)

    @pl.when(valid)
    def _():
        act, w2 = _swiglu_step(xb_ref[...], w1_ref[...], w3_ref[...], w2_ref[...])
        y_ref[...] += jnp.dot(act.astype(BF16), w2, preferred_element_type=F32)


def _moe_grouped(xs, tables, w1, w3, w2):
    rows, d = xs.shape
    dff = w1.shape[2]
    tm, tf = MOE_TM, FFN_TF
    nj = dff // tf

    def jj(t, j, tv):
        return j * tv[t] + (nj - 1) * (1 - tv[t])

    grid_spec = pltpu.PrefetchScalarGridSpec(
        num_scalar_prefetch=3,
        grid=(rows // tm, nj),
        in_specs=[
            pl.BlockSpec((tm, d), lambda t, j, te, tv, tr: (tr[t], 0)),
            pl.BlockSpec((None, d, tf), lambda t, j, te, tv, tr: (te[t], 0, jj(t, j, tv))),
            pl.BlockSpec((None, d, tf), lambda t, j, te, tv, tr: (te[t], 0, jj(t, j, tv))),
            pl.BlockSpec((None, tf, d), lambda t, j, te, tv, tr: (te[t], jj(t, j, tv), 0)),
        ],
        out_specs=pl.BlockSpec((tm, d), lambda t, j, te, tv, tr: (t, 0)),
        scratch_shapes=[pltpu.VMEM((tm, d), BF16)],
    )
    return pl.pallas_call(
        _moe_grouped_kernel,
        grid_spec=grid_spec,
        out_shape=jax.ShapeDtypeStruct((rows, d), F32),
        compiler_params=pltpu.CompilerParams(
            dimension_semantics=("arbitrary", "arbitrary"), vmem_limit_bytes=VMEM_LIMIT),
        name="moe_grouped",
    )(tables["tile_expert"], tables["tile_valid"], tables["tile_rows"], xs, w1, w3, w2)


def _moe_combine_kernel(dest_ref, x_ref, rt_ref, gate_ref, y_hbm, o_ref, y0_ref, y1_ref, sem, *, n):
    i = pl.program_id(0)
    tm = x_ref.shape[0]
    for k, yk_ref in enumerate((y0_ref, y1_ref)):
        _row_copies(lambda r, k=k, yk_ref=yk_ref: pltpu.make_async_copy(
            y_hbm.at[pl.ds(dest_ref[k * n + i * tm + r], 1)], yk_ref.at[pl.ds(r, 1)], sem), tm)
    for yk_ref in (y0_ref, y1_ref):
        pltpu.make_async_copy(y_hbm.at[pl.ds(0, tm)], yk_ref, sem).wait()
    p1 = rt_ref[:, 2:3]
    p2 = rt_ref[:, 3:4]
    o_ref[...] = x_ref[...] + gate_ref[...] * (p1 * y0_ref[...] + p2 * y1_ref[...])


def _moe_combine(x2, y, rt, gate, dest, seq):
    n, d = x2.shape
    tm = min(FFN_TM, seq)
    per_b = seq // tm
    kern = functools.partial(_moe_combine_kernel, n=n)
    grid_spec = pltpu.PrefetchScalarGridSpec(
        num_scalar_prefetch=1,
        grid=(n // tm,),
        in_specs=[
            pl.BlockSpec((tm, d), lambda i, dest: (i, 0)),
            pl.BlockSpec((tm, LANES), lambda i, dest: (i, 0)),
            pl.BlockSpec((None, 1, d), lambda i, dest: (i // per_b, 0, 0)),
            pl.BlockSpec(memory_space=pl.ANY),
        ],
        out_specs=pl.BlockSpec((tm, d), lambda i, dest: (i, 0)),
        scratch_shapes=[pltpu.VMEM((tm, d), F32), pltpu.VMEM((tm, d), F32),
                        pltpu.SemaphoreType.DMA(())],
    )
    return pl.pallas_call(
        kern,
        grid_spec=grid_spec,
        out_shape=jax.ShapeDtypeStruct((n, d), F32),
        compiler_params=pltpu.CompilerParams(
            dimension_semantics=("arbitrary",), vmem_limit_bytes=VMEM_LIMIT),
        name="moe_combine",
    )(dest, x2, rt, gate, y)


def _moe(x2, g, scale, shift, w_router, w1, w3, w2, gate, seq):
    n = x2.shape[0]
    hn, rt = _router(x2, g, scale, shift, w_router, seq)
    tables = _route_tables(rt, n, w1.shape[0], MOE_TM)
    xs = _dispatch(hn, tables)
    y = _moe_grouped(xs, tables, w1, w3, w2)
    return _moe_combine(x2, y, rt, gate, tables["dest"], seq)


def _prep_in_proj(w_in, q_norm, k_norm, idx_k_norm, d):
    sizes = dict(aq=ATT_HEADS * HEAD_DIM, ak=ATT_KV_HEADS * HEAD_DIM, av=ATT_KV_HEADS * HEAD_DIM,
                 iq=IDX_HEADS * HEAD_DIM, ik=HEAD_DIM, iw=IDX_HEADS,
                 gq=d // 2, gk=d // 2, gv=d, glr=GLA_GATE_RANK, gr=d, m_att=d, m_gla=d)
    src, acc = {}, 0
    for name in ("aq", "ak", "av", "iq", "ik", "iw", "gq", "gk", "gv", "glr", "gr", "m_att", "m_gla"):
        src[name] = (acc, sizes[name])
        acc += sizes[name]
    order = ("aq", "iq", "gv", "gr", "m_att", "m_gla", "ak", "av", "gq", "gk")
    cols, parts, off = {}, [], 0
    for name in order:
        s, width = src[name]
        cols[name] = off
        parts.append(w_in[:, s:s + width])
        off += width
    w_main = jnp.concatenate(parts, axis=1).astype(BF16)

    def seg(name):
        s, width = src[name]
        return w_in[:, s:s + width]

    pad = jnp.zeros((d, LANES - IDX_HEADS - GLA_GATE_RANK), w_in.dtype)
    w_small = jnp.concatenate([seg("ik"), seg("iw"), seg("glr"), pad], axis=1).astype(BF16)

    att_scale = HEAD_DIM ** -0.5 * LOG2_E
    idx_scale = (HEAD_DIM ** -0.5) * (IDX_HEADS ** -0.5)
    cg = jnp.ones((off,), F32)
    cg = cg.at[cols["aq"]:cols["aq"] + sizes["aq"]].set(jnp.tile(q_norm * att_scale, ATT_HEADS))
    cg = cg.at[cols["ak"]:cols["ak"] + sizes["ak"]].set(jnp.tile(k_norm, ATT_KV_HEADS))
    cg2 = jnp.concatenate([idx_k_norm, jnp.full((IDX_HEADS,), idx_scale, F32),
                           jnp.ones((LANES - IDX_HEADS,), F32)])
    norm_tiles = tuple(range(cols["aq"] // PROJ_SEG, (cols["aq"] + sizes["aq"]) // PROJ_SEG)) + \
        tuple(range(cols["ak"] // PROJ_SEG, (cols["ak"] + sizes["ak"]) // PROJ_SEG))
    return w_main, w_small, cg.reshape(1, -1), cg2.reshape(1, -1), cols, norm_tiles


def kernel(x, c, ada_w, ada_b, norm_mix, norm_ffn, w_in, q_norm, k_norm, idx_k_norm, w_gla_gate,
           b_gla_gate, gla_out_norm, w_out_attn, w_out_gla, w_out, ffn_w1, ffn_w3, ffn_w2,
           moe_router, moe_w1, moe_w3, moe_w2):
    batch, seq, d = x.shape
    depth = ada_w.shape[0]
    n = batch * seq
    for tile in (PROJ_TM, 2 * ATT_TK, GLA_T, ROUTE_TM, FFN_TM, MOE_TM):
        assert seq % tile == 0, (seq, tile)
    assert d % PROJ_TN == 0
    x2 = x.reshape(n, d)

    mod_all = _ada(c, ada_w, ada_b)

    for layer in range(depth):
        mod = mod_all[layer, :batch].reshape(batch, 6, 1, d)
        shift_m, scale_m, gate_m = mod[:, 0], mod[:, 1], mod[:, 2]
        shift_f, scale_f, gate_f = mod[:, 3], mod[:, 4], mod[:, 5]

        w_main, w_small, cg, cg2, cols, norm_tiles = _prep_in_proj(
            w_in[layer], q_norm[layer], k_norm[layer], idx_k_norm[layer], d)
        proj, ik, sm = _modproj(x2, norm_mix[layer].reshape(1, d), scale_m, shift_m,
                                w_main, cg, w_small, cg2, seq, norm_tiles)
        y_att = _attention(proj, ik, sm, batch, seq, cols)
        wg = jnp.zeros((LANES, w_gla_gate.shape[2]), F32).at[
            GLA_GATE_RANK:2 * GLA_GATE_RANK].set(w_gla_gate[layer])
        y_gla = _gla(proj, sm, wg, b_gla_gate[layer].reshape(1, -1),
                     gla_out_norm[layer].reshape(1, -1), batch, seq, cols)
        merged = _merge(y_att, y_gla, w_out_attn[layer].astype(BF16),
                        w_out_gla[layer].astype(BF16), proj, cols)
        x2 = _resproj(merged, w_out[layer].astype(BF16), x2, gate_m, seq)

        g_f = norm_ffn[layer].reshape(1, d)
        i = layer // 2
        if layer % 2 == 0:
            x2 = _ffn(x2, g_f, scale_f, shift_f, ffn_w1[i].astype(BF16), ffn_w3[i].astype(BF16),
                      ffn_w2[i].astype(BF16), gate_f, seq)
        else:
            w_r = jnp.zeros((d, LANES), F32).at[:, :N_EXPERTS].set(moe_router[i])
            x2 = _moe(x2, g_f, scale_f, shift_f, w_r, moe_w1[i].astype(BF16),
                      moe_w3[i].astype(BF16), moe_w2[i].astype(BF16), gate_f, seq)
    return x2.reshape(batch, seq, d)
```

```python
import functools

import jax
import jax.numpy as jnp
from jax import lax
from jax.experimental import pallas as pl
from jax.experimental.pallas import tpu as pltpu

F32 = jnp.float32
BF16 = jnp.bfloat16

ATT_HEADS = 16
ATT_KV_HEADS = 4
HEAD_DIM = 128
IDX_HEADS = 16
TOPK_MAX = 256
GLA_HEADS = 4
GLA_GATE_RANK = 16
GLA_GATE_TAU = 16.0
GLA_CHUNK = 64
N_EXPERTS = 8
EPS = 1e-6
LANES = 128
MASKED = -(2.0 ** 100)
INT_MIN = -(2 ** 31)
LOG2_E = 1.4426950408889634

PROJ_SEG = 512
PROJ_TN = 1024
PROJ_TM = 1024
VMEM_LIMIT = 56 * 1024 * 1024


def _nt_dot(a, b):
    return lax.dot_general(a, b, (((1,), (1,)), ((), ())), preferred_element_type=F32)


def _rms(a):
    return a * lax.rsqrt(jnp.mean(a * a, axis=-1, keepdims=True) + EPS)


def _sigmoid(a):
    return 1.0 / (1.0 + jnp.exp(-a))


def _modulate(x, g, scale, shift):
    return _rms(x) * g * (1.0 + scale) + shift


ADA_TN = 1024
ADA_STREAMS = 3


def _ada_kernel(ct_ref, *refs, batch):
    w_refs, b_ref, o_ref = refs[:ADA_STREAMS], refs[ADA_STREAMS], refs[ADA_STREAMS + 1]
    ct = ct_ref[...]
    cond = ct * _sigmoid(ct)
    o_ref[...] = jnp.zeros(o_ref.shape, F32)
    for s, w_ref in enumerate(w_refs):
        cols = slice(s * ADA_TN, (s + 1) * ADA_TN)
        w = w_ref[...]
        for b in range(batch):
            o_ref[b:b + 1, cols] = (jnp.sum(w * cond[:, b:b + 1], axis=0, keepdims=True)
                                    + b_ref[:, cols])


def _ada(c, ada_w, ada_b):
    depth, d, n = ada_w.shape
    batch = c.shape[0]
    step = ADA_TN * ADA_STREAMS
    assert batch <= 8 and n % step == 0
    ct = jnp.zeros((d, LANES), F32).at[:, :batch].set(c.T)
    slab = [pl.BlockSpec((None, d, ADA_TN), lambda l, j, s=s: (l, 0, ADA_STREAMS * j + s))
            for s in range(ADA_STREAMS)]
    return pl.pallas_call(
        functools.partial(_ada_kernel, batch=batch),
        grid=(depth, n // step),
        in_specs=[pl.BlockSpec((d, LANES), lambda l, j: (0, 0))] + slab
        + [pl.BlockSpec((None, 1, step), lambda l, j: (l, 0, j))],
        out_specs=pl.BlockSpec((None, 8, step), lambda l, j: (l, 0, j)),
        out_shape=jax.ShapeDtypeStruct((depth, 8, n), F32),
        compiler_params=pltpu.CompilerParams(
            dimension_semantics=("arbitrary", "arbitrary"), vmem_limit_bytes=VMEM_LIMIT),
        name="ada_mod",
    )(ct, *([ada_w] * ADA_STREAMS), ada_b.reshape(depth, 1, n))


def _modproj_kernel(x_ref, g_ref, sc_ref, sh_ref, w_ref, cg_ref, w2_ref, cg2_ref,
                    o_ref, ik_ref, sm_ref, hn_ref, *, norm_tiles):
    j = pl.program_id(1)

    @pl.when(j == 0)
    def _():
        h = _modulate(x_ref[...], g_ref[...], sc_ref[...], sh_ref[...]).astype(BF16)
        hn_ref[...] = h
        small = jnp.dot(h, w2_ref[...], preferred_element_type=F32)
        ik_ref[...] = (_rms(small[:, :LANES]) * cg2_ref[:, :LANES]).astype(BF16)
        sm_ref[...] = small[:, LANES:] * cg2_ref[:, LANES:]

    acc = jnp.dot(hn_ref[...], w_ref[...], preferred_element_type=F32)

    for part in range(PROJ_TN // PROJ_SEG):
        seg = j * (PROJ_TN // PROJ_SEG) + part
        cols = slice(part * PROJ_SEG, (part + 1) * PROJ_SEG)
        is_norm = functools.reduce(jnp.logical_or, [seg == t for t in norm_tiles])

        @pl.when(is_norm)
        def _(cols=cols):
            for c in range(cols.start, cols.stop, LANES):
                sl = slice(c, c + LANES)
                o_ref[:, sl] = (_rms(acc[:, sl]) * cg_ref[:, sl]).astype(BF16)

        @pl.when(jnp.logical_not(is_norm))
        def _(cols=cols):
            o_ref[:, cols] = acc[:, cols].astype(BF16)


def _modproj(x2, g, scale, shift, w, cg, w2, cg2, seq, norm_tiles):
    n, d = x2.shape
    ncols = w.shape[1]
    tm = PROJ_TM
    per_b = seq // tm
    kern = functools.partial(_modproj_kernel, norm_tiles=norm_tiles)
    return pl.pallas_call(
        kern,
        grid=(n // tm, ncols // PROJ_TN),
        in_specs=[
            pl.BlockSpec((tm, d), lambda i, j: (i, 0)),
            pl.BlockSpec((1, d), lambda i, j: (0, 0)),
            pl.BlockSpec((None, 1, d), lambda i, j: (i // per_b, 0, 0)),
            pl.BlockSpec((None, 1, d), lambda i, j: (i // per_b, 0, 0)),
            pl.BlockSpec((d, PROJ_TN), lambda i, j: (0, j)),
            pl.BlockSpec((1, PROJ_TN), lambda i, j: (0, j)),
            pl.BlockSpec((d, 2 * LANES), lambda i, j: (0, 0)),
            pl.BlockSpec((1, 2 * LANES), lambda i, j: (0, 0)),
        ],
        out_specs=[
            pl.BlockSpec((tm, PROJ_TN), lambda i, j: (i, j)),
            pl.BlockSpec((tm, LANES), lambda i, j: (i, 0)),
            pl.BlockSpec((tm, LANES), lambda i, j: (i, 0)),
        ],
        out_shape=[
            jax.ShapeDtypeStruct((n, ncols), BF16),
            jax.ShapeDtypeStruct((n, LANES), BF16),
            jax.ShapeDtypeStruct((n, LANES), F32),
        ],
        scratch_shapes=[pltpu.VMEM((tm, d), BF16)],
        compiler_params=pltpu.CompilerParams(
            dimension_semantics=("arbitrary", "arbitrary"), vmem_limit_bytes=VMEM_LIMIT),
        name="modproj",
    )(x2, g, scale, shift, w, cg, w2, cg2)


ATT_TQ = 128
ATT_TK = 256
BISECT_UNCHECKED_BITS = 19


def _sortable(a):
    a = jnp.where(a == 0.0, 0.0, a)
    bits = pltpu.bitcast(a, jnp.int32)
    return bits ^ ((bits >> 31) & 0x7FFFFFFF)


def _attn_kernel(aq_ref, iq_ref, ak_ref, av_ref, ik_ref, iw_ref, o_ref,
                 sc_ref, sct_ref, qs_ref, wb_ref, m_ref, acc_ref, *, topk, seq):
    tq, tk = ATT_TQ, ATT_TK
    group = ATT_HEADS // ATT_KV_HEADS
    qi = pl.program_id(1)
    n_kt = (qi * tq + tq + tk - 1) // tk
    row = qi * tq + lax.broadcasted_iota(jnp.int32, (tq, 1), 0)
    lane_col = lax.broadcasted_iota(jnp.int32, (1, tk), 1)
    for h in range(IDX_HEADS):
        wb_ref[h] = jnp.broadcast_to(iw_ref[:, h:h + 1], (tq, LANES))
    for h in range(ATT_HEADS):
        qs_ref[h * tq:(h + 1) * tq, :] = aq_ref[:, h * HEAD_DIM:(h + 1) * HEAD_DIM]

    n_full = n_kt // 2
    n_pair = (n_kt + 1) // 2

    def score_step(i, width):
        kblk = ik_ref[pl.ds(pl.multiple_of(i * 2 * tk, 2 * tk), width), :]
        acc = jnp.zeros((tq, width), F32)
        for h in range(IDX_HEADS):
            z = _nt_dot(iq_ref[:, h * HEAD_DIM:(h + 1) * HEAD_DIM], kblk)
            wb = wb_ref[h]
            acc = acc + jnp.concatenate([wb] * (width // LANES), axis=1) * jnp.maximum(z, 0.0)
        keys = []
        for u in range(width // tk):
            kt = 2 * i + u
            causal = (kt * tk + lane_col) <= row
            keys.append(jnp.where(causal, _sortable(acc[:, u * tk:(u + 1) * tk]), INT_MIN))
            sc_ref[kt] = keys[u]
        if width == 2 * tk:
            sct_ref[i] = jnp.concatenate(keys, axis=1).T
        else:
            sct_ref[i, 0:tk, :] = keys[0].T
            sct_ref[i, tk:2 * tk, :] = jnp.full((tk, tq), INT_MIN, jnp.int32)
            sc_ref[2 * i + 1] = jnp.full((tq, tk), INT_MIN, jnp.int32)

    def score_pair(i, carry):
        score_step(i, 2 * tk)
        return carry

    lax.fori_loop(0, n_full, score_pair, 0)

    @pl.when(n_kt % 2 == 1)
    def _():
        score_step(n_full, tk)

    def count_where(pred):
        def body(i, c):
            for u in range(2):
                kt = 2 * i + u
                hit = jnp.where(pred(sc_ref[kt], kt), 1.0, 0.0)
                for cc in range(tk // LANES):
                    c = c + hit[:, cc * LANES:(cc + 1) * LANES]
            return c
        c = lax.fori_loop(0, n_pair, body, jnp.zeros((tq, LANES), F32))
        return jnp.sum(c, axis=-1, keepdims=True)

    kf = float(topk)

    def count_ge_t(t):
        def body(i, c):
            hit = jnp.where(sct_ref[i] >= t, 1.0, 0.0)
            return c + jnp.sum(hit.reshape(2 * tk // 64, 8, 8, tq), axis=0)
        c = lax.fori_loop(0, n_pair, body, jnp.zeros((8, 8, tq), F32))
        return jnp.sum(jnp.sum(c, axis=0), axis=0, keepdims=True)

    c0 = count_ge_t(jnp.zeros((1, tq), jnp.int32))
    t0 = jnp.where(c0 >= kf, 0, INT_MIN).astype(jnp.int32)
    n_all = jnp.full((1, tq), float(seq), F32)

    def bit_step(i, state):
        t, ct, open_rows = state
        cand = t | jnp.left_shift(jnp.int32(1), 30 - i)
        c = count_ge_t(cand)
        take = jnp.logical_and(c >= kf, open_rows > 0.0)
        t = jnp.where(take, cand, t)
        ct = jnp.where(take, c, ct)
        open_rows = jnp.where(c == kf, 0.0, open_rows)
        return t, ct, open_rows

    def any_open(open_rows):
        return (jnp.max(open_rows) > 0.0).astype(jnp.int32)

    def tail_cond(state):
        i, _, go = state
        return jnp.logical_and(i < 31, go == 1)

    def tail_step(state):
        i, inner, _ = state
        inner = bit_step(i + 1, bit_step(i, inner))
        return i + 2, inner, any_open(inner[2])

    state = (t0, jnp.where(c0 >= kf, c0, n_all), jnp.where(c0 == kf, 0.0, 1.0))
    state = lax.fori_loop(0, BISECT_UNCHECKED_BITS, bit_step, state)
    _, (thr_t, cnt_t, open_t), _ = lax.while_loop(
        tail_cond, tail_step, (jnp.int32(BISECT_UNCHECKED_BITS), state, any_open(state[2])))

    eye = lax.broadcasted_iota(jnp.int32, (tq, tq), 0) == lax.broadcasted_iota(jnp.int32, (tq, tq), 1)

    def to_col(v):
        return jnp.sum(jnp.where(eye, v, 0.0), axis=1, keepdims=True)

    thr = ((to_col((thr_t >> 16).astype(F32)).astype(jnp.int32) << 16)
           | to_col((thr_t & 0xFFFF).astype(F32)).astype(jnp.int32))
    cnt_thr = to_col(cnt_t)
    open_rows = to_col(open_t)

    excess = jnp.logical_and(jnp.logical_and(open_rows > 0.0, cnt_thr > kf), thr > INT_MIN)

    @pl.when(jnp.max(jnp.where(excess, 1.0, 0.0)) > 0.0)
    def _():
        need = kf - count_where(lambda key, kt: key > thr)

        def ties_below(j):
            return count_where(
                lambda key, kt: jnp.logical_and(key == thr, (kt * tk + lane_col) < j))

        nbits = max(seq - 1, 1).bit_length()

        def idx_step(i, j0):
            cand = j0 | jnp.left_shift(jnp.int32(1), nbits - 1 - i)
            return jnp.where(ties_below(cand) < need, cand, j0)

        j0 = lax.fori_loop(0, nbits, idx_step, jnp.zeros((tq, 1), jnp.int32))

        def demote(kt, carry):
            key = sc_ref[kt]
            late_tie = jnp.logical_and(key == thr, (kt * tk + lane_col) > j0)
            sc_ref[kt] = jnp.where(jnp.logical_and(excess, late_tie), INT_MIN, key)
            return carry

        lax.fori_loop(0, n_kt, demote, 0)

    thr = jnp.maximum(thr, INT_MIN + 1)

    m_ref[...] = jnp.full(m_ref.shape, MASKED, F32)
    acc_ref[...] = jnp.zeros(acc_ref.shape, F32)
    def kv_step(i, tk2):
        r0 = pl.multiple_of(i * 2 * tk, 2 * tk)
        ones = jnp.ones((tk2, HEAD_DIM), BF16)
        keep = jnp.concatenate([sc_ref[2 * i + u] >= thr for u in range(tk2 // tk)], axis=1)
        bias = jnp.where(keep, 0.0, MASKED).astype(BF16)
        for g in range(ATT_KV_HEADS):
            kblk = ak_ref[pl.ds(r0, tk2), g * HEAD_DIM:(g + 1) * HEAD_DIM]
            vblk = av_ref[pl.ds(r0, tk2), g * HEAD_DIM:(g + 1) * HEAD_DIM]
            v1 = jnp.concatenate([vblk, ones], axis=1)
            for part in range(group):
                rows = slice(part * tq, (part + 1) * tq)
                q1 = qs_ref[(g * group + part) * tq:(g * group + part + 1) * tq, :]
                s = _nt_dot(q1, kblk).astype(BF16) + bias
                m_old = m_ref[g, rows, :]
                m_new = jnp.maximum(m_old, jnp.max(s, axis=-1, keepdims=True).astype(F32))
                alpha = jnp.exp2(m_old - m_new)
                shift = m_new.astype(BF16)
                p = jnp.exp2(s - jnp.concatenate([shift] * (tk2 // LANES), axis=1))
                pv = jnp.dot(p, v1, preferred_element_type=F32)
                acc_ref[g, rows, :] = jnp.concatenate([alpha, alpha], axis=1) * acc_ref[g, rows, :] + pv
                m_ref[g, rows, :] = m_new

    def kv_pair(i, carry):
        kv_step(i, 2 * tk)
        return carry

    lax.fori_loop(0, n_full, kv_pair, 0)

    @pl.when(n_kt % 2 == 1)
    def _():
        kv_step(n_full, tk)

    for g in range(ATT_KV_HEADS):
        og = acc_ref[g]
        og = og[:, :HEAD_DIM] / og[:, HEAD_DIM:]
        for r in range(group):
            h = g * group + r
            o_ref[:, h * HEAD_DIM:(h + 1) * HEAD_DIM] = og[r * tq:(r + 1) * tq, :].astype(BF16)


def _attention(proj, ik, sm, batch, seq, cols):
    n = proj.shape[0]
    tq, tk = ATT_TQ, ATT_TK
    nq = seq // tq
    width = ATT_HEADS * HEAD_DIM
    kvw = ATT_KV_HEADS * HEAD_DIM
    topk = min(TOPK_MAX, seq // 4)
    group = ATT_HEADS // ATT_KV_HEADS
    kern = functools.partial(_attn_kernel, topk=topk, seq=seq)
    once = pl.Buffered(1)
    return pl.pallas_call(
        kern,
        grid=(batch, nq),
        in_specs=[
            pl.BlockSpec((tq, width), lambda b, q: (b * nq + q, cols["aq"] // width)),
            pl.BlockSpec((tq, width), lambda b, q: (b * nq + q, cols["iq"] // width)),
            pl.BlockSpec((seq, kvw), lambda b, q: (b, cols["ak"] // kvw), pipeline_mode=once),
            pl.BlockSpec((seq, kvw), lambda b, q: (b, cols["av"] // kvw), pipeline_mode=once),
            pl.BlockSpec((seq, LANES), lambda b, q: (b, 0), pipeline_mode=once),
            pl.BlockSpec((tq, LANES), lambda b, q: (b * nq + q, 0)),
        ],
        out_specs=pl.BlockSpec((tq, width), lambda b, q: (b * nq + q, 0)),
        out_shape=jax.ShapeDtypeStruct((n, width), BF16),
        scratch_shapes=[
            pltpu.VMEM((seq // tk, tq, tk), jnp.int32),
            pltpu.VMEM((seq // (2 * tk), 2 * tk, tq), jnp.int32),
            pltpu.VMEM((ATT_HEADS * tq, HEAD_DIM), BF16),
            pltpu.VMEM((IDX_HEADS, tq, LANES), F32),
            pltpu.VMEM((ATT_KV_HEADS, group * tq, LANES), F32),
            pltpu.VMEM((ATT_KV_HEADS, group * tq, 2 * HEAD_DIM), F32),
        ],
        compiler_params=pltpu.CompilerParams(
            dimension_semantics=("arbitrary", "arbitrary"), vmem_limit_bytes=VMEM_LIMIT),
        name="dsa_attention",
    )(proj, proj, proj, proj, ik, sm)


GLA_T = 256


def _gla_kernel(q_ref, k_ref, v_ref, r_ref, lr_ref, wg_ref, bg_ref, gn_ref, o_ref, st_ref, la_ref,
                *, dk, dv):
    c_len = GLA_CHUNK
    hi = lax.Precision.HIGHEST

    @pl.when(pl.program_id(1) == 0)
    def _():
        st_ref[...] = jnp.zeros(st_ref.shape, F32)

    ri = lax.broadcasted_iota(jnp.int32, (c_len, c_len), 0)
    ci = lax.broadcasted_iota(jnp.int32, (c_len, c_len), 1)
    lower = ri >= ci
    tril = jnp.where(lower, 1.0, 0.0).astype(F32)
    qscale = dk ** -0.5

    gl = jnp.dot(lr_ref[...], wg_ref[...], preferred_element_type=F32, precision=hi) + bg_ref[...]
    la_ref[...] = (jnp.minimum(gl, 0.0) - jnp.log(1.0 + jnp.exp(-jnp.abs(gl)))) / GLA_GATE_TAU

    for c in range(GLA_T // c_len):
        rows = slice(c * c_len, (c + 1) * c_len)
        b = jnp.dot(tril, la_ref[rows, :], preferred_element_type=F32, precision=hi)
        b_last = b[c_len - 1:c_len, :]
        k = k_ref[rows, :].astype(F32)
        q_dec = (q_ref[rows, :].astype(F32) * qscale * jnp.exp(b)).astype(BF16)
        k_inv = (k * jnp.exp(-b)).astype(BF16)
        k_tail = k * jnp.exp(b_last - b)
        for h in range(GLA_HEADS):
            ks = slice(h * dk, (h + 1) * dk)
            vs = slice(h * dv, (h + 1) * dv)
            v = v_ref[rows, vs]
            a = jnp.where(lower, _nt_dot(q_dec[:, ks], k_inv[:, ks]), 0.0)
            state = st_ref[h]
            o = (jnp.dot(a.astype(BF16), v, preferred_element_type=F32)
                 + jnp.dot(q_dec[:, ks], state.astype(BF16), preferred_element_type=F32))
            dec = jnp.exp(b[:, ks].T[:, c_len - 1:c_len])
            upd = jnp.dot(k_tail[:, ks].T.astype(BF16), v, preferred_element_type=F32)
            for cc in range(dv // LANES):
                sl = slice(cc * LANES, (cc + 1) * LANES)
                st_ref[h, :, sl] = dec * state[:, sl] + upd[:, sl]
            r = r_ref[rows, vs].astype(F32)
            o_ref[rows, vs] = (_rms(o) * gn_ref[...] * (r * _sigmoid(r))).astype(BF16)


def _gla(proj, sm, wg, bg, gn, batch, seq, cols):
    n = proj.shape[0]
    dk = wg.shape[1] // GLA_HEADS
    dv = gn.shape[1]
    t = min(GLA_T, seq)
    assert t == GLA_T
    ns = seq // t
    kw, vw = GLA_HEADS * dk, GLA_HEADS * dv
    kern = functools.partial(_gla_kernel, dk=dk, dv=dv)
    return pl.pallas_call(
        kern,
        grid=(batch, ns),
        in_specs=[
            pl.BlockSpec((t, kw), lambda b, s: (b * ns + s, cols["gq"] // kw)),
            pl.BlockSpec((t, kw), lambda b, s: (b * ns + s, cols["gk"] // kw)),
            pl.BlockSpec((t, vw), lambda b, s: (b * ns + s, cols["gv"] // vw)),
            pl.BlockSpec((t, vw), lambda b, s: (b * ns + s, cols["gr"] // vw)),
            pl.BlockSpec((t, LANES), lambda b, s: (b * ns + s, 0)),
            pl.BlockSpec((LANES, kw), lambda b, s: (0, 0)),
            pl.BlockSpec((1, kw), lambda b, s: (0, 0)),
            pl.BlockSpec((1, dv), lambda b, s: (0, 0)),
        ],
        out_specs=pl.BlockSpec((t, vw), lambda b, s: (b * ns + s, 0)),
        out_shape=jax.ShapeDtypeStruct((n, vw), BF16),
        scratch_shapes=[pltpu.VMEM((GLA_HEADS, dk, dv), F32), pltpu.VMEM((t, kw), F32)],
        compiler_params=pltpu.CompilerParams(
            dimension_semantics=("arbitrary", "arbitrary"),
            vmem_limit_bytes=VMEM_LIMIT),
        name="gla",
    )(proj, proj, proj, proj, sm, wg, bg, gn)


def _merge_kernel(ya_ref, yg_ref, wa_ref, wg_ref, ma_ref, mg_ref, o_ref):
    a = jnp.dot(ya_ref[...], wa_ref[...], preferred_element_type=F32)
    g = jnp.dot(yg_ref[...], wg_ref[...], preferred_element_type=F32)
    gate_a = _sigmoid(ma_ref[...].astype(F32))
    gate_g = _sigmoid(mg_ref[...].astype(F32))
    o_ref[...] = (gate_a * a + gate_g * g).astype(BF16)


def _merge(y_att, y_gla, w_att, w_gla, proj, cols):
    n, d_att = y_att.shape
    d_gla = y_gla.shape[1]
    d = w_att.shape[1]
    tm, tn = PROJ_TM, PROJ_TN
    return pl.pallas_call(
        _merge_kernel,
        grid=(n // tm, d // tn),
        in_specs=[
            pl.BlockSpec((tm, d_att), lambda i, j: (i, 0)),
            pl.BlockSpec((tm, d_gla), lambda i, j: (i, 0)),
            pl.BlockSpec((d_att, tn), lambda i, j: (0, j)),
            pl.BlockSpec((d_gla, tn), lambda i, j: (0, j)),
            pl.BlockSpec((tm, tn), lambda i, j: (i, cols["m_att"] // tn + j)),
            pl.BlockSpec((tm, tn), lambda i, j: (i, cols["m_gla"] // tn + j)),
        ],
        out_specs=pl.BlockSpec((tm, tn), lambda i, j: (i, j)),
        out_shape=jax.ShapeDtypeStruct((n, d), BF16),
        compiler_params=pltpu.CompilerParams(
            dimension_semantics=("arbitrary", "arbitrary"), vmem_limit_bytes=VMEM_LIMIT),
        name="merge",
    )(y_att, y_gla, w_att, w_gla, proj, proj)


def _resproj_kernel(m_ref, w_ref, x_ref, gate_ref, o_ref):
    y = jnp.dot(m_ref[...], w_ref[...], preferred_element_type=F32)
    o_ref[...] = x_ref[...] + gate_ref[...] * y


def _resproj(merged, w, x2, gate, seq):
    n, d = x2.shape
    tm, tn = PROJ_TM, PROJ_TN
    per_b = seq // tm
    return pl.pallas_call(
        _resproj_kernel,
        grid=(n // tm, d // tn),
        in_specs=[
            pl.BlockSpec((tm, merged.shape[1]), lambda i, j: (i, 0)),
            pl.BlockSpec((merged.shape[1], tn), lambda i, j: (0, j)),
            pl.BlockSpec((tm, tn), lambda i, j: (i, j)),
            pl.BlockSpec((None, 1, tn), lambda i, j: (i // per_b, 0, j)),
        ],
        out_specs=pl.BlockSpec((tm, tn), lambda i, j: (i, j)),
        out_shape=jax.ShapeDtypeStruct((n, d), F32),
        compiler_params=pltpu.CompilerParams(
            dimension_semantics=("arbitrary", "arbitrary"), vmem_limit_bytes=VMEM_LIMIT),
        name="resproj",
    )(merged, w, x2, gate)


FFN_TM = 512
FFN_TF = 1024


def _swiglu_step(hn, w1, w3, w2):
    a = jnp.dot(hn, w1, preferred_element_type=F32)
    b = jnp.dot(hn, w3, preferred_element_type=F32)
    return a * _sigmoid(a) * b, w2


def _ffn_kernel(x_ref, g_ref, sc_ref, sh_ref, w1_ref, w3_ref, w2_ref, gate_ref, o_ref,
                hn_ref, acc_ref):
    j = pl.program_id(1)

    @pl.when(j == 0)
    def _():
        hn_ref[...] = _modulate(x_ref[...], g_ref[...], sc_ref[...], sh_ref[...]).astype(BF16)
        acc_ref[...] = jnp.zeros(acc_ref.shape, F32)

    act, w2 = _swiglu_step(hn_ref[...], w1_ref[...], w3_ref[...], w2_ref[...])
    acc_ref[...] += jnp.dot(act.astype(BF16), w2, preferred_element_type=F32)

    @pl.when(j == pl.num_programs(1) - 1)
    def _():
        o_ref[...] = x_ref[...] + gate_ref[...] * acc_ref[...]


def _ffn(x2, g, scale, shift, w1, w3, w2, gate, seq):
    n, d = x2.shape
    dff = w1.shape[1]
    tm, tf = min(FFN_TM, seq), FFN_TF
    per_b = seq // tm
    vec = pl.BlockSpec((None, 1, d), lambda i, j: (i // per_b, 0, 0))
    return pl.pallas_call(
        _ffn_kernel,
        grid=(n // tm, dff // tf),
        in_specs=[
            pl.BlockSpec((tm, d), lambda i, j: (i, 0)),
            pl.BlockSpec((1, d), lambda i, j: (0, 0)),
            vec, vec,
            pl.BlockSpec((d, tf), lambda i, j: (0, j)),
            pl.BlockSpec((d, tf), lambda i, j: (0, j)),
            pl.BlockSpec((tf, d), lambda i, j: (j, 0)),
            vec,
        ],
        out_specs=pl.BlockSpec((tm, d), lambda i, j: (i, 0)),
        out_shape=jax.ShapeDtypeStruct((n, d), F32),
        scratch_shapes=[pltpu.VMEM((tm, d), BF16), pltpu.VMEM((tm, d), F32)],
        compiler_params=pltpu.CompilerParams(
            dimension_semantics=("arbitrary", "arbitrary"), vmem_limit_bytes=VMEM_LIMIT),
        name="ffn_swiglu",
    )(x2, g, scale, shift, w1, w3, w2, gate)


MOE_TM = 512
ROUTE_TM = 1024


def _router_kernel(x_ref, g_ref, sc_ref, sh_ref, wr_ref, hn_ref, rt_ref):
    lane = lax.broadcasted_iota(jnp.int32, (1, LANES), 1)
    lane_f = lane.astype(F32)
    h = _modulate(x_ref[...], g_ref[...], sc_ref[...], sh_ref[...])
    hn_ref[...] = h
    logits = jnp.dot(h, wr_ref[...], preferred_element_type=F32, precision=lax.Precision.HIGHEST)
    logits = jnp.where(lane < N_EXPERTS, logits, -jnp.inf)
    m1 = jnp.max(logits, axis=-1, keepdims=True)
    i1 = jnp.min(jnp.where(logits == m1, lane_f, float(LANES)), axis=-1, keepdims=True)
    rest = jnp.where(lane_f == i1, -jnp.inf, logits)
    m2 = jnp.max(rest, axis=-1, keepdims=True)
    i2 = jnp.min(jnp.where(rest == m2, lane_f, float(LANES)), axis=-1, keepdims=True)
    e2 = jnp.exp(m2 - m1)
    p1 = 1.0 / (1.0 + e2)
    p2 = e2 / (1.0 + e2)
    rt_ref[...] = (jnp.where(lane == 0, i1, 0.0) + jnp.where(lane == 1, i2, 0.0)
                   + jnp.where(lane == 2, p1, 0.0) + jnp.where(lane == 3, p2, 0.0))


def _router(x2, g, scale, shift, w_router, seq):
    n, d = x2.shape
    tm = min(ROUTE_TM, seq)
    per_b = seq // tm
    vec = pl.BlockSpec((None, 1, d), lambda i: (i // per_b, 0, 0))
    return pl.pallas_call(
        _router_kernel,
        grid=(n // tm,),
        in_specs=[
            pl.BlockSpec((tm, d), lambda i: (i, 0)),
            pl.BlockSpec((1, d), lambda i: (0, 0)),
            vec, vec,
            pl.BlockSpec((d, LANES), lambda i: (0, 0)),
        ],
        out_specs=[pl.BlockSpec((tm, d), lambda i: (i, 0)),
                   pl.BlockSpec((tm, LANES), lambda i: (i, 0))],
        out_shape=[jax.ShapeDtypeStruct((n, d), F32), jax.ShapeDtypeStruct((n, LANES), F32)],
        compiler_params=pltpu.CompilerParams(
            dimension_semantics=("arbitrary",), vmem_limit_bytes=VMEM_LIMIT),
        name="moe_router",
    )(x2, g, scale, shift, w_router)


def _route_tables(rt, n, n_exp, tm):
    rows = 2 * n + n_exp * tm
    n_tiles = rows // tm
    ef = rt[:, :2].astype(jnp.int32).T.reshape(-1)
    onehot = (ef[:, None] == jnp.arange(n_exp, dtype=jnp.int32)[None, :]).astype(jnp.int32)
    csum = jnp.cumsum(onehot, axis=0)
    rank = jnp.take_along_axis(csum, ef[:, None], axis=1)[:, 0] - 1
    cnt = csum[-1]
    padded = ((cnt + tm - 1) // tm) * tm
    ends = jnp.cumsum(padded)
    dest = (ends - padded)[ef] + rank
    total = ends[-1]
    tile_start = jnp.arange(n_tiles, dtype=jnp.int32) * tm
    tile_valid = (tile_start < total).astype(jnp.int32)
    tile_expert = jnp.sum((tile_start[:, None] >= ends[None, :]).astype(jnp.int32), axis=1)
    last_tile = jnp.maximum(total // tm - 1, 0)
    tile_expert = jnp.where(tile_valid == 1, tile_expert, tile_expert[last_tile])
    tile_rows = jnp.where(tile_valid == 1, jnp.arange(n_tiles, dtype=jnp.int32), last_tile)
    spare_lo = jnp.concatenate([ends - padded + cnt, total[None]]).astype(jnp.int32)
    spare_hi = jnp.concatenate([ends, jnp.full((1,), rows, ends.dtype)]).astype(jnp.int32)
    return dict(tile_expert=tile_expert, tile_valid=tile_valid, tile_rows=tile_rows, dest=dest,
                spare_lo=spare_lo, spare_hi=spare_hi, rows=rows)


def _row_copies(copy, count):
    def body(r, carry):
        copy(r).start()
        return carry
    lax.fori_loop(0, count, body, 0, unroll=8)


def _dispatch_kernel(dest_ref, lo_ref, hi_ref, hn_ref, xs_hbm, zero_ref, sem, zsem, *, n, n_spare):
    i = pl.program_id(0)
    tm = hn_ref.shape[0]

    @pl.when(i == 0)
    def _():
        zero_ref[...] = jnp.zeros(zero_ref.shape, F32)
        for e in range(n_spare):
            def zero_row(r):
                return pltpu.make_async_copy(zero_ref.at[pl.ds(0, 1)], xs_hbm.at[pl.ds(r, 1)], zsem)

            def start(r, carry):
                zero_row(r).start()
                return carry

            def drain(r, carry):
                zero_row(r).wait()
                return carry
            lax.fori_loop(lo_ref[e], hi_ref[e], start, 0)
            lax.fori_loop(lo_ref[e], hi_ref[e], drain, 0)

    for k in range(2):
        _row_copies(lambda r, k=k: pltpu.make_async_copy(
            hn_ref.at[pl.ds(r, 1)], xs_hbm.at[pl.ds(dest_ref[k * n + i * tm + r], 1)], sem), tm)
    for k in range(2):
        pltpu.make_async_copy(hn_ref, xs_hbm.at[pl.ds(0, tm)], sem).wait()


def _dispatch(hn, tables):
    n, d = hn.shape
    tm = MOE_TM
    kern = functools.partial(_dispatch_kernel, n=n, n_spare=tables["spare_lo"].shape[0])
    grid_spec = pltpu.PrefetchScalarGridSpec(
        num_scalar_prefetch=3,
        grid=(n // tm,),
        in_specs=[pl.BlockSpec((tm, d), lambda i, dest, lo, hi: (i, 0))],
        out_specs=pl.BlockSpec(memory_space=pl.ANY),
        scratch_shapes=[pltpu.VMEM((8, d), F32), pltpu.SemaphoreType.DMA(()),
                        pltpu.SemaphoreType.DMA(())],
    )
    return pl.pallas_call(
        kern,
        grid_spec=grid_spec,
        out_shape=jax.ShapeDtypeStruct((tables["rows"], d), F32),
        compiler_params=pltpu.CompilerParams(
            dimension_semantics=("arbitrary",), vmem_limit_bytes=VMEM_LIMIT),
        name="moe_dispatch",
    )(tables["dest"], tables["spare_lo"], tables["spare_hi"], hn)


def _moe_grouped_kernel(te_ref, tv_ref, tr_ref, x_ref, w1_ref, w3_ref, w2_ref, y_ref, xb_ref):
    t = pl.program_id(0)
    j = pl.program_id(1)
    valid = tv_ref[t] == 1

    @pl.when(j == 0)
    def _():
        xb_ref[...] = x_ref[...].astype(BF16)
        y_ref[...] = jnp.zeros(y_ref.shape, F32)

    @pl.when(valid)
    def _():
        act, w2 = _swiglu_step(xb_ref[...], w1_ref[...], w3_ref[...], w2_ref[...])
        y_ref[...] += jnp.dot(act.astype(BF16), w2, preferred_element_type=F32)


def _moe_grouped(xs, tables, w1, w3, w2):
    rows, d = xs.shape
    dff = w1.shape[2]
    tm, tf = MOE_TM, FFN_TF
    nj = dff // tf

    def jj(t, j, tv):
        return j * tv[t] + (nj - 1) * (1 - tv[t])

    grid_spec = pltpu.PrefetchScalarGridSpec(
        num_scalar_prefetch=3,
        grid=(rows // tm, nj),
        in_specs=[
            pl.BlockSpec((tm, d), lambda t, j, te, tv, tr: (tr[t], 0)),
            pl.BlockSpec((None, d, tf), lambda t, j, te, tv, tr: (te[t], 0, jj(t, j, tv))),
            pl.BlockSpec((None, d, tf), lambda t, j, te, tv, tr: (te[t], 0, jj(t, j, tv))),
            pl.BlockSpec((None, tf, d), lambda t, j, te, tv, tr: (te[t], jj(t, j, tv), 0)),
        ],
        out_specs=pl.BlockSpec((tm, d), lambda t, j, te, tv, tr: (t, 0)),
        scratch_shapes=[pltpu.VMEM((tm, d), BF16)],
    )
    return pl.pallas_call(
        _moe_grouped_kernel,
        grid_spec=grid_spec,
        out_shape=jax.ShapeDtypeStruct((rows, d), F32),
        compiler_params=pltpu.CompilerParams(
            dimension_semantics=("arbitrary", "arbitrary"), vmem_limit_bytes=VMEM_LIMIT),
        name="moe_grouped",
    )(tables["tile_expert"], tables["tile_valid"], tables["tile_rows"], xs, w1, w3, w2)


def _moe_combine_kernel(dest_ref, x_ref, rt_ref, gate_ref, y_hbm, o_ref, y0_ref, y1_ref, sem, *, n):
    i = pl.program_id(0)
    tm = x_ref.shape[0]
    for k, yk_ref in enumerate((y0_ref, y1_ref)):
        _row_copies(lambda r, k=k, yk_ref=yk_ref: pltpu.make_async_copy(
            y_hbm.at[pl.ds(dest_ref[k * n + i * tm + r], 1)], yk_ref.at[pl.ds(r, 1)], sem), tm)
    for yk_ref in (y0_ref, y1_ref):
        pltpu.make_async_copy(y_hbm.at[pl.ds(0, tm)], yk_ref, sem).wait()
    p1 = rt_ref[:, 2:3]
    p2 = rt_ref[:, 3:4]
    o_ref[...] = x_ref[...] + gate_ref[...] * (p1 * y0_ref[...] + p2 * y1_ref[...])


def _moe_combine(x2, y, rt, gate, dest, seq):
    n, d = x2.shape
    tm = min(FFN_TM, seq)
    per_b = seq // tm
    kern = functools.partial(_moe_combine_kernel, n=n)
    grid_spec = pltpu.PrefetchScalarGridSpec(
        num_scalar_prefetch=1,
        grid=(n // tm,),
        in_specs=[
            pl.BlockSpec((tm, d), lambda i, dest: (i, 0)),
            pl.BlockSpec((tm, LANES), lambda i, dest: (i, 0)),
            pl.BlockSpec((None, 1, d), lambda i, dest: (i // per_b, 0, 0)),
            pl.BlockSpec(memory_space=pl.ANY),
        ],
        out_specs=pl.BlockSpec((tm, d), lambda i, dest: (i, 0)),
        scratch_shapes=[pltpu.VMEM((tm, d), F32), pltpu.VMEM((tm, d), F32),
                        pltpu.SemaphoreType.DMA(())],
    )
    return pl.pallas_call(
        kern,
        grid_spec=grid_spec,
        out_shape=jax.ShapeDtypeStruct((n, d), F32),
        compiler_params=pltpu.CompilerParams(
            dimension_semantics=("arbitrary",), vmem_limit_bytes=VMEM_LIMIT),
        name="moe_combine",
    )(dest, x2, rt, gate, y)


def _moe(x2, g, scale, shift, w_router, w1, w3, w2, gate, seq):
    n = x2.shape[0]
    hn, rt = _router(x2, g, scale, shift, w_router, seq)
    tables = _route_tables(rt, n, w1.shape[0], MOE_TM)
    xs = _dispatch(hn, tables)
    y = _moe_grouped(xs, tables, w1, w3, w2)
    return _moe_combine(x2, y, rt, gate, tables["dest"], seq)


def _prep_in_proj(w_in, q_norm, k_norm, idx_k_norm, d):
    sizes = dict(aq=ATT_HEADS * HEAD_DIM, ak=ATT_KV_HEADS * HEAD_DIM, av=ATT_KV_HEADS * HEAD_DIM,
                 iq=IDX_HEADS * HEAD_DIM, ik=HEAD_DIM, iw=IDX_HEADS,
                 gq=d // 2, gk=d // 2, gv=d, glr=GLA_GATE_RANK, gr=d, m_att=d, m_gla=d)
    src, acc = {}, 0
    for name in ("aq", "ak", "av", "iq", "ik", "iw", "gq", "gk", "gv", "glr", "gr", "m_att", "m_gla"):
        src[name] = (acc, sizes[name])
        acc += sizes[name]
    order = ("aq", "iq", "gv", "gr", "m_att", "m_gla", "ak", "av", "gq", "gk")
    cols, parts, off = {}, [], 0
    for name in order:
        s, width = src[name]
        cols[name] = off
        parts.append(w_in[:, s:s + width])
        off += width
    w_main = jnp.concatenate(parts, axis=1).astype(BF16)

    def seg(name):
        s, width = src[name]
        return w_in[:, s:s + width]

    pad = jnp.zeros((d, LANES - IDX_HEADS - GLA_GATE_RANK), w_in.dtype)
    w_small = jnp.concatenate([seg("ik"), seg("iw"), seg("glr"), pad], axis=1).astype(BF16)

    att_scale = HEAD_DIM ** -0.5 * LOG2_E
    idx_scale = (HEAD_DIM ** -0.5) * (IDX_HEADS ** -0.5)
    cg = jnp.ones((off,), F32)
    cg = cg.at[cols["aq"]:cols["aq"] + sizes["aq"]].set(jnp.tile(q_norm * att_scale, ATT_HEADS))
    cg = cg.at[cols["ak"]:cols["ak"] + sizes["ak"]].set(jnp.tile(k_norm, ATT_KV_HEADS))
    cg2 = jnp.concatenate([idx_k_norm, jnp.full((IDX_HEADS,), idx_scale, F32),
                           jnp.ones((LANES - IDX_HEADS,), F32)])
    norm_tiles = tuple(range(cols["aq"] // PROJ_SEG, (cols["aq"] + sizes["aq"]) // PROJ_SEG)) + \
        tuple(range(cols["ak"] // PROJ_SEG, (cols["ak"] + sizes["ak"]) // PROJ_SEG))
    return w_main, w_small, cg.reshape(1, -1), cg2.reshape(1, -1), cols, norm_tiles


def kernel(x, c, ada_w, ada_b, norm_mix, norm_ffn, w_in, q_norm, k_norm, idx_k_norm, w_gla_gate,
           b_gla_gate, gla_out_norm, w_out_attn, w_out_gla, w_out, ffn_w1, ffn_w3, ffn_w2,
           moe_router, moe_w1, moe_w3, moe_w2):
    batch, seq, d = x.shape
    depth = ada_w.shape[0]
    n = batch * seq
    for tile in (PROJ_TM, 2 * ATT_TK, GLA_T, ROUTE_TM, FFN_TM, MOE_TM):
        assert seq % tile == 0, (seq, tile)
    assert d % PROJ_TN == 0
    x2 = x.reshape(n, d)

    mod_all = _ada(c, ada_w, ada_b)

    for layer in range(depth):
        mod = mod_all[layer, :batch].reshape(batch, 6, 1, d)
        shift_m, scale_m, gate_m = mod[:, 0], mod[:, 1], mod[:, 2]
        shift_f, scale_f, gate_f = mod[:, 3], mod[:, 4], mod[:, 5]

        w_main, w_small, cg, cg2, cols, norm_tiles = _prep_in_proj(
            w_in[layer], q_norm[layer], k_norm[layer], idx_k_norm[layer], d)
        proj, ik, sm = _modproj(x2, norm_mix[layer].reshape(1, d), scale_m, shift_m,
                                w_main, cg, w_small, cg2, seq, norm_tiles)
        y_att = _attention(proj, ik, sm, batch, seq, cols)
        wg = jnp.zeros((LANES, w_gla_gate.shape[2]), F32).at[
            GLA_GATE_RANK:2 * GLA_GATE_RANK].set(w_gla_gate[layer])
        y_gla = _gla(proj, sm, wg, b_gla_gate[layer].reshape(1, -1),
                     gla_out_norm[layer].reshape(1, -1), batch, seq, cols)
        merged = _merge(y_att, y_gla, w_out_attn[layer].astype(BF16),
                        w_out_gla[layer].astype(BF16), proj, cols)
        x2 = _resproj(merged, w_out[layer].astype(BF16), x2, gate_m, seq)

        g_f = norm_ffn[layer].reshape(1, d)
        i = layer // 2
        if layer % 2 == 0:
            x2 = _ffn(x2, g_f, scale_f, shift_f, ffn_w1[i].astype(BF16), ffn_w3[i].astype(BF16),
                      ffn_w2[i].astype(BF16), gate_f, seq)
        else:
            w_r = jnp.zeros((d, LANES), F32).at[:, :N_EXPERTS].set(moe_router[i])
            x2 = _moe(x2, g_f, scale_f, shift_f, w_r, moe_w1[i].astype(BF16),
                      moe_w3[i].astype(BF16), moe_w2[i].astype(BF16), gate_f, seq)
    return x2.reshape(batch, seq, d)
```

```python
import functools

import jax
import jax.numpy as jnp
from jax import lax
from jax.experimental import pallas as pl
from jax.experimental.pallas import tpu as pltpu

F32 = jnp.float32
BF16 = jnp.bfloat16

ATT_HEADS = 16
ATT_KV_HEADS = 4
HEAD_DIM = 128
IDX_HEADS = 16
TOPK_MAX = 256
GLA_HEADS = 4
GLA_GATE_RANK = 16
GLA_GATE_TAU = 16.0
GLA_CHUNK = 64
N_EXPERTS = 8
EPS = 1e-6
LANES = 128
MASKED = -(2.0 ** 100)
INT_MIN = -(2 ** 31)
LOG2_E = 1.4426950408889634

PROJ_SEG = 512
PROJ_TN = 1024
PROJ_TM = 1024
VMEM_LIMIT = 56 * 1024 * 1024


def _nt_dot(a, b):
    return lax.dot_general(a, b, (((1,), (1,)), ((), ())), preferred_element_type=F32)


def _rms(a):
    return a * lax.rsqrt(jnp.mean(a * a, axis=-1, keepdims=True) + EPS)


def _sigmoid(a):
    return 1.0 / (1.0 + jnp.exp(-a))


def _modulate(x, g, scale, shift):
    return _rms(x) * g * (1.0 + scale) + shift


ADA_TN = 1024
ADA_STREAMS = 3


def _ada_kernel(ct_ref, *refs, batch):
    w_refs, b_ref, o_ref = refs[:ADA_STREAMS], refs[ADA_STREAMS], refs[ADA_STREAMS + 1]
    ct = ct_ref[...]
    cond = ct * _sigmoid(ct)
    o_ref[...] = jnp.zeros(o_ref.shape, F32)
    for s, w_ref in enumerate(w_refs):
        cols = slice(s * ADA_TN, (s + 1) * ADA_TN)
        w = w_ref[...]
        for b in range(batch):
            o_ref[b:b + 1, cols] = (jnp.sum(w * cond[:, b:b + 1], axis=0, keepdims=True)
                                    + b_ref[:, cols])


def _ada(c, ada_w, ada_b):
    depth, d, n = ada_w.shape
    batch = c.shape[0]
    step = ADA_TN * ADA_STREAMS
    assert batch <= 8 and n % step == 0
    ct = jnp.zeros((d, LANES), F32).at[:, :batch].set(c.T)
    slab = [pl.BlockSpec((None, d, ADA_TN), lambda l, j, s=s: (l, 0, ADA_STREAMS * j + s))
            for s in range(ADA_STREAMS)]
    return pl.pallas_call(
        functools.partial(_ada_kernel, batch=batch),
        grid=(depth, n // step),
        in_specs=[pl.BlockSpec((d, LANES), lambda l, j: (0, 0))] + slab
        + [pl.BlockSpec((None, 1, step), lambda l, j: (l, 0, j))],
        out_specs=pl.BlockSpec((None, 8, step), lambda l, j: (l, 0, j)),
        out_shape=jax.ShapeDtypeStruct((depth, 8, n), F32),
        compiler_params=pltpu.CompilerParams(
            dimension_semantics=("arbitrary", "arbitrary"), vmem_limit_bytes=VMEM_LIMIT),
        name="ada_mod",
    )(ct, *([ada_w] * ADA_STREAMS), ada_b.reshape(depth, 1, n))


def _modproj_kernel(x_ref, g_ref, sc_ref, sh_ref, w_ref, cg_ref, w2_ref, cg2_ref,
                    o_ref, ik_ref, sm_ref, hn_ref, *, norm_tiles):
    j = pl.program_id(1)

    @pl.when(j == 0)
    def _():
        h = _modulate(x_ref[...], g_ref[...], sc_ref[...], sh_ref[...]).astype(BF16)
        hn_ref[...] = h
        small = jnp.dot(h, w2_ref[...], preferred_element_type=F32)
        ik_ref[...] = (_rms(small[:, :LANES]) * cg2_ref[:, :LANES]).astype(BF16)
        sm_ref[...] = small[:, LANES:] * cg2_ref[:, LANES:]

    acc = jnp.dot(hn_ref[...], w_ref[...], preferred_element_type=F32)

    for part in range(PROJ_TN // PROJ_SEG):
        seg = j * (PROJ_TN // PROJ_SEG) + part
        cols = slice(part * PROJ_SEG, (part + 1) * PROJ_SEG)
        is_norm = functools.reduce(jnp.logical_or, [seg == t for t in norm_tiles])

        @pl.when(is_norm)
        def _(cols=cols):
            for c in range(cols.start, cols.stop, LANES):
                sl = slice(c, c + LANES)
                o_ref[:, sl] = (_rms(acc[:, sl]) * cg_ref[:, sl]).astype(BF16)

        @pl.when(jnp.logical_not(is_norm))
        def _(cols=cols):
            o_ref[:, cols] = acc[:, cols].astype(BF16)


def _modproj(x2, g, scale, shift, w, cg, w2, cg2, seq, norm_tiles):
    n, d = x2.shape
    ncols = w.shape[1]
    tm = PROJ_TM
    per_b = seq // tm
    kern = functools.partial(_modproj_kernel, norm_tiles=norm_tiles)
    return pl.pallas_call(
        kern,
        grid=(n // tm, ncols // PROJ_TN),
        in_specs=[
            pl.BlockSpec((tm, d), lambda i, j: (i, 0)),
            pl.BlockSpec((1, d), lambda i, j: (0, 0)),
            pl.BlockSpec((None, 1, d), lambda i, j: (i // per_b, 0, 0)),
            pl.BlockSpec((None, 1, d), lambda i, j: (i // per_b, 0, 0)),
            pl.BlockSpec((d, PROJ_TN), lambda i, j: (0, j)),
            pl.BlockSpec((1, PROJ_TN), lambda i, j: (0, j)),
            pl.BlockSpec((d, 2 * LANES), lambda i, j: (0, 0)),
            pl.BlockSpec((1, 2 * LANES), lambda i, j: (0, 0)),
        ],
        out_specs=[
            pl.BlockSpec((tm, PROJ_TN), lambda i, j: (i, j)),
            pl.BlockSpec((tm, LANES), lambda i, j: (i, 0)),
            pl.BlockSpec((tm, LANES), lambda i, j: (i, 0)),
        ],
        out_shape=[
            jax.ShapeDtypeStruct((n, ncols), BF16),
            jax.ShapeDtypeStruct((n, LANES), BF16),
            jax.ShapeDtypeStruct((n, LANES), F32),
        ],
        scratch_shapes=[pltpu.VMEM((tm, d), BF16)],
        compiler_params=pltpu.CompilerParams(
            dimension_semantics=("arbitrary", "arbitrary"), vmem_limit_bytes=VMEM_LIMIT),
        name="modproj",
    )(x2, g, scale, shift, w, cg, w2, cg2)


ATT_TQ = 128
ATT_TK = 256
BISECT_UNCHECKED_BITS = 19


def _sortable(a):
    a = jnp.where(a == 0.0, 0.0, a)
    bits = pltpu.bitcast(a, jnp.int32)
    return bits ^ ((bits >> 31) & 0x7FFFFFFF)


def _attn_kernel(aq_ref, iq_ref, ak_ref, av_ref, ik_ref, iw_ref, o_ref,
                 sc_ref, sct_ref, qs_ref, wb_ref, m_ref, acc_ref, *, topk, seq):
    tq, tk = ATT_TQ, ATT_TK
    group = ATT_HEADS // ATT_KV_HEADS
    qi = pl.program_id(1)
    n_kt = (qi * tq + tq + tk - 1) // tk
    row = qi * tq + lax.broadcasted_iota(jnp.int32, (tq, 1), 0)
    lane_col = lax.broadcasted_iota(jnp.int32, (1, tk), 1)
    for h in range(IDX_HEADS):
        wb_ref[h] = jnp.broadcast_to(iw_ref[:, h:h + 1], (tq, LANES))
    for h in range(ATT_HEADS):
        qs_ref[h * tq:(h + 1) * tq, :] = aq_ref[:, h * HEAD_DIM:(h + 1) * HEAD_DIM]

    n_full = n_kt // 2
    n_pair = (n_kt + 1) // 2

    def score_step(i, width):
        kblk = ik_ref[pl.ds(pl.multiple_of(i * 2 * tk, 2 * tk), width), :]
        acc = jnp.zeros((tq, width), F32)
        for h in range(IDX_HEADS):
            z = _nt_dot(iq_ref[:, h * HEAD_DIM:(h + 1) * HEAD_DIM], kblk)
            wb = wb_ref[h]
            acc = acc + jnp.concatenate([wb] * (width // LANES), axis=1) * jnp.maximum(z, 0.0)
        keys = []
        for u in range(width // tk):
            kt = 2 * i + u
            causal = (kt * tk + lane_col) <= row
            keys.append(jnp.where(causal, _sortable(acc[:, u * tk:(u + 1) * tk]), INT_MIN))
            sc_ref[kt] = keys[u]
        if width == 2 * tk:
            sct_ref[i] = jnp.concatenate(keys, axis=1).T
        else:
            sct_ref[i, 0:tk, :] = keys[0].T
            sct_ref[i, tk:2 * tk, :] = jnp.full((tk, tq), INT_MIN, jnp.int32)
            sc_ref[2 * i + 1] = jnp.full((tq, tk), INT_MIN, jnp.int32)

    def score_pair(i, carry):
        score_step(i, 2 * tk)
        return carry

    lax.fori_loop(0, n_full, score_pair, 0)

    @pl.when(n_kt % 2 == 1)
    def _():
        score_step(n_full, tk)

    def count_where(pred):
        def body(i, c):
            for u in range(2):
                kt = 2 * i + u
                hit = jnp.where(pred(sc_ref[kt], kt), 1.0, 0.0)
                for cc in range(tk // LANES):
                    c = c + hit[:, cc * LANES:(cc + 1) * LANES]
            return c
        c = lax.fori_loop(0, n_pair, body, jnp.zeros((tq, LANES), F32))
        return jnp.sum(c, axis=-1, keepdims=True)

    kf = float(topk)

    def count_ge_t(t):
        def body(i, c):
            hit = jnp.where(sct_ref[i] >= t, 1.0, 0.0)
            return c + jnp.sum(hit.reshape(2 * tk // 64, 8, 8, tq), axis=0)
        c = lax.fori_loop(0, n_pair, body, jnp.zeros((8, 8, tq), F32))
        return jnp.sum(jnp.sum(c, axis=0), axis=0, keepdims=True)

    c0 = count_ge_t(jnp.zeros((1, tq), jnp.int32))
    t0 = jnp.where(c0 >= kf, 0, INT_MIN).astype(jnp.int32)
    n_all = jnp.full((1, tq), float(seq), F32)

    def bit_step(i, state):
        t, ct, open_rows = state
        cand = t | jnp.left_shift(jnp.int32(1), 30 - i)
        c = count_ge_t(cand)
        take = jnp.logical_and(c >= kf, open_rows > 0.0)
        t = jnp.where(take, cand, t)
        ct = jnp.where(take, c, ct)
        open_rows = jnp.where(c == kf, 0.0, open_rows)
        return t, ct, open_rows

    def any_open(open_rows):
        return (jnp.max(open_rows) > 0.0).astype(jnp.int32)

    def tail_cond(state):
        i, _, go = state
        return jnp.logical_and(i < 31, go == 1)

    def tail_step(state):
        i, inner, _ = state
        inner = bit_step(i + 1, bit_step(i, inner))
        return i + 2, inner, any_open(inner[2])

    state = (t0, jnp.where(c0 >= kf, c0, n_all), jnp.where(c0 == kf, 0.0, 1.0))
    state = lax.fori_loop(0, BISECT_UNCHECKED_BITS, bit_step, state)
    _, (thr_t, cnt_t, open_t), _ = lax.while_loop(
        tail_cond, tail_step, (jnp.int32(BISECT_UNCHECKED_BITS), state, any_open(state[2])))

    eye = lax.broadcasted_iota(jnp.int32, (tq, tq), 0) == lax.broadcasted_iota(jnp.int32, (tq, tq), 1)

    def to_col(v):
        return jnp.sum(jnp.where(eye, v, 0.0), axis=1, keepdims=True)

    thr = ((to_col((thr_t >> 16).astype(F32)).astype(jnp.int32) << 16)
           | to_col((thr_t & 0xFFFF).astype(F32)).astype(jnp.int32))
    cnt_thr = to_col(cnt_t)
    open_rows = to_col(open_t)

    excess = jnp.logical_and(jnp.logical_and(open_rows > 0.0, cnt_thr > kf), thr > INT_MIN)

    @pl.when(jnp.max(jnp.where(excess, 1.0, 0.0)) > 0.0)
    def _():
        need = kf - count_where(lambda key, kt: key > thr)

        def ties_below(j):
            return count_where(
                lambda key, kt: jnp.logical_and(key == thr, (kt * tk + lane_col) < j))

        nbits = max(seq - 1, 1).bit_length()

        def idx_step(i, j0):
            cand = j0 | jnp.left_shift(jnp.int32(1), nbits - 1 - i)
            return jnp.where(ties_below(cand) < need, cand, j0)

        j0 = lax.fori_loop(0, nbits, idx_step, jnp.zeros((tq, 1), jnp.int32))

        def demote(kt, carry):
            key = sc_ref[kt]
            late_tie = jnp.logical_and(key == thr, (kt * tk + lane_col) > j0)
            sc_ref[kt] = jnp.where(jnp.logical_and(excess, late_tie), INT_MIN, key)
            return carry

        lax.fori_loop(0, n_kt, demote, 0)

    thr = jnp.maximum(thr, INT_MIN + 1)

    m_ref[...] = jnp.full(m_ref.shape, MASKED, F32)
    acc_ref[...] = jnp.zeros(acc_ref.shape, F32)
    def kv_step(i, tk2):
        r0 = pl.multiple_of(i * 2 * tk, 2 * tk)
        ones = jnp.ones((tk2, HEAD_DIM), BF16)
        keep = jnp.concatenate([sc_ref[2 * i + u] >= thr for u in range(tk2 // tk)], axis=1)
        bias = jnp.where(keep, 0.0, MASKED).astype(BF16)
        for g in range(ATT_KV_HEADS):
            kblk = ak_ref[pl.ds(r0, tk2), g * HEAD_DIM:(g + 1) * HEAD_DIM]
            vblk = av_ref[pl.ds(r0, tk2), g * HEAD_DIM:(g + 1) * HEAD_DIM]
            v1 = jnp.concatenate([vblk, ones], axis=1)
            for part in range(group):
                rows = slice(part * tq, (part + 1) * tq)
                q1 = qs_ref[(g * group + part) * tq:(g * group + part + 1) * tq, :]
                s = _nt_dot(q1, kblk).astype(BF16) + bias
                m_old = m_ref[g, rows, :]
                m_new = jnp.maximum(m_old, jnp.max(s, axis=-1, keepdims=True).astype(F32))
                alpha = jnp.exp2(m_old - m_new)
                shift = m_new.astype(BF16)
                p = jnp.exp2(s - jnp.concatenate([shift] * (tk2 // LANES), axis=1))
                pv = jnp.dot(p, v1, preferred_element_type=F32)
                acc_ref[g, rows, :] = jnp.concatenate([alpha, alpha], axis=1) * acc_ref[g, rows, :] + pv
                m_ref[g, rows, :] = m_new

    def kv_pair(i, carry):
        kv_step(i, 2 * tk)
        return carry

    lax.fori_loop(0, n_full, kv_pair, 0)

    @pl.when(n_kt % 2 == 1)
    def _():
        kv_step(n_full, tk)

    for g in range(ATT_KV_HEADS):
        og = acc_ref[g]
        og = og[:, :HEAD_DIM] / og[:, HEAD_DIM:]
        for r in range(group):
            h = g * group + r
            o_ref[:, h * HEAD_DIM:(h + 1) * HEAD_DIM] = og[r * tq:(r + 1) * tq, :].astype(BF16)


def _attention(proj, ik, sm, batch, seq, cols):
    n = proj.shape[0]
    tq, tk = ATT_TQ, ATT_TK
    nq = seq // tq
    width = ATT_HEADS * HEAD_DIM
    kvw = ATT_KV_HEADS * HEAD_DIM
    topk = min(TOPK_MAX, seq // 4)
    group = ATT_HEADS // ATT_KV_HEADS
    kern = functools.partial(_attn_kernel, topk=topk, seq=seq)
    once = pl.Buffered(1)
    return pl.pallas_call(
        kern,
        grid=(batch, nq),
        in_specs=[
            pl.BlockSpec((tq, width), lambda b, q: (b * nq + q, cols["aq"] // width)),
            pl.BlockSpec((tq, width), lambda b, q: (b * nq + q, cols["iq"] // width)),
            pl.BlockSpec((seq, kvw), lambda b, q: (b, cols["ak"] // kvw), pipeline_mode=once),
            pl.BlockSpec((seq, kvw), lambda b, q: (b, cols["av"] // kvw), pipeline_mode=once),
            pl.BlockSpec((seq, LANES), lambda b, q: (b, 0), pipeline_mode=once),
            pl.BlockSpec((tq, LANES), lambda b, q: (b * nq + q, 0)),
        ],
        out_specs=pl.BlockSpec((tq, width), lambda b, q: (b * nq + q, 0)),
        out_shape=jax.ShapeDtypeStruct((n, width), BF16),
        scratch_shapes=[
            pltpu.VMEM((seq // tk, tq, tk), jnp.int32),
            pltpu.VMEM((seq // (2 * tk), 2 * tk, tq), jnp.int32),
            pltpu.VMEM((ATT_HEADS * tq, HEAD_DIM), BF16),
            pltpu.VMEM((IDX_HEADS, tq, LANES), F32),
            pltpu.VMEM((ATT_KV_HEADS, group * tq, LANES), F32),
            pltpu.VMEM((ATT_KV_HEADS, group * tq, 2 * HEAD_DIM), F32),
        ],
        compiler_params=pltpu.CompilerParams(
            dimension_semantics=("arbitrary", "arbitrary"), vmem_limit_bytes=VMEM_LIMIT),
        name="dsa_attention",
    )(proj, proj, proj, proj, ik, sm)


GLA_T = 256


def _gla_kernel(q_ref, k_ref, v_ref, r_ref, lr_ref, wg_ref, bg_ref, gn_ref, o_ref, st_ref, la_ref,
                *, dk, dv):
    c_len = GLA_CHUNK
    hi = lax.Precision.HIGHEST

    @pl.when(pl.program_id(1) == 0)
    def _():
        st_ref[...] = jnp.zeros(st_ref.shape, F32)

    ri = lax.broadcasted_iota(jnp.int32, (c_len, c_len), 0)
    ci = lax.broadcasted_iota(jnp.int32, (c_len, c_len), 1)
    lower = ri >= ci
    tril = jnp.where(lower, 1.0, 0.0).astype(F32)
    qscale = dk ** -0.5

    gl = jnp.dot(lr_ref[...], wg_ref[...], preferred_element_type=F32, precision=hi) + bg_ref[...]
    la_ref[...] = (jnp.minimum(gl, 0.0) - jnp.log(1.0 + jnp.exp(-jnp.abs(gl)))) / GLA_GATE_TAU

    for c in range(GLA_T // c_len):
        rows = slice(c * c_len, (c + 1) * c_len)
        b = jnp.dot(tril, la_ref[rows, :], preferred_element_type=F32, precision=hi)
        b_last = b[c_len - 1:c_len, :]
        k = k_ref[rows, :].astype(F32)
        q_dec = (q_ref[rows, :].astype(F32) * qscale * jnp.exp(b)).astype(BF16)
        k_inv = (k * jnp.exp(-b)).astype(BF16)
        k_tail = k * jnp.exp(b_last - b)
        for h in range(GLA_HEADS):
            ks = slice(h * dk, (h + 1) * dk)
            vs = slice(h * dv, (h + 1) * dv)
            v = v_ref[rows, vs]
            a = jnp.where(lower, _nt_dot(q_dec[:, ks], k_inv[:, ks]), 0.0)
            state = st_ref[h]
            o = (jnp.dot(a.astype(BF16), v, preferred_element_type=F32)
                 + jnp.dot(q_dec[:, ks], state.astype(BF16), preferred_element_type=F32))
            dec = jnp.exp(b[:, ks].T[:, c_len - 1:c_len])
            upd = jnp.dot(k_tail[:, ks].T.astype(BF16), v, preferred_element_type=F32)
            for cc in range(dv // LANES):
                sl = slice(cc * LANES, (cc + 1) * LANES)
                st_ref[h, :, sl] = dec * state[:, sl] + upd[:, sl]
            r = r_ref[rows, vs].astype(F32)
            o_ref[rows, vs] = (_rms(o) * gn_ref[...] * (r * _sigmoid(r))).astype(BF16)


def _gla(proj, sm, wg, bg, gn, batch, seq, cols):
    n = proj.shape[0]
    dk = wg.shape[1] // GLA_HEADS
    dv = gn.shape[1]
    t = min(GLA_T, seq)
    assert t == GLA_T
    ns = seq // t
    kw, vw = GLA_HEADS * dk, GLA_HEADS * dv
    kern = functools.partial(_gla_kernel, dk=dk, dv=dv)
    return pl.pallas_call(
        kern,
        grid=(batch, ns),
        in_specs=[
            pl.BlockSpec((t, kw), lambda b, s: (b * ns + s, cols["gq"] // kw)),
            pl.BlockSpec((t, kw), lambda b, s: (b * ns + s, cols["gk"] // kw)),
            pl.BlockSpec((t, vw), lambda b, s: (b * ns + s, cols["gv"] // vw)),
            pl.BlockSpec((t, vw), lambda b, s: (b * ns + s, cols["gr"] // vw)),
            pl.BlockSpec((t, LANES), lambda b, s: (b * ns + s, 0)),
            pl.BlockSpec((LANES, kw), lambda b, s: (0, 0)),
            pl.BlockSpec((1, kw), lambda b, s: (0, 0)),
            pl.BlockSpec((1, dv), lambda b, s: (0, 0)),
        ],
        out_specs=pl.BlockSpec((t, vw), lambda b, s: (b * ns + s, 0)),
        out_shape=jax.ShapeDtypeStruct((n, vw), BF16),
        scratch_shapes=[pltpu.VMEM((GLA_HEADS, dk, dv), F32), pltpu.VMEM((t, kw), F32)],
        compiler_params=pltpu.CompilerParams(
            dimension_semantics=("arbitrary", "arbitrary"),
            vmem_limit_bytes=VMEM_LIMIT),
        name="gla",
    )(proj, proj, proj, proj, sm, wg, bg, gn)


def _merge_kernel(ya_ref, yg_ref, wa_ref, wg_ref, ma_ref, mg_ref, o_ref):
    a = jnp.dot(ya_ref[...], wa_ref[...], preferred_element_type=F32)
    g = jnp.dot(yg_ref[...], wg_ref[...], preferred_element_type=F32)
    gate_a = _sigmoid(ma_ref[...].astype(F32))
    gate_g = _sigmoid(mg_ref[...].astype(F32))
    o_ref[...] = (gate_a * a + gate_g * g).astype(BF16)


def _merge(y_att, y_gla, w_att, w_gla, proj, cols):
    n, d_att = y_att.shape
    d_gla = y_gla.shape[1]
    d = w_att.shape[1]
    tm, tn = PROJ_TM, PROJ_TN
    return pl.pallas_call(
        _merge_kernel,
        grid=(n // tm, d // tn),
        in_specs=[
            pl.BlockSpec((tm, d_att), lambda i, j: (i, 0)),
            pl.BlockSpec((tm, d_gla), lambda i, j: (i, 0)),
            pl.BlockSpec((d_att, tn), lambda i, j: (0, j)),
            pl.BlockSpec((d_gla, tn), lambda i, j: (0, j)),
            pl.BlockSpec((tm, tn), lambda i, j: (i, cols["m_att"] // tn + j)),
            pl.BlockSpec((tm, tn), lambda i, j: (i, cols["m_gla"] // tn + j)),
        ],
        out_specs=pl.BlockSpec((tm, tn), lambda i, j: (i, j)),
        out_shape=jax.ShapeDtypeStruct((n, d), BF16),
        compiler_params=pltpu.CompilerParams(
            dimension_semantics=("arbitrary", "arbitrary"), vmem_limit_bytes=VMEM_LIMIT),
        name="merge",
    )(y_att, y_gla, w_att, w_gla, proj, proj)


def _resproj_kernel(m_ref, w_ref, x_ref, gate_ref, o_ref):
    y = jnp.dot(m_ref[...], w_ref[...], preferred_element_type=F32)
    o_ref[...] = x_ref[...] + gate_ref[...] * y


def _resproj(merged, w, x2, gate, seq):
    n, d = x2.shape
    tm, tn = PROJ_TM, PROJ_TN
    per_b = seq // tm
    return pl.pallas_call(
        _resproj_kernel,
        grid=(n // tm, d // tn),
        in_specs=[
            pl.BlockSpec((tm, merged.shape[1]), lambda i, j: (i, 0)),
            pl.BlockSpec((merged.shape[1], tn), lambda i, j: (0, j)),
            pl.BlockSpec((tm, tn), lambda i, j: (i, j)),
            pl.BlockSpec((None, 1, tn), lambda i, j: (i // per_b, 0, j)),
        ],
        out_specs=pl.BlockSpec((tm, tn), lambda i, j: (i, j)),
        out_shape=jax.ShapeDtypeStruct((n, d), F32),
        compiler_params=pltpu.CompilerParams(
            dimension_semantics=("arbitrary", "arbitrary"), vmem_limit_bytes=VMEM_LIMIT),
        name="resproj",
    )(merged, w, x2, gate)


FFN_TM = 512
FFN_TF = 1024


def _swiglu_step(hn, w1, w3, w2):
    a = jnp.dot(hn, w1, preferred_element_type=F32)
    b = jnp.dot(hn, w3, preferred_element_type=F32)
    return a * _sigmoid(a) * b, w2


def _ffn_kernel(x_ref, g_ref, sc_ref, sh_ref, w1_ref, w3_ref, w2_ref, gate_ref, o_ref,
                hn_ref, acc_ref):
    j = pl.program_id(1)

    @pl.when(j == 0)
    def _():
        hn_ref[...] = _modulate(x_ref[...], g_ref[...], sc_ref[...], sh_ref[...]).astype(BF16)
        acc_ref[...] = jnp.zeros(acc_ref.shape, F32)

    act, w2 = _swiglu_step(hn_ref[...], w1_ref[...], w3_ref[...], w2_ref[...])
    acc_ref[...] += jnp.dot(act.astype(BF16), w2, preferred_element_type=F32)

    @pl.when(j == pl.num_programs(1) - 1)
    def _():
        o_ref[...] = x_ref[...] + gate_ref[...] * acc_ref[...]


def _ffn(x2, g, scale, shift, w1, w3, w2, gate, seq):
    n, d = x2.shape
    dff = w1.shape[1]
    tm, tf = min(FFN_TM, seq), FFN_TF
    per_b = seq // tm
    vec = pl.BlockSpec((None, 1, d), lambda i, j: (i // per_b, 0, 0))
    return pl.pallas_call(
        _ffn_kernel,
        grid=(n // tm, dff // tf),
        in_specs=[
            pl.BlockSpec((tm, d), lambda i, j: (i, 0)),
            pl.BlockSpec((1, d), lambda i, j: (0, 0)),
            vec, vec,
            pl.BlockSpec((d, tf), lambda i, j: (0, j)),
            pl.BlockSpec((d, tf), lambda i, j: (0, j)),
            pl.BlockSpec((tf, d), lambda i, j: (j, 0)),
            vec,
        ],
        out_specs=pl.BlockSpec((tm, d), lambda i, j: (i, 0)),
        out_shape=jax.ShapeDtypeStruct((n, d), F32),
        scratch_shapes=[pltpu.VMEM((tm, d), BF16), pltpu.VMEM((tm, d), F32)],
        compiler_params=pltpu.CompilerParams(
            dimension_semantics=("arbitrary", "arbitrary"), vmem_limit_bytes=VMEM_LIMIT),
        name="ffn_swiglu",
    )(x2, g, scale, shift, w1, w3, w2, gate)


MOE_TM = 512
ROUTE_TM = 1024
ROW_COPY_UNROLL = 8


def _router_kernel(x_ref, g_ref, sc_ref, sh_ref, wr_ref, hn_ref, rt_ref):
    lane = lax.broadcasted_iota(jnp.int32, (1, LANES), 1)
    lane_f = lane.astype(F32)
    h = _modulate(x_ref[...], g_ref[...], sc_ref[...], sh_ref[...])
    hn_ref[...] = h
    logits = jnp.dot(h, wr_ref[...], preferred_element_type=F32, precision=lax.Precision.HIGHEST)
    logits = jnp.where(lane < N_EXPERTS, logits, -jnp.inf)
    m1 = jnp.max(logits, axis=-1, keepdims=True)
    i1 = jnp.min(jnp.where(logits == m1, lane_f, float(LANES)), axis=-1, keepdims=True)
    rest = jnp.where(lane_f == i1, -jnp.inf, logits)
    m2 = jnp.max(rest, axis=-1, keepdims=True)
    i2 = jnp.min(jnp.where(rest == m2, lane_f, float(LANES)), axis=-1, keepdims=True)
    e2 = jnp.exp(m2 - m1)
    p1 = 1.0 / (1.0 + e2)
    p2 = e2 / (1.0 + e2)
    rt_ref[...] = (jnp.where(lane == 0, i1, 0.0) + jnp.where(lane == 1, i2, 0.0)
                   + jnp.where(lane == 2, p1, 0.0) + jnp.where(lane == 3, p2, 0.0))


def _router(x2, g, scale, shift, w_router, seq):
    n, d = x2.shape
    tm = min(ROUTE_TM, seq)
    per_b = seq // tm
    vec = pl.BlockSpec((None, 1, d), lambda i: (i // per_b, 0, 0))
    return pl.pallas_call(
        _router_kernel,
        grid=(n // tm,),
        in_specs=[
            pl.BlockSpec((tm, d), lambda i: (i, 0)),
            pl.BlockSpec((1, d), lambda i: (0, 0)),
            vec, vec,
            pl.BlockSpec((d, LANES), lambda i: (0, 0)),
        ],
        out_specs=[pl.BlockSpec((tm, d), lambda i: (i, 0)),
                   pl.BlockSpec((tm, LANES), lambda i: (i, 0))],
        out_shape=[jax.ShapeDtypeStruct((n, d), F32), jax.ShapeDtypeStruct((n, LANES), F32)],
        compiler_params=pltpu.CompilerParams(
            dimension_semantics=("arbitrary",), vmem_limit_bytes=VMEM_LIMIT),
        name="moe_router",
    )(x2, g, scale, shift, w_router)


def _route_tables(rt, n, n_exp, tm):
    rows = 2 * n + n_exp * tm
    n_tiles = rows // tm
    ef = rt[:, :2].astype(jnp.int32).T.reshape(-1)
    onehot = (ef[:, None] == jnp.arange(n_exp, dtype=jnp.int32)[None, :]).astype(jnp.int32)
    csum = jnp.cumsum(onehot, axis=0)
    rank = jnp.take_along_axis(csum, ef[:, None], axis=1)[:, 0] - 1
    cnt = csum[-1]
    padded = ((cnt + tm - 1) // tm) * tm
    ends = jnp.cumsum(padded)
    dest = (ends - padded)[ef] + rank
    total = ends[-1]
    tile_start = jnp.arange(n_tiles, dtype=jnp.int32) * tm
    tile_valid = (tile_start < total).astype(jnp.int32)
    tile_expert = jnp.sum((tile_start[:, None] >= ends[None, :]).astype(jnp.int32), axis=1)
    last_tile = jnp.maximum(total // tm - 1, 0)
    tile_expert = jnp.where(tile_valid == 1, tile_expert, tile_expert[last_tile])
    tile_rows = jnp.where(tile_valid == 1, jnp.arange(n_tiles, dtype=jnp.int32), last_tile)
    spare_lo = jnp.concatenate([ends - padded + cnt, total[None]]).astype(jnp.int32)
    spare_hi = jnp.concatenate([ends, jnp.full((1,), rows, ends.dtype)]).astype(jnp.int32)
    return dict(tile_expert=tile_expert, tile_valid=tile_valid, tile_rows=tile_rows, dest=dest,
                spare_lo=spare_lo, spare_hi=spare_hi, rows=rows)


def _row_copies(copy, count):
    assert count % ROW_COPY_UNROLL == 0

    def body(blk, carry):
        for u in range(ROW_COPY_UNROLL):
            copy(blk * ROW_COPY_UNROLL + u).start(priority=u % 2)
        return carry
    lax.fori_loop(0, count // ROW_COPY_UNROLL, body, 0)


def _dispatch_kernel(dest_ref, lo_ref, hi_ref, hn_ref, xs_hbm, zero_ref, sem, zsem, *, n, n_spare):
    i = pl.program_id(0)
    tm = hn_ref.shape[0]

    @pl.when(i == 0)
    def _():
        zero_ref[...] = jnp.zeros(zero_ref.shape, F32)
        for e in range(n_spare):
            def zero_row(r):
                return pltpu.make_async_copy(zero_ref.at[pl.ds(0, 1)], xs_hbm.at[pl.ds(r, 1)], zsem)

            def start(r, carry):
                zero_row(r).start()
                return carry

            def drain(r, carry):
                zero_row(r).wait()
                return carry
            lax.fori_loop(lo_ref[e], hi_ref[e], start, 0)
            lax.fori_loop(lo_ref[e], hi_ref[e], drain, 0)

    for k in range(2):
        _row_copies(lambda r, k=k: pltpu.make_async_copy(
            hn_ref.at[pl.ds(r, 1)], xs_hbm.at[pl.ds(dest_ref[k * n + i * tm + r], 1)], sem), tm)
    for k in range(2):
        pltpu.make_async_copy(hn_ref, xs_hbm.at[pl.ds(0, tm)], sem).wait()


def _dispatch(hn, tables):
    n, d = hn.shape
    tm = MOE_TM
    kern = functools.partial(_dispatch_kernel, n=n, n_spare=tables["spare_lo"].shape[0])
    grid_spec = pltpu.PrefetchScalarGridSpec(
        num_scalar_prefetch=3,
        grid=(n // tm,),
        in_specs=[pl.BlockSpec((tm, d), lambda i, dest, lo, hi: (i, 0))],
        out_specs=pl.BlockSpec(memory_space=pl.ANY),
        scratch_shapes=[pltpu.VMEM((8, d), F32), pltpu.SemaphoreType.DMA(()),
                        pltpu.SemaphoreType.DMA(())],
    )
    return pl.pallas_call(
        kern,
        grid_spec=grid_spec,
        out_shape=jax.ShapeDtypeStruct((tables["rows"], d), F32),
        compiler_params=pltpu.CompilerParams(
            dimension_semantics=("arbitrary",), vmem_limit_bytes=VMEM_LIMIT),
        name="moe_dispatch",
    )(tables["dest"], tables["spare_lo"], tables["spare_hi"], hn)


def _moe_grouped_kernel(te_ref, tv_ref, tr_ref, x_ref, w1_ref, w3_ref, w2_ref, y_ref, xb_ref):
    t = pl.program_id(0)
    j = pl.program_id(1)
    valid = tv_ref[t] == 1

    @pl.when(j == 0)
    def _():
        xb_ref[...] = x_ref[...].astype(BF16)
        y_ref[...] = jnp.zeros(y_ref.shape, F32)

    @pl.when(valid)
    def _():
        act, w2 = _swiglu_step(xb_ref[...], w1_ref[...], w3_ref[...], w2_ref[...])
        y_ref[...] += jnp.dot(act.astype(BF16), w2, preferred_element_type=F32)


def _moe_grouped(xs, tables, w1, w3, w2):
    rows, d = xs.shape
    dff = w1.shape[2]
    tm, tf = MOE_TM, FFN_TF
    nj = dff // tf

    def jj(t, j, tv):
        return j * tv[t] + (nj - 1) * (1 - tv[t])

    grid_spec = pltpu.PrefetchScalarGridSpec(
        num_scalar_prefetch=3,
        grid=(rows // tm, nj),
        in_specs=[
            pl.BlockSpec((tm, d), lambda t, j, te, tv, tr: (tr[t], 0)),
            pl.BlockSpec((None, d, tf), lambda t, j, te, tv, tr: (te[t], 0, jj(t, j, tv))),
            pl.BlockSpec((None, d, tf), lambda t, j, te, tv, tr: (te[t], 0, jj(t, j, tv))),
            pl.BlockSpec((None, tf, d), lambda t, j, te, tv, tr: (te[t], jj(t, j, tv), 0)),
        ],
        out_specs=pl.BlockSpec((tm, d), lambda t, j, te, tv, tr: (t, 0)),
        scratch_shapes=[pltpu.VMEM((tm, d), BF16)],
    )
    return pl.pallas_call(
        _moe_grouped_kernel,
        grid_spec=grid_spec,
        out_shape=jax.ShapeDtypeStruct((rows, d), F32),
        compiler_params=pltpu.CompilerParams(
            dimension_semantics=("arbitrary", "arbitrary"), vmem_limit_bytes=VMEM_LIMIT),
        name="moe_grouped",
    )(tables["tile_expert"], tables["tile_valid"], tables["tile_rows"], xs, w1, w3, w2)


def _moe_combine_kernel(dest_ref, x_ref, rt_ref, gate_ref, y_hbm, o_ref, y0_ref, y1_ref, sem, *, n):
    i = pl.program_id(0)
    tm = x_ref.shape[0]
    for k, yk_ref in enumerate((y0_ref, y1_ref)):
        _row_copies(lambda r, k=k, yk_ref=yk_ref: pltpu.make_async_copy(
            y_hbm.at[pl.ds(dest_ref[k * n + i * tm + r], 1)], yk_ref.at[pl.ds(r, 1)], sem), tm)
    for yk_ref in (y0_ref, y1_ref):
        pltpu.make_async_copy(y_hbm.at[pl.ds(0, tm)], yk_ref, sem).wait()
    p1 = rt_ref[:, 2:3]
    p2 = rt_ref[:, 3:4]
    o_ref[...] = x_ref[...] + gate_ref[...] * (p1 * y0_ref[...] + p2 * y1_ref[...])


def _moe_combine(x2, y, rt, gate, dest, seq):
    n, d = x2.shape
    tm = min(FFN_TM, seq)
    per_b = seq // tm
    kern = functools.partial(_moe_combine_kernel, n=n)
    grid_spec = pltpu.PrefetchScalarGridSpec(
        num_scalar_prefetch=1,
        grid=(n // tm,),
        in_specs=[
            pl.BlockSpec((tm, d), lambda i, dest: (i, 0)),
            pl.BlockSpec((tm, LANES), lambda i, dest: (i, 0)),
            pl.BlockSpec((None, 1, d), lambda i, dest: (i // per_b, 0, 0)),
            pl.BlockSpec(memory_space=pl.ANY),
        ],
        out_specs=pl.BlockSpec((tm, d), lambda i, dest: (i, 0)),
        scratch_shapes=[pltpu.VMEM((tm, d), F32), pltpu.VMEM((tm, d), F32),
                        pltpu.SemaphoreType.DMA(())],
    )
    return pl.pallas_call(
        kern,
        grid_spec=grid_spec,
        out_shape=jax.ShapeDtypeStruct((n, d), F32),
        compiler_params=pltpu.CompilerParams(
            dimension_semantics=("arbitrary",), vmem_limit_bytes=VMEM_LIMIT),
        name="moe_combine",
    )(dest, x2, rt, gate, y)


def _moe(x2, g, scale, shift, w_router, w1, w3, w2, gate, seq):
    n = x2.shape[0]
    hn, rt = _router(x2, g, scale, shift, w_router, seq)
    tables = _route_tables(rt, n, w1.shape[0], MOE_TM)
    xs = _dispatch(hn, tables)
    y = _moe_grouped(xs, tables, w1, w3, w2)
    return _moe_combine(x2, y, rt, gate, tables["dest"], seq)


def _prep_in_proj(w_in, q_norm, k_norm, idx_k_norm, d):
    sizes = dict(aq=ATT_HEADS * HEAD_DIM, ak=ATT_KV_HEADS * HEAD_DIM, av=ATT_KV_HEADS * HEAD_DIM,
                 iq=IDX_HEADS * HEAD_DIM, ik=HEAD_DIM, iw=IDX_HEADS,
                 gq=d // 2, gk=d // 2, gv=d, glr=GLA_GATE_RANK, gr=d, m_att=d, m_gla=d)
    src, acc = {}, 0
    for name in ("aq", "ak", "av", "iq", "ik", "iw", "gq", "gk", "gv", "glr", "gr", "m_att", "m_gla"):
        src[name] = (acc, sizes[name])
        acc += sizes[name]
    order = ("aq", "iq", "gv", "gr", "m_att", "m_gla", "ak", "av", "gq", "gk")
    cols, parts, off = {}, [], 0
    for name in order:
        s, width = src[name]
        cols[name] = off
        parts.append(w_in[:, s:s + width])
        off += width
    w_main = jnp.concatenate(parts, axis=1).astype(BF16)

    def seg(name):
        s, width = src[name]
        return w_in[:, s:s + width]

    pad = jnp.zeros((d, LANES - IDX_HEADS - GLA_GATE_RANK), w_in.dtype)
    w_small = jnp.concatenate([seg("ik"), seg("iw"), seg("glr"), pad], axis=1).astype(BF16)

    att_scale = HEAD_DIM ** -0.5 * LOG2_E
    idx_scale = (HEAD_DIM ** -0.5) * (IDX_HEADS ** -0.5)
    cg = jnp.ones((off,), F32)
    cg = cg.at[cols["aq"]:cols["aq"] + sizes["aq"]].set(jnp.tile(q_norm * att_scale, ATT_HEADS))
    cg = cg.at[cols["ak"]:cols["ak"] + sizes["ak"]].set(jnp.tile(k_norm, ATT_KV_HEADS))
    cg2 = jnp.concatenate([idx_k_norm, jnp.full((IDX_HEADS,), idx_scale, F32),
                           jnp.ones((LANES - IDX_HEADS,), F32)])
    norm_tiles = tuple(range(cols["aq"] // PROJ_SEG, (cols["aq"] + sizes["aq"]) // PROJ_SEG)) + \
        tuple(range(cols["ak"] // PROJ_SEG, (cols["ak"] + sizes["ak"]) // PROJ_SEG))
    return w_main, w_small, cg.reshape(1, -1), cg2.reshape(1, -1), cols, norm_tiles


def kernel(x, c, ada_w, ada_b, norm_mix, norm_ffn, w_in, q_norm, k_norm, idx_k_norm, w_gla_gate,
           b_gla_gate, gla_out_norm, w_out_attn, w_out_gla, w_out, ffn_w1, ffn_w3, ffn_w2,
           moe_router, moe_w1, moe_w3, moe_w2):
    batch, seq, d = x.shape
    depth = ada_w.shape[0]
    n = batch * seq
    for tile in (PROJ_TM, 2 * ATT_TK, GLA_T, ROUTE_TM, FFN_TM, MOE_TM):
        assert seq % tile == 0, (seq, tile)
    assert d % PROJ_TN == 0
    x2 = x.reshape(n, d)

    mod_all = _ada(c, ada_w, ada_b)

    for layer in range(depth):
        mod = mod_all[layer, :batch].reshape(batch, 6, 1, d)
        shift_m, scale_m, gate_m = mod[:, 0], mod[:, 1], mod[:, 2]
        shift_f, scale_f, gate_f = mod[:, 3], mod[:, 4], mod[:, 5]

        w_main, w_small, cg, cg2, cols, norm_tiles = _prep_in_proj(
            w_in[layer], q_norm[layer], k_norm[layer], idx_k_norm[layer], d)
        proj, ik, sm = _modproj(x2, norm_mix[layer].reshape(1, d), scale_m, shift_m,
                                w_main, cg, w_small, cg2, seq, norm_tiles)
        y_att = _attention(proj, ik, sm, batch, seq, cols)
        wg = jnp.zeros((LANES, w_gla_gate.shape[2]), F32).at[
            GLA_GATE_RANK:2 * GLA_GATE_RANK].set(w_gla_gate[layer])
        y_gla = _gla(proj, sm, wg, b_gla_gate[layer].reshape(1, -1),
                     gla_out_norm[layer].reshape(1, -1), batch, seq, cols)
        merged = _merge(y_att, y_gla, w_out_attn[layer].astype(BF16),
                        w_out_gla[layer].astype(BF16), proj, cols)
        x2 = _resproj(merged, w_out[layer].astype(BF16), x2, gate_m, seq)

        g_f = norm_ffn[layer].reshape(1, d)
        i = layer // 2
        if layer % 2 == 0:
            x2 = _ffn(x2, g_f, scale_f, shift_f, ffn_w1[i].astype(BF16), ffn_w3[i].astype(BF16),
                      ffn_w2[i].astype(BF16), gate_f, seq)
        else:
            w_r = jnp.zeros((d, LANES), F32).at[:, :N_EXPERTS].set(moe_router[i])
            x2 = _moe(x2, g_f, scale_f, shift_f, w_r, moe_w1[i].astype(BF16),
                      moe_w3[i].astype(BF16), moe_w2[i].astype(BF16), gate_f, seq)
    return x2.reshape(batch, seq, d)
```
